```python
import jax, jax.numpy as jnp
from jax import lax
import numpy as np

D_MODEL = 1024
BATCH = 16
SEQ = 2048
DEPTH = 2
DEC_BATCH = 32
DEC_SEQ = 16
PAST_LEN = 2048

CHUNK = 64
W_A = D_MODEL
GMLP_CHUNK = 128
GMLP_HEADS = 4
GMLP_HEAD_DIM = W_A // GMLP_HEADS
W_B = D_MODEL
POOL_WINDOWS = (2, 4, 8, 16)
POOL_GROUPS = len(POOL_WINDOWS)
POOL_GROUP_DIM = W_B // POOL_GROUPS
POOL_STATE = max(POOL_WINDOWS) - 1
W_C = D_MODEL
CONV_K = 31
CONV_STATE = CONV_K - 1
N_BRANCH = 3
OFF_U = 0
OFF_V = OFF_U + W_A
OFF_B = OFF_V + W_A
OFF_C = OFF_B + W_B
OFF_G = OFF_C + 2 * W_C
IN_COLS = OFF_G + N_BRANCH * D_MODEL
N_EXPERTS = 32
TOP_K = 4
D_FF = D_MODEL
SWIGLU_LIMIT = 7.0
SWIGLU_ALPHA = 1.702
MOE_BLOCK = 128
EPS = 1e-6

kernel_name = 'streaming_gmlp_pool_conformer_moe'


def rmsnorm(x, g):
    xf = x.astype(jnp.float32)
    y = xf * lax.rsqrt(jnp.mean(xf * xf, axis=-1, keepdims=True) + EPS)
    return (y * g.astype(jnp.float32)).astype(x.dtype)


def layernorm(x, g, b):
    xf = x.astype(jnp.float32)
    mu = jnp.mean(xf, axis=-1, keepdims=True)
    var = jnp.mean(jnp.square(xf - mu), axis=-1, keepdims=True)
    return ((xf - mu) * lax.rsqrt(var + EPS) * g.astype(jnp.float32) + b.astype(jnp.float32)).astype(x.dtype)


def spatial_gating(u, v, w_s, b_s):
    bn, L, _ = v.shape
    n_chunks = -(-L // GMLP_CHUNK)
    pad = n_chunks * GMLP_CHUNK - L
    vp = jnp.pad(v, ((0, 0), (0, pad), (0, 0))).reshape(bn, n_chunks, GMLP_CHUNK, GMLP_HEADS, GMLP_HEAD_DIM)
    mask = jnp.tril(jnp.ones((GMLP_CHUNK, GMLP_CHUNK), dtype=bool))
    w = jnp.where(mask[None], w_s, 0).astype(v.dtype)
    mixed = jnp.einsum('hij,bnjhc->bnihc', w, vp) + b_s.T[:, :, None]
    mixed = mixed.reshape(bn, n_chunks * GMLP_CHUNK, W_A)[:, :L]
    return u * mixed


def multiscale_pool(xb, prev, start_pos):
    P = prev.shape[1]
    T = xb.shape[1]
    full = jnp.concatenate([prev, xb], axis=1).astype(jnp.float32)
    cs = jnp.pad(jnp.cumsum(full, axis=1), ((0, 0), (1, 0), (0, 0)))
    end = P + jnp.arange(T) + 1
    pos = start_pos + jnp.arange(T)
    means = []
    for gi, win in enumerate(POOL_WINDOWS):
        csg = cs[..., gi * POOL_GROUP_DIM:(gi + 1) * POOL_GROUP_DIM]
        cnt = jnp.minimum(pos + 1, win).astype(jnp.float32)
        means.append((csg[:, end] - csg[:, end - win]) / cnt[None, :, None])
    return (jnp.concatenate(means, axis=-1) - full[:, P:]).astype(xb.dtype)


def causal_dwconv(x, prev, w_dw, b_dw):
    full = jnp.concatenate([prev, x], axis=1)
    y = lax.conv_general_dilated(full, w_dw[:, None, :].astype(full.dtype), (1,), 'VALID',
                                 dimension_numbers=('NWC', 'WIO', 'NWC'),
                                 feature_group_count=full.shape[-1])
    return y + b_dw


def moe(h, w_router, b_router, w_gu, b_gu, w_dn, b_dn):
    n, d = h.shape
    logits = (h @ w_router + b_router).astype(jnp.float32)
    top_v, top_i = lax.top_k(logits, TOP_K)
    gates = jax.nn.softmax(top_v, axis=-1)
    a = n * TOP_K
    flat_e = top_i.reshape(a).astype(jnp.int32)
    flat_tok = jnp.arange(a, dtype=jnp.int32) // TOP_K
    flat_g = gates.reshape(a)
    order = jnp.argsort(flat_e)
    e_sorted = flat_e[order]
    counts = jnp.bincount(flat_e, length=N_EXPERTS).astype(jnp.int32)
    starts = jnp.cumsum(counts) - counts
    padded = (counts + MOE_BLOCK - 1) // MOE_BLOCK * MOE_BLOCK
    pad_ends = jnp.cumsum(padded)
    pad_starts = pad_ends - padded
    dest = pad_starts[e_sorted] + jnp.arange(a, dtype=jnp.int32) - starts[e_sorted]
    n_blocks = -(-a // MOE_BLOCK) + N_EXPERTS
    n_slots = n_blocks * MOE_BLOCK
    slot_tok = jnp.full((n_slots,), n, dtype=jnp.int32).at[dest].set(flat_tok[order])
    slot_gate = jnp.zeros((n_slots,), jnp.float32).at[dest].set(flat_g[order])
    block_e = jnp.minimum(jnp.searchsorted(pad_ends, jnp.arange(n_blocks, dtype=jnp.int32) * MOE_BLOCK, side='right'),
                          N_EXPERTS - 1)
    h_pad = jnp.concatenate([h, jnp.zeros((1, d), h.dtype)], axis=0)
    xb = h_pad[slot_tok].reshape(n_blocks, MOE_BLOCK, d)

    def expert_block(args):
        xblk, e = args
        gu = xblk @ w_gu[e] + b_gu[e]
        g = jnp.minimum(gu[:, :D_FF], SWIGLU_LIMIT)
        u = jnp.clip(gu[:, D_FF:], -SWIGLU_LIMIT, SWIGLU_LIMIT)
        act = (u + 1.0) * g * jax.nn.sigmoid(SWIGLU_ALPHA * g)
        return act @ w_dn[e] + b_dn[e]

    yb = lax.map(expert_block, (xb, block_e)).reshape(n_slots, d)
    out = jnp.zeros((n + 1, d), yb.dtype).at[slot_tok].add(yb * slot_gate[:, None].astype(yb.dtype))
    return out[:n].astype(h.dtype)


def mixer(h, prev_pool, prev_conv, start_pos, w_in, ln_v_g, ln_v_b, w_s, b_s, w_a_out,
          w_pool, b_pool, pool_scale, w_dw, b_dw, ln_c_g, ln_c_b, w_c_out, w_out):
    bn, T, _ = h.shape
    z = h @ w_in
    u = jax.nn.gelu(z[..., OFF_U:OFF_V])
    v = layernorm(jax.nn.gelu(z[..., OFF_V:OFF_B]), ln_v_g, ln_v_b)
    y_a = spatial_gating(u, v, w_s, b_s) @ w_a_out
    x_b = z[..., OFF_B:OFF_C]
    pooled = multiscale_pool(x_b, prev_pool, start_pos).reshape(bn, T, POOL_GROUPS, POOL_GROUP_DIM)
    y_b = (jnp.einsum('btgc,gcd->btgd', pooled, w_pool) + b_pool).reshape(bn, T, W_B) * pool_scale
    zc = z[..., OFF_C:OFF_G]
    x_c = zc[..., :W_C] * jax.nn.sigmoid(zc[..., W_C:])
    y_c = jax.nn.silu(layernorm(causal_dwconv(x_c, prev_conv, w_dw, b_dw), ln_c_g, ln_c_b)) @ w_c_out
    gates = jax.nn.sigmoid(z[..., OFF_G:]).reshape(bn, T, N_BRANCH, D_MODEL)
    merged = gates[..., 0, :] * y_a + gates[..., 1, :] * y_b + gates[..., 2, :] * y_c
    new_pool = jnp.concatenate([prev_pool, x_b], axis=1)[:, -POOL_STATE:]
    new_conv = jnp.concatenate([prev_conv, x_c], axis=1)[:, -CONV_STATE:]
    return merged @ w_out, new_pool, new_conv, v


def trunk(x, c, prev_pool, prev_conv, start_pos, lp, final_norm_g):
    (norm_mix_g, norm_ffn_g, w_ada, b_ada, w_in, ln_v_g, ln_v_b, w_spatial, b_spatial, w_a_out,
     w_pool, b_pool, pool_scale, w_dw, b_dw, ln_c_g, ln_c_b, w_c_out, w_out,
     w_router, b_router, w_gate_up, b_gate_up, w_down, b_down) = lp
    bn, T, d = x.shape
    pools, convs, vs = [], [], []
    for l in range(DEPTH):
        mod = jax.nn.silu(c) @ w_ada[l] + b_ada[l]
        sh1, sc1, g1, sh2, sc2, g2 = jnp.split(mod, 6, axis=-1)
        h = rmsnorm(x, norm_mix_g[l]) * (1.0 + sc1[:, None]) + sh1[:, None]
        out, npool, nconv, v = mixer(h, prev_pool[l], prev_conv[l], start_pos, w_in[l], ln_v_g[l], ln_v_b[l],
                                     w_spatial[l], b_spatial[l], w_a_out[l], w_pool[l], b_pool[l],
                                     pool_scale[l], w_dw[l], b_dw[l], ln_c_g[l], ln_c_b[l], w_c_out[l], w_out[l])
        x = x + g1[:, None] * out
        h = rmsnorm(x, norm_ffn_g[l]) * (1.0 + sc2[:, None]) + sh2[:, None]
        f = moe(h.reshape(bn * T, d), w_router[l], b_router[l], w_gate_up[l], b_gate_up[l],
                w_down[l], b_down[l]).reshape(bn, T, d)
        x = x + g2[:, None] * f
        pools.append(npool)
        convs.append(nconv)
        vs.append(v)
    return rmsnorm(x, final_norm_g), jnp.stack(pools), jnp.stack(convs), jnp.stack(vs)


def setup_inputs(seed: int = 0) -> dict:
    key = jax.random.key(seed)
    ks = iter(jax.random.split(key, 40))

    def nrm(shape, scale):
        return jax.random.normal(next(ks), shape, jnp.float32) * scale

    L = DEPTH
    return {
        'x_prompt': nrm((BATCH, SEQ, D_MODEL), 1.0),
        'x_sample': nrm((DEC_BATCH, DEC_SEQ, D_MODEL), 1.0),
        'c_prompt': nrm((BATCH, D_MODEL), 1.0),
        'c_sample': nrm((DEC_BATCH, D_MODEL), 1.0),
        'state_pool': nrm((L, DEC_BATCH, POOL_STATE, W_B), 1.0),
        'state_conv': nrm((L, DEC_BATCH, CONV_STATE, W_C), 0.5),
        'norm_mix_g': 1.0 + nrm((L, D_MODEL), 0.02),
        'norm_ffn_g': 1.0 + nrm((L, D_MODEL), 0.02),
        'w_ada': nrm((L, D_MODEL, 6 * D_MODEL), 0.5 * D_MODEL ** -0.5),
        'b_ada': nrm((L, 6 * D_MODEL), 0.02),
        'w_in': nrm((L, D_MODEL, IN_COLS), D_MODEL ** -0.5),
        'ln_v_g': 1.0 + nrm((L, W_A), 0.02),
        'ln_v_b': nrm((L, W_A), 0.02),
        'w_spatial': nrm((L, GMLP_HEADS, GMLP_CHUNK, GMLP_CHUNK), 0.5 * GMLP_CHUNK ** -0.5),
        'b_spatial': 1.0 + nrm((L, GMLP_HEADS, GMLP_CHUNK), 0.02),
        'w_a_out': nrm((L, W_A, D_MODEL), W_A ** -0.5),
        'w_pool': nrm((L, POOL_GROUPS, POOL_GROUP_DIM, POOL_GROUP_DIM), POOL_GROUP_DIM ** -0.5),
        'b_pool': nrm((L, POOL_GROUPS, POOL_GROUP_DIM), 0.02),
        'pool_scale': 1.0 + nrm((L, W_B), 0.1),
        'w_dw': nrm((L, CONV_K, W_C), CONV_K ** -0.5),
        'b_dw': nrm((L, W_C), 0.02),
        'ln_c_g': 1.0 + nrm((L, W_C), 0.02),
        'ln_c_b': nrm((L, W_C), 0.02),
        'w_c_out': nrm((L, W_C, D_MODEL), W_C ** -0.5),
        'w_out': nrm((L, D_MODEL, D_MODEL), D_MODEL ** -0.5),
        'w_router': nrm((L, D_MODEL, N_EXPERTS), D_MODEL ** -0.5),
        'b_router': nrm((L, N_EXPERTS), 0.01),
        'w_gate_up': nrm((L, N_EXPERTS, D_MODEL, 2 * D_FF), D_MODEL ** -0.5),
        'b_gate_up': nrm((L, N_EXPERTS, 2 * D_FF), 0.01),
        'w_down': nrm((L, N_EXPERTS, D_FF, D_MODEL), D_FF ** -0.5),
        'b_down': nrm((L, N_EXPERTS, D_MODEL), 0.01),
        'final_norm_g': 1.0 + nrm((D_MODEL,), 0.02),
    }


def reference(x_prompt, x_sample, c_prompt, c_sample, state_pool, state_conv,
              norm_mix_g, norm_ffn_g, w_ada, b_ada, w_in, ln_v_g, ln_v_b, w_spatial, b_spatial, w_a_out,
              w_pool, b_pool, pool_scale, w_dw, b_dw, ln_c_g, ln_c_b, w_c_out, w_out,
              w_router, b_router, w_gate_up, b_gate_up, w_down, b_down, final_norm_g):
    lp = (norm_mix_g, norm_ffn_g, w_ada, b_ada, w_in, ln_v_g, ln_v_b, w_spatial, b_spatial, w_a_out,
          w_pool, b_pool, pool_scale, w_dw, b_dw, ln_c_g, ln_c_b, w_c_out, w_out,
          w_router, b_router, w_gate_up, b_gate_up, w_down, b_down)
    bp = x_prompt.shape[0]
    zero_pool = jnp.zeros((DEPTH, bp, POOL_STATE, W_B), x_prompt.dtype)
    zero_conv = jnp.zeros((DEPTH, bp, CONV_STATE, W_C), x_prompt.dtype)
    y_prompt, new_pool_prompt, new_conv_prompt, _ = trunk(x_prompt, c_prompt, zero_pool, zero_conv, 0,
                                                          lp, final_norm_g)
    y_sample, new_pool_sample, new_conv_sample, new_v_sample = trunk(x_sample, c_sample, state_pool, state_conv,
                                                                     PAST_LEN, lp, final_norm_g)
    return (y_prompt, y_sample, new_pool_prompt, new_conv_prompt, new_pool_sample, new_conv_sample, new_v_sample)
```

```python
import functools
from typing import NamedTuple

import jax
import jax.numpy as jnp
from jax import lax
from jax.experimental import pallas as pl
from jax.experimental.pallas import tpu as pltpu

GMLP_CHUNK = 128
GMLP_HEADS = 4
POOL_WINDOWS = (2, 4, 8, 16)
POOL_STATE = max(POOL_WINDOWS) - 1
CONV_K = 31
CONV_STATE = CONV_K - 1
N_BRANCH = 3
N_EXPERTS = 32
TOP_K = 4
SWIGLU_LIMIT = 7.0
SWIGLU_ALPHA = 1.702
EPS = 1e-6
PAST_LEN = 2048

LANES = 128
SUBLANES = 8
V7X_VMEM_LIMIT_BYTES = 56 * 2**20

POOL_HIST = 16
CONV_HIST = 32
ROUTER_LANES = LANES
TOPK_LANES = 8
MOE_TM = 256
PROMPT_TILE = 256
SAMPLE_BT = 8

F32 = jnp.float32
BF16 = jnp.bfloat16


def _round_up(a, m):
    return (a + m - 1) // m * m


class MixerCfg(NamedTuple):
    bt: int
    t: int
    ch: int
    kg: int
    kp: int
    d: int
    start_pos: int
    emit_v: bool


def _rms(x, g):
    return x * lax.rsqrt(jnp.mean(x * x, axis=-1, keepdims=True) + EPS) * g


def _ln(x, g, b):
    mu = jnp.mean(x, axis=-1, keepdims=True)
    xc = x - mu
    var = jnp.mean(xc * xc, axis=-1, keepdims=True)
    return xc * lax.rsqrt(var + EPS) * g + b


def _dot(a, b):
    return jnp.dot(a, b, preferred_element_type=F32)


def _mixer_kernel(cfg, x_ref, mod_ref, pp_ref, pc_ref, nmg_ref, nfg_ref, win_ref, lnvg_ref, lnvb_ref,
                  wsp_ref, bsf_ref, wao_ref, wpool_ref, bpool_ref, pscale_ref, wdw_ref, bdw_ref,
                  lncg_ref, lncb_ref, wco_ref, wout_ref, wrh_ref, wrl_ref, br_ref, pmat_ref,
                  xo_ref, h2_ref, ti_ref, tg_ref, npool_ref, nconv_ref, *rest):
    if cfg.emit_v:
        v_ref, hp_scr, xc_scr = rest
    else:
        hp_scr, xc_scr = rest
    bt, t, d = cfg.bt, cfg.t, cfg.d
    m = bt * t
    n_slab = d // LANES
    gd = d // len(POOL_WINDOWS)
    hd = d // GMLP_HEADS
    j = pl.program_id(1)

    def mod_rows(k):
        r = mod_ref[:, :, k * d:(k + 1) * d]
        if bt == 1:
            return r.reshape(1, d)
        return jnp.broadcast_to(r, (bt, t, d)).reshape(m, d)

    @pl.when(j == 0)
    def _load_state():
        hp_scr[...] = pp_ref[...]
        for bi in range(bt):
            for c in range(n_slab):
                xc_scr[bi, c, 0:CONV_HIST, :] = pc_ref[bi, :, c * LANES:(c + 1) * LANES]

    x = x_ref[...].reshape(m, d)
    sh1, sc1, g1 = mod_rows(0), mod_rows(1), mod_rows(2)
    h = (_rms(x, nmg_ref[...]) * (1.0 + sc1) + sh1).astype(BF16)

    def zcols(lo, hi):
        return _dot(h, win_ref[:, lo:hi])

    off_u, off_v, off_b, off_c, off_g = 0, d, 2 * d, 3 * d, 5 * d

    zb = zcols(off_b, off_b + d)
    lane = lax.broadcasted_iota(jnp.int32, (1, d), 1)
    win_lane = jnp.left_shift(2, lane // gd).astype(F32)
    pos1 = (cfg.start_pos + 1 + j * t + lax.broadcasted_iota(jnp.int32, (t, 1), 0)).astype(F32)
    cnt = jnp.minimum(pos1, win_lane)
    pooled_parts = []
    for bi in range(bt):
        zb_b = zb[bi * t:(bi + 1) * t]
        pieces = [hp_scr[bi], zb_b]
        if cfg.kp > POOL_HIST + t:
            pieces = [jnp.zeros((cfg.kp - POOL_HIST - t, d), F32)] + pieces
        full = jnp.concatenate(pieces, axis=0)
        fullb = full.astype(BF16)
        sums = jnp.concatenate(
            [_dot(pmat_ref[g], fullb[:, g * gd:(g + 1) * gd]) for g in range(len(POOL_WINDOWS))], axis=1)
        pooled_parts.append(sums / cnt - zb_b)
        hp_scr[bi] = full[cfg.kp - POOL_HIST:]
        npool_ref[bi] = hp_scr[bi, pl.ds(POOL_HIST - POOL_STATE, POOL_STATE), :]
    pooled = (pooled_parts[0] if bt == 1 else jnp.concatenate(pooled_parts, axis=0)).astype(BF16)
    y_b = jnp.concatenate(
        [_dot(pooled[:, g * gd:(g + 1) * gd], wpool_ref[g]) for g in range(len(POOL_WINDOWS))], axis=1)
    y_b = (y_b + bpool_ref[...]) * pscale_ref[...]
    merged = jax.nn.sigmoid(zcols(off_g + d, off_g + 2 * d)) * y_b

    u = jax.nn.gelu(zcols(off_u, off_u + d))
    v = _ln(jax.nn.gelu(zcols(off_v, off_v + d)), lnvg_ref[...], lnvb_ref[...])
    if cfg.emit_v:
        v_ref[...] = v.reshape(bt, t, d)
    vb = v.astype(BF16)
    mixed_rows = []
    for r0 in range(0, m, cfg.ch):
        vc = vb[r0:r0 + cfg.ch]
        if cfg.kg > cfg.ch:
            vc = jnp.concatenate([vc, jnp.zeros((cfg.kg - cfg.ch, d), BF16)], axis=0)
        mixed_rows.append(jnp.concatenate(
            [_dot(wsp_ref[hh], vc[:, hh * hd:(hh + 1) * hd]) for hh in range(GMLP_HEADS)], axis=1) + bsf_ref[...])
    mixed = mixed_rows[0] if len(mixed_rows) == 1 else jnp.concatenate(mixed_rows, axis=0)
    y_a = _dot((u * mixed).astype(BF16), wao_ref[...])
    merged = merged + jax.nn.sigmoid(zcols(off_g, off_g + d)) * y_a

    x_c = zcols(off_c, off_c + d) * jax.nn.sigmoid(zcols(off_c + d, off_c + 2 * d))
    conv_parts = []
    for bi in range(bt):
        accs = []
        for c in range(n_slab):
            cs = slice(c * LANES, (c + 1) * LANES)
            xc_scr[bi, c, CONV_HIST:CONV_HIST + t, :] = x_c[bi * t:(bi + 1) * t, cs]
            acc = jnp.broadcast_to(bdw_ref[:, cs], (t, LANES))
            for k in range(CONV_K):
                acc = acc + wdw_ref[k:k + 1, cs] * xc_scr[bi, c, pl.ds(CONV_HIST - CONV_STATE + k, t), :]
            accs.append(acc)
            tail = xc_scr[bi, c, t:t + CONV_HIST, :]
            nconv_ref[bi, :, cs] = xc_scr[bi, c, pl.ds(t + CONV_HIST - CONV_STATE, CONV_STATE), :]
            xc_scr[bi, c, 0:CONV_HIST, :] = tail
        conv_parts.append(jnp.concatenate(accs, axis=1))
    conv = conv_parts[0] if bt == 1 else jnp.concatenate(conv_parts, axis=0)
    y_c = _dot(jax.nn.silu(_ln(conv, lncg_ref[...], lncb_ref[...])).astype(BF16), wco_ref[...])
    merged = merged + jax.nn.sigmoid(zcols(off_g + 2 * d, off_g + 3 * d)) * y_c

    x_new = x + g1 * _dot(merged.astype(BF16), wout_ref[...])
    xo_ref[...] = x_new.reshape(bt, t, d)
    sh2, sc2 = mod_rows(3), mod_rows(4)
    h2 = _rms(x_new, nfg_ref[...]) * (1.0 + sc2) + sh2
    h2_ref[...] = h2.reshape(bt, t, d)
    h2_hi = h2.astype(BF16)
    h2_lo = (h2 - h2_hi.astype(F32)).astype(BF16)
    logits = _dot(h2_hi, wrh_ref[...]) + _dot(h2_lo, wrh_ref[...]) + _dot(h2_hi, wrl_ref[...]) + br_ref[...]
    lane_r = lax.broadcasted_iota(jnp.int32, (m, ROUTER_LANES), 1).astype(F32)
    lane_k = lax.broadcasted_iota(jnp.int32, (m, TOPK_LANES), 1)
    work = logits
    top_vals = []
    ti = jnp.zeros((m, TOPK_LANES), jnp.int32)
    for r in range(TOP_K):
        mx = jnp.max(work, axis=-1, keepdims=True)
        idx = jnp.min(jnp.where(work == mx, lane_r, float(ROUTER_LANES)), axis=-1, keepdims=True)
        ti = jnp.where(lane_k == r, idx.astype(jnp.int32), ti)
        top_vals.append(mx)
        work = jnp.where(lane_r == idx, -jnp.inf, work)
    exps = [jnp.exp(tv - top_vals[0]) for tv in top_vals]
    denom = exps[0] + exps[1] + exps[2] + exps[3]
    tg = jnp.zeros((m, TOPK_LANES), F32)
    for r in range(TOP_K):
        tg = jnp.where(lane_k == r, exps[r] / denom, tg)
    ti_ref[...] = ti
    tg_ref[...] = tg


def _const_spec(shape):
    nd = len(shape)
    return pl.BlockSpec(shape, lambda b, j, _nd=nd: (0,) * _nd, pipeline_mode=pl.Buffered(1))


def _pool_band(t, kp):
    col = jnp.arange(kp)[None, :]
    end = (kp - t) + jnp.arange(t)[:, None]
    return jnp.stack([((col <= end) & (col > end - w)) for w in POOL_WINDOWS]).astype(BF16)


def _mixer_call(cfg, x, mod, prev_pool, prev_conv, lw, row_block0, n_rows_total):
    b, s, d = x.shape
    bt, t = cfg.bt, cfg.t
    m = bt * t
    grid = (b // bt, s // t)
    n_slab = d // LANES

    def tile_spec():
        return pl.BlockSpec((bt, t, d), lambda bi, j: (bi, j, 0))

    def batch_spec(rows, width):
        return pl.BlockSpec((bt, rows, width), lambda bi, j: (bi, 0, 0))

    def rows_spec(width):
        return pl.BlockSpec((m, width), lambda bi, j: (bi * (s // t) + j, 0))

    consts = [lw["norm_mix_g"], lw["norm_ffn_g"], lw["w_in"], lw["ln_v_g"], lw["ln_v_b"], lw["w_sp"], lw["bs_full"],
              lw["w_a_out"], lw["w_pool"], lw["b_pool"], lw["pool_scale"], lw["w_dw"], lw["b_dw"], lw["ln_c_g"],
              lw["ln_c_b"], lw["w_c_out"], lw["w_out"], lw["wr_hi"], lw["wr_lo"], lw["b_router"], lw["pmat"]]
    in_specs = [tile_spec(), batch_spec(1, 6 * d), batch_spec(POOL_HIST, d), batch_spec(CONV_HIST, d)]
    in_specs += [_const_spec(c.shape) for c in consts]
    out_shape = [jax.ShapeDtypeStruct((b, s, d), F32), jax.ShapeDtypeStruct((b, s, d), F32),
                 jax.ShapeDtypeStruct((b * s, TOPK_LANES), jnp.int32), jax.ShapeDtypeStruct((b * s, TOPK_LANES), F32),
                 jax.ShapeDtypeStruct((b, POOL_STATE, d), F32), jax.ShapeDtypeStruct((b, CONV_STATE, d), F32)]
    out_specs = [tile_spec(), tile_spec(), rows_spec(TOPK_LANES), rows_spec(TOPK_LANES),
                 batch_spec(POOL_STATE, d), batch_spec(CONV_STATE, d)]
    if cfg.emit_v:
        out_shape.append(jax.ShapeDtypeStruct((b, s, d), F32))
        out_specs.append(tile_spec())
    return pl.pallas_call(
        functools.partial(_mixer_kernel, cfg),
        grid=grid,
        in_specs=in_specs,
        out_specs=out_specs,
        out_shape=out_shape,
        scratch_shapes=[pltpu.VMEM((bt, POOL_HIST, d), F32),
                        pltpu.VMEM((bt, n_slab, CONV_HIST + t, LANES), F32)],
        compiler_params=pltpu.CompilerParams(dimension_semantics=("arbitrary", "arbitrary"),
                                             vmem_limit_bytes=V7X_VMEM_LIMIT_BYTES),
        name=f"mixer_t{t}",
    )(x, mod, prev_pool, prev_conv, *consts)


def _ada_kernel(c_ref, w_ref, b_ref, o_ref):
    o_ref[0] = _dot(jax.nn.silu(c_ref[...]).astype(BF16), w_ref[0].astype(BF16)) + b_ref[0]


def _ada_call(c_all, w_ada, b_ada):
    n_layers, d, six_d = w_ada.shape
    rows = c_all.shape[0]
    bn = six_d // 6
    return pl.pallas_call(
        _ada_kernel,
        grid=(n_layers, six_d // bn),
        in_specs=[pl.BlockSpec((rows, d), lambda l, n: (0, 0)),
                  pl.BlockSpec((1, d, bn), lambda l, n: (l, 0, n)),
                  pl.BlockSpec((1, 1, bn), lambda l, n: (l, 0, n))],
        out_specs=pl.BlockSpec((1, rows, bn), lambda l, n: (l, 0, n)),
        out_shape=jax.ShapeDtypeStruct((n_layers, rows, six_d), F32),
        compiler_params=pltpu.CompilerParams(dimension_semantics=("arbitrary", "arbitrary")),
        name="adaln",
    )(c_all, w_ada, b_ada.reshape(n_layers, 1, six_d))


def _moe_kernel(be_ref, nused_ref, tok_ref, tokn_ref, dst_ref, h_hbm, wgu_ref, bgu_ref, wdn_ref, bdn_ref,
                y_hbm, xbuf, ybuf, wgu_bf, wdn_bf, gsem, ssem):
    i = pl.program_id(0)
    nused = nused_ref[0]
    tm, d = ybuf.shape
    dff = wdn_bf.shape[0]
    slot = i % 2

    def gather_copy(idx_ref, r, s):
        return pltpu.make_async_copy(h_hbm.at[pl.ds(idx_ref[0, 0, r], 1)], xbuf.at[s, pl.ds(r, 1)], gsem.at[s])

    def scatter_copy(r):
        return pltpu.make_async_copy(ybuf.at[pl.ds(r, 1)], y_hbm.at[pl.ds(dst_ref[0, 0, r], 1)], ssem.at[0])

    def start_gather(idx_ref, s):
        def body(r, carry):
            gather_copy(idx_ref, r, s).start()
            return carry
        lax.fori_loop(0, tm, body, 0)

    def wait_rows(make_copy):
        def body(r, carry):
            make_copy(r).wait()
            return carry
        lax.fori_loop(0, tm, body, 0)

    @pl.when(i == 0)
    def _prologue():
        start_gather(tok_ref, 0)

    @pl.when(i + 1 < nused)
    def _prefetch():
        start_gather(tokn_ref, 1 - slot)

    @pl.when(i < nused)
    def _body():
        changed = jnp.logical_or(i == 0, be_ref[i] != be_ref[jnp.maximum(i - 1, 0)])

        @pl.when(changed)
        def _cast_weights():
            wgu_bf[...] = wgu_ref[0].astype(BF16)
            wdn_bf[...] = wdn_ref[0].astype(BF16)

        wait_rows(lambda r: gather_copy(tok_ref, r, slot))
        xb = xbuf[slot].astype(BF16)
        gu = _dot(xb, wgu_bf[...]) + bgu_ref[0]
        g = jnp.minimum(gu[:, :dff], SWIGLU_LIMIT)
        u = jnp.clip(gu[:, dff:], -SWIGLU_LIMIT, SWIGLU_LIMIT)
        act = (u + 1.0) * g * jax.nn.sigmoid(SWIGLU_ALPHA * g)
        y = _dot(act.astype(BF16), wdn_bf[...]) + bdn_ref[0]

        @pl.when(i > 0)
        def _drain_prev():
            wait_rows(scatter_copy)

        ybuf[...] = y

        def sbody(r, carry):
            scatter_copy(r).start()
            return carry
        lax.fori_loop(0, tm, sbody, 0)

        @pl.when(i == nused - 1)
        def _drain_last():
            wait_rows(scatter_copy)


def _moe_call(h_all, slot_tok, slot_dst, block_e, nused, w_gu, b_gu, w_dn, b_dn, n_pad):
    n_blocks = block_e.shape[0]
    n_exp, d, two_f = w_gu.shape
    dff = two_f // 2
    tm = MOE_TM
    last = n_blocks - 1
    grid_spec = pltpu.PrefetchScalarGridSpec(
        num_scalar_prefetch=2,
        grid=(n_blocks,),
        in_specs=[
            pl.BlockSpec((1, 1, tm), lambda i, be, nu: (i, 0, 0), memory_space=pltpu.SMEM),
            pl.BlockSpec((1, 1, tm), lambda i, be, nu: (jnp.minimum(i + 1, last), 0, 0), memory_space=pltpu.SMEM),
            pl.BlockSpec((1, 1, tm), lambda i, be, nu: (i, 0, 0), memory_space=pltpu.SMEM),
            pl.BlockSpec(memory_space=pl.ANY),
            pl.BlockSpec((1, d, two_f), lambda i, be, nu: (be[i], 0, 0)),
            pl.BlockSpec((1, 1, two_f), lambda i, be, nu: (be[i], 0, 0)),
            pl.BlockSpec((1, dff, d), lambda i, be, nu: (be[i], 0, 0)),
            pl.BlockSpec((1, 1, d), lambda i, be, nu: (be[i], 0, 0)),
        ],
        out_specs=pl.BlockSpec(memory_space=pl.ANY),
        scratch_shapes=[pltpu.VMEM((2, tm, d), F32), pltpu.VMEM((tm, d), F32),
                        pltpu.VMEM((d, two_f), BF16), pltpu.VMEM((dff, d), BF16),
                        pltpu.SemaphoreType.DMA((2,)), pltpu.SemaphoreType.DMA((1,))],
    )
    slot_tok3 = slot_tok.reshape(n_blocks, 1, tm)
    return pl.pallas_call(
        _moe_kernel,
        grid_spec=grid_spec,
        out_shape=jax.ShapeDtypeStruct((TOP_K * n_pad, d), F32),
        compiler_params=pltpu.CompilerParams(dimension_semantics=("arbitrary",),
                                             vmem_limit_bytes=V7X_VMEM_LIMIT_BYTES),
        name="moe_experts",
    )(block_e, nused, slot_tok3, slot_tok3, slot_dst.reshape(n_blocks, 1, tm), h_all,
      w_gu, b_gu.reshape(n_exp, 1, two_f), w_dn, b_dn.reshape(n_exp, 1, d))


def _route(top_i, n_all, n_pad):
    tm = MOE_TM
    a = n_all * TOP_K
    n_blocks = a // tm + N_EXPERTS
    flat_e = top_i.reshape(a)
    order = jnp.argsort(flat_e, stable=True).astype(jnp.int32)
    counts = jnp.sum((flat_e[:, None] == jnp.arange(N_EXPERTS, dtype=jnp.int32)[None, :]).astype(jnp.int32), axis=0)
    starts = jnp.cumsum(counts) - counts
    padded = (counts + tm - 1) // tm * tm
    pad_ends = jnp.cumsum(padded)
    pad_starts = pad_ends - padded
    block_e = jnp.minimum(jnp.searchsorted(pad_ends, jnp.arange(n_blocks, dtype=jnp.int32) * tm, side="right"),
                          N_EXPERTS - 1).astype(jnp.int32)
    nused = (pad_ends[-1] // tm).astype(jnp.int32).reshape(1)
    s = jnp.arange(n_blocks * tm, dtype=jnp.int32)
    e_s = block_e[s // tm]
    jj = s - pad_starts[e_s]
    valid = jj < counts[e_s]
    a_s = order[jnp.clip(starts[e_s] + jj, 0, a - 1)]
    tok = a_s // TOP_K
    slot_tok = jnp.where(valid, tok, 0).astype(jnp.int32)
    slot_dst = jnp.where(valid, (a_s % TOP_K) * n_pad + tok, n_all + s % tm).astype(jnp.int32)
    return slot_tok, slot_dst, block_e, nused


def _combine_kernel(final, x_ref, mod_ref, y_ref, tg_ref, fg_ref, o_ref):
    bt, t, d = x_ref.shape
    m = bt * t
    g2 = mod_ref[:, :, 5 * d:6 * d]
    g2 = g2.reshape(1, d) if bt == 1 else jnp.broadcast_to(g2, (bt, t, d)).reshape(m, d)
    tg = tg_ref[...]
    f = tg[:, 0:1] * y_ref[0]
    for k in range(1, TOP_K):
        f = f + tg[:, k:k + 1] * y_ref[k]
    out = x_ref[...].reshape(m, d) + g2 * f
    if final:
        out = _rms(out, fg_ref[...])
    o_ref[...] = out.reshape(bt, t, d)


def _combine_call(x_new, mod, y4, tg, final_g, bt, t, row_block0, final):
    b, s, d = x_new.shape
    m = bt * t
    steps = s // t
    return pl.pallas_call(
        functools.partial(_combine_kernel, final),
        grid=(b // bt, steps),
        in_specs=[pl.BlockSpec((bt, t, d), lambda bi, j: (bi, j, 0)),
                  pl.BlockSpec((bt, 1, 6 * d), lambda bi, j: (bi, 0, 0)),
                  pl.BlockSpec((TOP_K, m, d), lambda bi, j: (0, row_block0 + bi * steps + j, 0)),
                  pl.BlockSpec((m, TOPK_LANES), lambda bi, j: (row_block0 + bi * steps + j, 0)),
                  pl.BlockSpec((1, d), lambda bi, j: (0, 0))],
        out_specs=pl.BlockSpec((bt, t, d), lambda bi, j: (bi, j, 0)),
        out_shape=jax.ShapeDtypeStruct((b, s, d), F32),
        compiler_params=pltpu.CompilerParams(dimension_semantics=("arbitrary", "arbitrary")),
        name=f"combine_t{t}",
    )(x_new, mod, y4, tg, final_g)


def _layer_weights(l, p, cfgs):
    d = p["w_out"].shape[-1]
    row = lambda v: v[l].reshape(1, d)
    wr = p["w_router"][l]
    wr_pad = jnp.zeros((d, ROUTER_LANES), F32).at[:, :N_EXPERTS].set(wr)
    wr_hi = wr_pad.astype(BF16)
    wr_lo = (wr_pad - wr_hi.astype(F32)).astype(BF16)
    b_router = jnp.full((1, ROUTER_LANES), -1e30, F32).at[0, :N_EXPERTS].set(p["b_router"][l])
    mask = jnp.tril(jnp.ones((GMLP_CHUNK, GMLP_CHUNK), dtype=bool))
    w_sp_full = jnp.where(mask[None], p["w_spatial"][l], 0)
    hd = d // GMLP_HEADS
    base = dict(norm_mix_g=row(p["norm_mix_g"]), norm_ffn_g=row(p["norm_ffn_g"]), w_in=p["w_in"][l].astype(BF16),
                ln_v_g=row(p["ln_v_g"]), ln_v_b=row(p["ln_v_b"]), w_a_out=p["w_a_out"][l].astype(BF16),
                w_pool=p["w_pool"][l].astype(BF16), b_pool=row(p["b_pool"]), pool_scale=row(p["pool_scale"]),
                w_dw=jnp.zeros((_round_up(CONV_K, SUBLANES), d), F32).at[:CONV_K].set(p["w_dw"][l]),
                b_dw=row(p["b_dw"]), ln_c_g=row(p["ln_c_g"]), ln_c_b=row(p["ln_c_b"]),
                w_c_out=p["w_c_out"][l].astype(BF16), w_out=p["w_out"][l].astype(BF16),
                wr_hi=wr_hi, wr_lo=wr_lo, b_router=b_router)
    out = []
    for cfg in cfgs:
        w_sp = jnp.zeros((GMLP_HEADS, cfg.ch, cfg.kg), F32).at[:, :, :cfg.ch].set(w_sp_full[:, :cfg.ch, :cfg.ch])
        bs_full = jnp.repeat(p["b_spatial"][l][:, :cfg.ch].T, hd, axis=1)
        out.append(dict(base, w_sp=w_sp.astype(BF16), bs_full=bs_full, pmat=_pool_band(cfg.t, cfg.kp)))
    return out


def _pad_front(state, rows):
    pad = rows - state.shape[-2]
    return jnp.pad(state, ((0, 0),) * (state.ndim - 2) + ((pad, 0), (0, 0)))


def kernel(x_prompt, x_sample, c_prompt, c_sample, state_pool, state_conv, norm_mix_g, norm_ffn_g, w_ada, b_ada, w_in, ln_v_g, ln_v_b, w_spatial, b_spatial, w_a_out, w_pool, b_pool, pool_scale, w_dw, b_dw, ln_c_g, ln_c_b, w_c_out, w_out, w_router, b_router, w_gate_up, b_gate_up, w_down, b_down, final_norm_g):
    p = dict(norm_mix_g=norm_mix_g, norm_ffn_g=norm_ffn_g, w_in=w_in, ln_v_g=ln_v_g, ln_v_b=ln_v_b,
             w_spatial=w_spatial, b_spatial=b_spatial, w_a_out=w_a_out, w_pool=w_pool, b_pool=b_pool,
             pool_scale=pool_scale, w_dw=w_dw, b_dw=b_dw, ln_c_g=ln_c_g, ln_c_b=ln_c_b, w_c_out=w_c_out,
             w_out=w_out, w_router=w_router, b_router=b_router)
    n_layers = w_in.shape[0]
    bp, sp, d = x_prompt.shape
    bs, ss, _ = x_sample.shape
    tp = min(PROMPT_TILE, sp)
    bts = min(SAMPLE_BT, bs)
    cfg_p = MixerCfg(bt=1, t=tp, ch=GMLP_CHUNK, kg=GMLP_CHUNK, kp=_round_up(POOL_HIST + tp, LANES), d=d,
                     start_pos=0, emit_v=False)
    cfg_s = MixerCfg(bt=bts, t=ss, ch=ss, kg=_round_up(ss, LANES), kp=_round_up(POOL_HIST + ss, LANES), d=d,
                     start_pos=PAST_LEN, emit_v=True)
    n_p, n_s = bp * sp, bs * ss
    n_all = n_p + n_s
    n_pad = n_all + MOE_TM
    blk0_s = n_p // (bts * ss)

    mod_all = _ada_call(jnp.concatenate([c_prompt, c_sample], axis=0), w_ada, b_ada)
    mod_p = mod_all[:, :bp].reshape(n_layers, bp, 1, 6 * d)
    mod_s = mod_all[:, bp:].reshape(n_layers, bs, 1, 6 * d)
    final_g = final_norm_g.reshape(1, d)

    xp, xs = x_prompt, x_sample
    zero_pool = jnp.zeros((bp, POOL_HIST, d), F32)
    zero_conv = jnp.zeros((bp, CONV_HIST, d), F32)
    pools_p, convs_p, pools_s, convs_s, vs = [], [], [], [], []
    for l in range(n_layers):
        lw_p, lw_s = _layer_weights(l, p, (cfg_p, cfg_s))
        xp_new, h2p, tip, tgp, npool_p, nconv_p = _mixer_call(cfg_p, xp, mod_p[l], zero_pool, zero_conv, lw_p, 0, n_all)
        xs_new, h2s, tis, tgs, npool_s, nconv_s, v_s = _mixer_call(
            cfg_s, xs, mod_s[l], _pad_front(state_pool[l], POOL_HIST), _pad_front(state_conv[l], CONV_HIST),
            lw_s, blk0_s, n_all)
        h_all = jnp.concatenate([h2p.reshape(n_p, d), h2s.reshape(n_s, d)], axis=0)
        top_i = jnp.concatenate([tip, tis], axis=0)[:, :TOP_K]
        tg_all = jnp.concatenate([tgp, tgs], axis=0)
        slot_tok, slot_dst, block_e, nused = _route(top_i, n_all, n_pad)
        y4 = _moe_call(h_all, slot_tok, slot_dst, block_e, nused, w_gate_up[l], b_gate_up[l], w_down[l], b_down[l],
                       n_pad).reshape(TOP_K, n_pad, d)
        final = l == n_layers - 1
        xp = _combine_call(xp_new, mod_p[l], y4, tg_all, final_g, 1, tp, 0, final)
        xs = _combine_call(xs_new, mod_s[l], y4, tg_all, final_g, bts, ss, blk0_s, final)
        pools_p.append(npool_p)
        convs_p.append(nconv_p)
        pools_s.append(npool_s)
        convs_s.append(nconv_s)
        vs.append(v_s)
    return (xp, xs, jnp.stack(pools_p), jnp.stack(convs_p), jnp.stack(pools_s), jnp.stack(convs_s), jnp.stack(vs))
```

```python
import functools
from typing import NamedTuple

import jax
import jax.numpy as jnp
from jax import lax
from jax.experimental import pallas as pl
from jax.experimental.pallas import tpu as pltpu
from jax.experimental.pallas import tpu_sc as plsc

GMLP_CHUNK = 128
GMLP_HEADS = 4
POOL_WINDOWS = (2, 4, 8, 16)
POOL_STATE = max(POOL_WINDOWS) - 1
CONV_K = 31
CONV_STATE = CONV_K - 1
N_BRANCH = 3
N_EXPERTS = 32
TOP_K = 4
SWIGLU_LIMIT = 7.0
SWIGLU_ALPHA = 1.702
EPS = 1e-6
PAST_LEN = 2048

LANES = 128
SUBLANES = 8
V7X_VMEM_LIMIT_BYTES = 56 * 2**20

POOL_HIST = 16
CONV_HIST = 32
ROUTER_LANES = LANES
TOPK_LANES = 8
MOE_TM = 256
PROMPT_TILE = 256
SAMPLE_BT = 8

F32 = jnp.float32
BF16 = jnp.bfloat16


def _round_up(a, m):
    return (a + m - 1) // m * m


class MixerCfg(NamedTuple):
    bt: int
    t: int
    ch: int
    kg: int
    kp: int
    d: int
    start_pos: int
    emit_v: bool


def _rms(x, g):
    return x * lax.rsqrt(jnp.mean(x * x, axis=-1, keepdims=True) + EPS) * g


def _ln(x, g, b):
    mu = jnp.mean(x, axis=-1, keepdims=True)
    xc = x - mu
    var = jnp.mean(xc * xc, axis=-1, keepdims=True)
    return xc * lax.rsqrt(var + EPS) * g + b


def _dot(a, b):
    return jnp.dot(a, b, preferred_element_type=F32)


def _mixer_kernel(cfg, x_ref, mod_ref, pp_ref, pc_ref, nmg_ref, nfg_ref, win_ref, lnvg_ref, lnvb_ref,
                  wsp_ref, bsf_ref, wao_ref, wpool_ref, bpool_ref, pscale_ref, wdw_ref, bdw_ref,
                  lncg_ref, lncb_ref, wco_ref, wout_ref, wrh_ref, wrl_ref, br_ref, pmat_ref, ltri_ref, cnt0_ref,
                  xo_ref, h2_ref, ti_ref, tg_ref, rk_ref, cnt_ref, npool_ref, nconv_ref, *rest):
    if cfg.emit_v:
        v_ref, hp_scr, xc_scr, run_scr = rest
    else:
        hp_scr, xc_scr, run_scr = rest
    bt, t, d = cfg.bt, cfg.t, cfg.d
    m = bt * t
    n_slab = d // LANES
    gd = d // len(POOL_WINDOWS)
    hd = d // GMLP_HEADS
    j = pl.program_id(1)

    def mod_rows(k):
        r = mod_ref[:, :, k * d:(k + 1) * d]
        if bt == 1:
            return r.reshape(1, d)
        return jnp.broadcast_to(r, (bt, t, d)).reshape(m, d)

    @pl.when(j == 0)
    def _load_state():
        hp_scr[...] = pp_ref[...]
        for bi in range(bt):
            for c in range(n_slab):
                xc_scr[bi, c, 0:CONV_HIST, :] = pc_ref[bi, :, c * LANES:(c + 1) * LANES]

    x = x_ref[...].reshape(m, d)
    sh1, sc1, g1 = mod_rows(0), mod_rows(1), mod_rows(2)
    h = (_rms(x, nmg_ref[...]) * (1.0 + sc1) + sh1).astype(BF16)

    def zcols(lo, hi):
        return _dot(h, win_ref[:, lo:hi])

    off_u, off_v, off_b, off_c, off_g = 0, d, 2 * d, 3 * d, 5 * d

    zb = zcols(off_b, off_b + d)
    lane = lax.broadcasted_iota(jnp.int32, (1, d), 1)
    win_lane = jnp.left_shift(2, lane // gd).astype(F32)
    pos1 = (cfg.start_pos + 1 + j * t + lax.broadcasted_iota(jnp.int32, (t, 1), 0)).astype(F32)
    cnt = jnp.minimum(pos1, win_lane)
    pooled_parts = []
    for bi in range(bt):
        zb_b = zb[bi * t:(bi + 1) * t]
        pieces = [hp_scr[bi], zb_b]
        if cfg.kp > POOL_HIST + t:
            pieces = [jnp.zeros((cfg.kp - POOL_HIST - t, d), F32)] + pieces
        full = jnp.concatenate(pieces, axis=0)
        fullb = full.astype(BF16)
        sums = jnp.concatenate(
            [_dot(pmat_ref[g], fullb[:, g * gd:(g + 1) * gd]) for g in range(len(POOL_WINDOWS))], axis=1)
        pooled_parts.append(sums / cnt - zb_b)
        hp_scr[bi] = full[cfg.kp - POOL_HIST:]
        npool_ref[bi] = hp_scr[bi, pl.ds(POOL_HIST - POOL_STATE, POOL_STATE), :]
    pooled = (pooled_parts[0] if bt == 1 else jnp.concatenate(pooled_parts, axis=0)).astype(BF16)
    y_b = jnp.concatenate(
        [_dot(pooled[:, g * gd:(g + 1) * gd], wpool_ref[g]) for g in range(len(POOL_WINDOWS))], axis=1)
    y_b = (y_b + bpool_ref[...]) * pscale_ref[...]
    merged = jax.nn.sigmoid(zcols(off_g + d, off_g + 2 * d)) * y_b

    u = jax.nn.gelu(zcols(off_u, off_u + d))
    v = _ln(jax.nn.gelu(zcols(off_v, off_v + d)), lnvg_ref[...], lnvb_ref[...])
    if cfg.emit_v:
        v_ref[...] = v.reshape(bt, t, d)
    vb = v.astype(BF16)
    mixed_rows = []
    for r0 in range(0, m, cfg.ch):
        vc = vb[r0:r0 + cfg.ch]
        if cfg.kg > cfg.ch:
            vc = jnp.concatenate([vc, jnp.zeros((cfg.kg - cfg.ch, d), BF16)], axis=0)
        mixed_rows.append(jnp.concatenate(
            [_dot(wsp_ref[hh], vc[:, hh * hd:(hh + 1) * hd]) for hh in range(GMLP_HEADS)], axis=1) + bsf_ref[...])
    mixed = mixed_rows[0] if len(mixed_rows) == 1 else jnp.concatenate(mixed_rows, axis=0)
    y_a = _dot((u * mixed).astype(BF16), wao_ref[...])
    merged = merged + jax.nn.sigmoid(zcols(off_g, off_g + d)) * y_a

    x_c = zcols(off_c, off_c + d) * jax.nn.sigmoid(zcols(off_c + d, off_c + 2 * d))
    conv_parts = []
    for bi in range(bt):
        accs = []
        for c in range(n_slab):
            cs = slice(c * LANES, (c + 1) * LANES)
            xc_scr[bi, c, CONV_HIST:CONV_HIST + t, :] = x_c[bi * t:(bi + 1) * t, cs]
            acc = jnp.broadcast_to(bdw_ref[:, cs], (t, LANES))
            for k in range(CONV_K):
                acc = acc + wdw_ref[k:k + 1, cs] * xc_scr[bi, c, pl.ds(CONV_HIST - CONV_STATE + k, t), :]
            accs.append(acc)
            tail = xc_scr[bi, c, t:t + CONV_HIST, :]
            nconv_ref[bi, :, cs] = xc_scr[bi, c, pl.ds(t + CONV_HIST - CONV_STATE, CONV_STATE), :]
            xc_scr[bi, c, 0:CONV_HIST, :] = tail
        conv_parts.append(jnp.concatenate(accs, axis=1))
    conv = conv_parts[0] if bt == 1 else jnp.concatenate(conv_parts, axis=0)
    y_c = _dot(jax.nn.silu(_ln(conv, lncg_ref[...], lncb_ref[...])).astype(BF16), wco_ref[...])
    merged = merged + jax.nn.sigmoid(zcols(off_g + 2 * d, off_g + 3 * d)) * y_c

    x_new = x + g1 * _dot(merged.astype(BF16), wout_ref[...])
    xo_ref[...] = x_new.reshape(bt, t, d)
    sh2, sc2 = mod_rows(3), mod_rows(4)
    h2 = _rms(x_new, nfg_ref[...]) * (1.0 + sc2) + sh2
    h2_ref[...] = h2.reshape(bt, t, d)
    h2_hi = h2.astype(BF16)
    h2_lo = (h2 - h2_hi.astype(F32)).astype(BF16)
    logits = _dot(h2_hi, wrh_ref[...]) + _dot(h2_lo, wrh_ref[...]) + _dot(h2_hi, wrl_ref[...]) + br_ref[...]
    lane_r = lax.broadcasted_iota(jnp.int32, (m, ROUTER_LANES), 1).astype(F32)
    lane_k = lax.broadcasted_iota(jnp.int32, (m, TOPK_LANES), 1)
    work = logits
    top_vals, onehots = [], []
    ti = jnp.zeros((m, TOPK_LANES), jnp.int32)
    for r in range(TOP_K):
        mx = jnp.max(work, axis=-1, keepdims=True)
        idx = jnp.min(jnp.where(work == mx, lane_r, float(ROUTER_LANES)), axis=-1, keepdims=True)
        ti = jnp.where(lane_k == r, idx.astype(jnp.int32), ti)
        top_vals.append(mx)
        work = jnp.where(lane_r == idx, -jnp.inf, work)
        onehots.append((lane_r == idx).astype(F32))
    exps = [jnp.exp(tv - top_vals[0]) for tv in top_vals]
    denom = exps[0] + exps[1] + exps[2] + exps[3]
    tg = jnp.zeros((m, TOPK_LANES), F32)
    for r in range(TOP_K):
        tg = jnp.where(lane_k == r, exps[r] / denom, tg)
    ti_ref[...] = ti
    tg_ref[...] = tg

    @pl.when(jnp.logical_and(pl.program_id(0) == 0, j == 0))
    def _init_counts():
        run_scr[...] = cnt0_ref[...]

    base = run_scr[...]
    rk = jnp.zeros((m, TOPK_LANES), jnp.int32)
    for r in range(TOP_K):
        before = _dot(ltri_ref[...], onehots[r].astype(BF16)) + base
        rank = jnp.sum(onehots[r] * before, axis=-1, keepdims=True)
        rk = jnp.where(lane_k == r, rank.astype(jnp.int32), rk)
        base = base + jnp.sum(onehots[r], axis=0, keepdims=True)
    run_scr[...] = base
    rk_ref[...] = rk
    cnt_ref[...] = jnp.broadcast_to(base, cnt_ref.shape)


def _const_spec(shape):
    nd = len(shape)
    return pl.BlockSpec(shape, lambda b, j, _nd=nd: (0,) * _nd, pipeline_mode=pl.Buffered(1))


def _pool_band(t, kp):
    col = jnp.arange(kp)[None, :]
    end = (kp - t) + jnp.arange(t)[:, None]
    return jnp.stack([((col <= end) & (col > end - w)) for w in POOL_WINDOWS]).astype(BF16)


def _mixer_call(cfg, x, mod, prev_pool, prev_conv, lw, cnt0):
    b, s, d = x.shape
    bt, t = cfg.bt, cfg.t
    m = bt * t
    grid = (b // bt, s // t)
    n_slab = d // LANES

    def tile_spec():
        return pl.BlockSpec((bt, t, d), lambda bi, j: (bi, j, 0))

    def batch_spec(rows, width):
        return pl.BlockSpec((bt, rows, width), lambda bi, j: (bi, 0, 0))

    def rows_spec(width):
        return pl.BlockSpec((m, width), lambda bi, j: (bi * (s // t) + j, 0))

    consts = [lw["norm_mix_g"], lw["norm_ffn_g"], lw["w_in"], lw["ln_v_g"], lw["ln_v_b"], lw["w_sp"], lw["bs_full"],
              lw["w_a_out"], lw["w_pool"], lw["b_pool"], lw["pool_scale"], lw["w_dw"], lw["b_dw"], lw["ln_c_g"],
              lw["ln_c_b"], lw["w_c_out"], lw["w_out"], lw["wr_hi"], lw["wr_lo"], lw["b_router"], lw["pmat"],
              lw["ltri"], cnt0]
    in_specs = [tile_spec(), batch_spec(1, 6 * d), batch_spec(POOL_HIST, d), batch_spec(CONV_HIST, d)]
    in_specs += [_const_spec(c.shape) for c in consts]
    out_shape = [jax.ShapeDtypeStruct((b, s, d), F32), jax.ShapeDtypeStruct((b, s, d), F32),
                 jax.ShapeDtypeStruct((b * s, TOPK_LANES), jnp.int32), jax.ShapeDtypeStruct((b * s, TOPK_LANES), F32),
                 jax.ShapeDtypeStruct((b * s, TOPK_LANES), jnp.int32),
                 jax.ShapeDtypeStruct((SUBLANES, ROUTER_LANES), F32),
                 jax.ShapeDtypeStruct((b, POOL_STATE, d), F32), jax.ShapeDtypeStruct((b, CONV_STATE, d), F32)]
    out_specs = [tile_spec(), tile_spec(), rows_spec(TOPK_LANES), rows_spec(TOPK_LANES), rows_spec(TOPK_LANES),
                 pl.BlockSpec((SUBLANES, ROUTER_LANES), lambda bi, j: (0, 0)),
                 batch_spec(POOL_STATE, d), batch_spec(CONV_STATE, d)]
    if cfg.emit_v:
        out_shape.append(jax.ShapeDtypeStruct((b, s, d), F32))
        out_specs.append(tile_spec())
    return pl.pallas_call(
        functools.partial(_mixer_kernel, cfg),
        grid=grid,
        in_specs=in_specs,
        out_specs=out_specs,
        out_shape=out_shape,
        scratch_shapes=[pltpu.VMEM((bt, POOL_HIST, d), F32),
                        pltpu.VMEM((bt, n_slab, CONV_HIST + t, LANES), F32),
                        pltpu.VMEM((1, ROUTER_LANES), F32)],
        compiler_params=pltpu.CompilerParams(dimension_semantics=("arbitrary", "arbitrary"),
                                             vmem_limit_bytes=V7X_VMEM_LIMIT_BYTES),
        name=f"mixer_t{t}",
    )(x, mod, prev_pool, prev_conv, *consts)


def _ada_kernel(c_ref, w_ref, b_ref, o_ref):
    o_ref[0] = _dot(jax.nn.silu(c_ref[...]).astype(BF16), w_ref[0].astype(BF16)) + b_ref[0]


def _ada_call(c_all, w_ada, b_ada):
    n_layers, d, six_d = w_ada.shape
    rows = c_all.shape[0]
    bn = six_d // 6
    return pl.pallas_call(
        _ada_kernel,
        grid=(n_layers, six_d // bn),
        in_specs=[pl.BlockSpec((rows, d), lambda l, n: (0, 0)),
                  pl.BlockSpec((1, d, bn), lambda l, n: (l, 0, n)),
                  pl.BlockSpec((1, 1, bn), lambda l, n: (l, 0, n))],
        out_specs=pl.BlockSpec((1, rows, bn), lambda l, n: (l, 0, n)),
        out_shape=jax.ShapeDtypeStruct((n_layers, rows, six_d), F32),
        compiler_params=pltpu.CompilerParams(dimension_semantics=("arbitrary", "arbitrary")),
        name="adaln",
    )(c_all, w_ada, b_ada.reshape(n_layers, 1, six_d))


def _moe_kernel(be_ref, nused_ref, x_ref, wgu_ref, bgu_ref, wdn_ref, bdn_ref, y_ref, wgu_bf, wdn_bf):
    i = pl.program_id(0)
    nused = nused_ref[0]
    dff = wdn_bf.shape[0]

    @pl.when(i < nused)
    def _body():
        changed = jnp.logical_or(i == 0, be_ref[i] != be_ref[jnp.maximum(i - 1, 0)])

        @pl.when(changed)
        def _cast_weights():
            wgu_bf[...] = wgu_ref[0].astype(BF16)
            wdn_bf[...] = wdn_ref[0].astype(BF16)

        gu = _dot(x_ref[...].astype(BF16), wgu_bf[...]) + bgu_ref[0]
        g = jnp.minimum(gu[:, :dff], SWIGLU_LIMIT)
        u = jnp.clip(gu[:, dff:], -SWIGLU_LIMIT, SWIGLU_LIMIT)
        act = (u + 1.0) * g * jax.nn.sigmoid(SWIGLU_ALPHA * g)
        y_ref[...] = _dot(act.astype(BF16), wdn_bf[...]) + bdn_ref[0]

    @pl.when(i >= nused)
    def _unused_block():
        y_ref[...] = jnp.zeros_like(y_ref)


def _moe_call(xs, block_e, nused, w_gu, b_gu, w_dn, b_dn):
    n_blocks = block_e.shape[0]
    n_exp, d, two_f = w_gu.shape
    dff = two_f // 2
    tm = MOE_TM
    grid_spec = pltpu.PrefetchScalarGridSpec(
        num_scalar_prefetch=2,
        grid=(n_blocks,),
        in_specs=[
            pl.BlockSpec((tm, d), lambda i, be, nu: (i, 0)),
            pl.BlockSpec((1, d, two_f), lambda i, be, nu: (be[i], 0, 0)),
            pl.BlockSpec((1, 1, two_f), lambda i, be, nu: (be[i], 0, 0)),
            pl.BlockSpec((1, dff, d), lambda i, be, nu: (be[i], 0, 0)),
            pl.BlockSpec((1, 1, d), lambda i, be, nu: (be[i], 0, 0)),
        ],
        out_specs=pl.BlockSpec((tm, d), lambda i, be, nu: (i, 0)),
        scratch_shapes=[pltpu.VMEM((d, two_f), BF16), pltpu.VMEM((dff, d), BF16)],
    )
    return pl.pallas_call(
        _moe_kernel,
        grid_spec=grid_spec,
        out_shape=jax.ShapeDtypeStruct((n_blocks * tm, d), F32),
        compiler_params=pltpu.CompilerParams(dimension_semantics=("arbitrary",),
                                             vmem_limit_bytes=V7X_VMEM_LIMIT_BYTES),
        name="moe_experts",
    )(block_e, nused, xs, w_gu, b_gu.reshape(n_exp, 1, two_f), w_dn, b_dn.reshape(n_exp, 1, d))


def _route(top_i, rank, counts):
    tm = MOE_TM
    n_all = top_i.shape[0]
    n_blocks = n_all * TOP_K // tm + N_EXPERTS
    padded = (counts + tm - 1) // tm * tm
    pad_ends = jnp.cumsum(padded)
    pad_starts = pad_ends - padded
    blk = jnp.arange(n_blocks, dtype=jnp.int32)[:, None] * tm
    block_e = jnp.minimum(jnp.sum((pad_ends[None, :] <= blk).astype(jnp.int32), axis=1), N_EXPERTS - 1)
    nused = (pad_ends[-1] // tm).astype(jnp.int32).reshape(1)
    onehot = top_i[:, :, None] == jnp.arange(N_EXPERTS, dtype=jnp.int32)[None, None, :]
    slot_of = jnp.sum(jnp.where(onehot, pad_starts[None, None, :], 0), axis=-1) + rank
    return slot_of.T.astype(jnp.int32), block_e.astype(jnp.int32), nused


SC_CORES = 2
SC_SUBCORES = 16
SC_WORKERS = SC_CORES * SC_SUBCORES
SC_CHUNK = 32


def _sc_worker_id():
    return lax.axis_index("s") * SC_CORES + lax.axis_index("c")


def _sc_dispatch(hp, hs, slot_p, slot_s, n_slots):
    (n_p, d), n_s = hp.shape, hs.shape[0]
    tp, ts = n_p // SC_WORKERS, n_s // SC_WORKERS
    cp, cs = min(SC_CHUNK, tp), min(SC_CHUNK, ts)
    ncp, ncs = tp // cp, ts // cs
    assert tp * SC_WORKERS == n_p and ts * SC_WORKERS == n_s and ncp * cp == tp and ncs * cs == ts
    assert ncp % 2 == 0

    def body(hp_hbm, hs_hbm, idxp_hbm, idxs_hbm, out_hbm, idxp_v, idxs_v, buf, sbuf, rsem, wsem):
        wid = _sc_worker_id()
        pltpu.sync_copy(idxp_hbm.at[wid], idxp_v)
        pltpu.sync_copy(idxs_hbm.at[wid], idxs_v)

        def read(c, b):
            return pltpu.make_async_copy(hp_hbm.at[pl.ds(wid * tp + c * cp, cp)], buf.at[b], rsem.at[b])

        def write(c, k, b):
            return pltpu.make_async_copy(buf.at[b], out_hbm.at[idxp_v.at[c * TOP_K + k]], wsem.at[b])

        read(0, 0).start()

        @pl.loop(0, ncp, step=2)
        def _(c0):
            for b in range(2):
                c = c0 + b

                @pl.when(c + 1 < ncp)
                def _():
                    @pl.when(c >= 1)
                    def _():
                        for k in range(TOP_K):
                            write(c - 1, k, 1 - b).wait()
                    read(c + 1, 1 - b).start()

                read(c, b).wait()
                for k in range(TOP_K):
                    write(c, k, b).start()

        for c in (ncp - 2, ncp - 1):
            for k in range(TOP_K):
                write(c, k, c % 2).wait()

        for c in range(ncs):
            pltpu.sync_copy(hs_hbm.at[pl.ds(wid * ts + c * cs, cs)], sbuf)
            for k in range(TOP_K):
                pltpu.sync_copy(sbuf, out_hbm.at[idxs_v.at[c * TOP_K + k]])

    call = pl.kernel(
        body,
        out_type=jax.ShapeDtypeStruct((n_slots, d), F32),
        mesh=plsc.VectorSubcoreMesh(core_axis_name="c", subcore_axis_name="s"),
        scratch_types=[pltpu.VMEM((ncp * TOP_K, cp), jnp.int32), pltpu.VMEM((ncs * TOP_K, cs), jnp.int32),
                       pltpu.VMEM((2, cp, d), F32), pltpu.VMEM((cs, d), F32),
                       pltpu.SemaphoreType.DMA((2,)), pltpu.SemaphoreType.DMA((2,))],
        name="sc_dispatch",
    )
    idx_p = slot_p.reshape(TOP_K, SC_WORKERS, ncp, cp).transpose(1, 2, 0, 3).reshape(SC_WORKERS, ncp * TOP_K, cp)
    idx_s = slot_s.reshape(TOP_K, SC_WORKERS, ncs, cs).transpose(1, 2, 0, 3).reshape(SC_WORKERS, ncs * TOP_K, cs)
    return call(hp, hs, idx_p, idx_s)


def _sc_collect(ys, slot_p, slot_s):
    d = ys.shape[1]
    rows_p, rows_s = slot_p.size, slot_s.size
    per_p, per_s = rows_p // SC_WORKERS, rows_s // SC_WORKERS
    ch = SC_CHUNK
    assert per_p * SC_WORKERS == rows_p and per_s * SC_WORKERS == rows_s
    assert per_p % (2 * ch) == 0 and per_s % (2 * ch) == 0

    def body(ys_hbm, idxp_hbm, idxs_hbm, outp_hbm, outs_hbm, idxp_v, idxs_v, rows_v, gsem, wsem):
        wid = _sc_worker_id()

        def segment(idx_hbm, idx_v, out_hbm, per_w):
            n_chunks = per_w // ch
            base = wid * per_w
            pltpu.sync_copy(idx_hbm.at[pl.ds(base, per_w)], idx_v)

            def gather(g, b):
                return pltpu.make_async_copy(ys_hbm.at[idx_v.at[pl.ds(g * ch, ch)]], rows_v.at[b], gsem.at[b])

            def put(g, b):
                return pltpu.make_async_copy(rows_v.at[b], out_hbm.at[pl.ds(base + g * ch, ch)], wsem.at[b])

            gather(0, 0).start()

            @pl.loop(0, n_chunks, step=2)
            def _(g0):
                for b in range(2):
                    g = g0 + b

                    @pl.when(g + 1 < n_chunks)
                    def _():
                        @pl.when(g >= 1)
                        def _():
                            put(g - 1, 1 - b).wait()
                        gather(g + 1, 1 - b).start()

                    gather(g, b).wait()
                    put(g, b).start()

            for g in (n_chunks - 2, n_chunks - 1):
                put(g, g % 2).wait()

        segment(idxp_hbm, idxp_v, outp_hbm, per_p)
        segment(idxs_hbm, idxs_v, outs_hbm, per_s)

    call = pl.kernel(
        body,
        out_type=(jax.ShapeDtypeStruct((rows_p, d), F32), jax.ShapeDtypeStruct((rows_s, d), F32)),
        mesh=plsc.VectorSubcoreMesh(core_axis_name="c", subcore_axis_name="s"),
        scratch_types=[pltpu.VMEM((per_p,), jnp.int32), pltpu.VMEM((per_s,), jnp.int32),
                       pltpu.VMEM((2, ch, d), F32),
                       pltpu.SemaphoreType.DMA((2,)), pltpu.SemaphoreType.DMA((2,))],
        name="sc_collect",
    )
    return call(ys, slot_p.reshape(rows_p), slot_s.reshape(rows_s))


def _combine_kernel(final, x_ref, mod_ref, y_ref, tg_ref, fg_ref, o_ref):
    bt, t, d = x_ref.shape
    m = bt * t
    g2 = mod_ref[:, :, 5 * d:6 * d]
    g2 = g2.reshape(1, d) if bt == 1 else jnp.broadcast_to(g2, (bt, t, d)).reshape(m, d)
    tg = tg_ref[...]
    f = tg[:, 0:1] * y_ref[0]
    for k in range(1, TOP_K):
        f = f + tg[:, k:k + 1] * y_ref[k]
    out = x_ref[...].reshape(m, d) + g2 * f
    if final:
        out = _rms(out, fg_ref[...])
    o_ref[...] = out.reshape(bt, t, d)


def _combine_call(x_new, mod, y4, tg, final_g, bt, t, final):
    b, s, d = x_new.shape
    m = bt * t
    steps = s // t
    return pl.pallas_call(
        functools.partial(_combine_kernel, final),
        grid=(b // bt, steps),
        in_specs=[pl.BlockSpec((bt, t, d), lambda bi, j: (bi, j, 0)),
                  pl.BlockSpec((bt, 1, 6 * d), lambda bi, j: (bi, 0, 0)),
                  pl.BlockSpec((TOP_K, m, d), lambda bi, j: (0, bi * steps + j, 0)),
                  pl.BlockSpec((m, TOPK_LANES), lambda bi, j: (bi * steps + j, 0)),
                  pl.BlockSpec((1, d), lambda bi, j: (0, 0))],
        out_specs=pl.BlockSpec((bt, t, d), lambda bi, j: (bi, j, 0)),
        out_shape=jax.ShapeDtypeStruct((b, s, d), F32),
        compiler_params=pltpu.CompilerParams(dimension_semantics=("arbitrary", "arbitrary")),
        name=f"combine_t{t}",
    )(x_new, mod, y4, tg, final_g)


def _layer_weights(l, p, cfgs):
    d = p["w_out"].shape[-1]
    row = lambda v: v[l].reshape(1, d)
    wr = p["w_router"][l]
    wr_pad = jnp.zeros((d, ROUTER_LANES), F32).at[:, :N_EXPERTS].set(wr)
    wr_hi = wr_pad.astype(BF16)
    wr_lo = (wr_pad - wr_hi.astype(F32)).astype(BF16)
    b_router = jnp.full((1, ROUTER_LANES), -1e30, F32).at[0, :N_EXPERTS].set(p["b_router"][l])
    mask = jnp.tril(jnp.ones((GMLP_CHUNK, GMLP_CHUNK), dtype=bool))
    w_sp_full = jnp.where(mask[None], p["w_spatial"][l], 0)
    hd = d // GMLP_HEADS
    base = dict(norm_mix_g=row(p["norm_mix_g"]), norm_ffn_g=row(p["norm_ffn_g"]), w_in=p["w_in"][l].astype(BF16),
                ln_v_g=row(p["ln_v_g"]), ln_v_b=row(p["ln_v_b"]), w_a_out=p["w_a_out"][l].astype(BF16),
                w_pool=p["w_pool"][l].astype(BF16), b_pool=row(p["b_pool"]), pool_scale=row(p["pool_scale"]),
                w_dw=jnp.zeros((_round_up(CONV_K, SUBLANES), d), F32).at[:CONV_K].set(p["w_dw"][l]),
                b_dw=row(p["b_dw"]), ln_c_g=row(p["ln_c_g"]), ln_c_b=row(p["ln_c_b"]),
                w_c_out=p["w_c_out"][l].astype(BF16), w_out=p["w_out"][l].astype(BF16),
                wr_hi=wr_hi, wr_lo=wr_lo, b_router=b_router)
    out = []
    for cfg in cfgs:
        w_sp = jnp.zeros((GMLP_HEADS, cfg.ch, cfg.kg), F32).at[:, :, :cfg.ch].set(w_sp_full[:, :cfg.ch, :cfg.ch])
        bs_full = jnp.repeat(p["b_spatial"][l][:, :cfg.ch].T, hd, axis=1)
        m = cfg.bt * cfg.t
        ltri = (jnp.arange(m)[:, None] > jnp.arange(m)[None, :]).astype(BF16)
        out.append(dict(base, w_sp=w_sp.astype(BF16), bs_full=bs_full, pmat=_pool_band(cfg.t, cfg.kp), ltri=ltri))
    return out


def _pad_front(state, rows):
    pad = rows - state.shape[-2]
    return jnp.pad(state, ((0, 0),) * (state.ndim - 2) + ((pad, 0), (0, 0)))


def kernel(x_prompt, x_sample, c_prompt, c_sample, state_pool, state_conv, norm_mix_g, norm_ffn_g, w_ada, b_ada, w_in, ln_v_g, ln_v_b, w_spatial, b_spatial, w_a_out, w_pool, b_pool, pool_scale, w_dw, b_dw, ln_c_g, ln_c_b, w_c_out, w_out, w_router, b_router, w_gate_up, b_gate_up, w_down, b_down, final_norm_g):
    p = dict(norm_mix_g=norm_mix_g, norm_ffn_g=norm_ffn_g, w_in=w_in, ln_v_g=ln_v_g, ln_v_b=ln_v_b,
             w_spatial=w_spatial, b_spatial=b_spatial, w_a_out=w_a_out, w_pool=w_pool, b_pool=b_pool,
             pool_scale=pool_scale, w_dw=w_dw, b_dw=b_dw, ln_c_g=ln_c_g, ln_c_b=ln_c_b, w_c_out=w_c_out,
             w_out=w_out, w_router=w_router, b_router=b_router)
    n_layers = w_in.shape[0]
    bp, sp, d = x_prompt.shape
    bs, ss, _ = x_sample.shape
    tp = min(PROMPT_TILE, sp)
    bts = min(SAMPLE_BT, bs)
    cfg_p = MixerCfg(bt=1, t=tp, ch=GMLP_CHUNK, kg=GMLP_CHUNK, kp=_round_up(POOL_HIST + tp, LANES), d=d,
                     start_pos=0, emit_v=False)
    cfg_s = MixerCfg(bt=bts, t=ss, ch=ss, kg=_round_up(ss, LANES), kp=_round_up(POOL_HIST + ss, LANES), d=d,
                     start_pos=PAST_LEN, emit_v=True)
    n_p, n_s = bp * sp, bs * ss
    n_slots = (n_p + n_s) * TOP_K + N_EXPERTS * MOE_TM

    mod_all = _ada_call(jnp.concatenate([c_prompt, c_sample], axis=0), w_ada, b_ada)
    mod_p = mod_all[:, :bp].reshape(n_layers, bp, 1, 6 * d)
    mod_s = mod_all[:, bp:].reshape(n_layers, bs, 1, 6 * d)
    final_g = final_norm_g.reshape(1, d)

    xp, xs = x_prompt, x_sample
    zero_pool = jnp.zeros((bp, POOL_HIST, d), F32)
    zero_conv = jnp.zeros((bp, CONV_HIST, d), F32)
    zero_cnt = jnp.zeros((1, ROUTER_LANES), F32)
    pools_p, convs_p, pools_s, convs_s, vs = [], [], [], [], []
    for l in range(n_layers):
        lw_p, lw_s = _layer_weights(l, p, (cfg_p, cfg_s))
        xp_new, h2p, tip, tgp, rkp, cnt_p, npool_p, nconv_p = _mixer_call(
            cfg_p, xp, mod_p[l], zero_pool, zero_conv, lw_p, zero_cnt)
        xs_new, h2s, tis, tgs, rks, cnt_all, npool_s, nconv_s, v_s = _mixer_call(
            cfg_s, xs, mod_s[l], _pad_front(state_pool[l], POOL_HIST), _pad_front(state_conv[l], CONV_HIST),
            lw_s, cnt_p[0:1])
        top_i = jnp.concatenate([tip, tis], axis=0)[:, :TOP_K]
        rank = jnp.concatenate([rkp, rks], axis=0)[:, :TOP_K]
        counts = cnt_all[0, :N_EXPERTS].astype(jnp.int32)
        slot_of, block_e, nused = _route(top_i, rank, counts)
        slot_p, slot_s = slot_of[:, :n_p], slot_of[:, n_p:]
        xs_sorted = _sc_dispatch(h2p.reshape(n_p, d), h2s.reshape(n_s, d), slot_p, slot_s, n_slots)
        ys = _moe_call(xs_sorted, block_e, nused, w_gate_up[l], b_gate_up[l], w_down[l], b_down[l])
        y4p, y4s = _sc_collect(ys, slot_p, slot_s)
        final = l == n_layers - 1
        xp = _combine_call(xp_new, mod_p[l], y4p.reshape(TOP_K, n_p, d), tgp, final_g, 1, tp, final)
        xs = _combine_call(xs_new, mod_s[l], y4s.reshape(TOP_K, n_s, d), tgs, final_g, bts, ss, final)
        pools_p.append(npool_p)
        convs_p.append(nconv_p)
        pools_s.append(npool_s)
        convs_s.append(nconv_s)
        vs.append(v_s)
    return (xp, xs, jnp.stack(pools_p), jnp.stack(convs_p), jnp.stack(pools_s), jnp.stack(convs_s), jnp.stack(vs))
```

```python
import functools
from typing import NamedTuple

import jax
import jax.numpy as jnp
from jax import lax
from jax.experimental import pallas as pl
from jax.experimental.pallas import tpu as pltpu
from jax.experimental.pallas import tpu_sc as plsc

GMLP_CHUNK = 128
GMLP_HEADS = 4
POOL_WINDOWS = (2, 4, 8, 16)
POOL_STATE = max(POOL_WINDOWS) - 1
CONV_K = 31
CONV_STATE = CONV_K - 1
N_BRANCH = 3
N_EXPERTS = 32
TOP_K = 4
SWIGLU_LIMIT = 7.0
SWIGLU_ALPHA = 1.702
EPS = 1e-6
PAST_LEN = 2048

LANES = 128
SUBLANES = 8
V7X_VMEM_LIMIT_BYTES = 56 * 2**20

POOL_HIST = 16
CONV_HIST = 32
ROUTER_LANES = LANES
TOPK_LANES = 8
MOE_TM = 512
PROMPT_TILE = 256
SAMPLE_BT = 8

F32 = jnp.float32
BF16 = jnp.bfloat16


def _round_up(a, m):
    return (a + m - 1) // m * m


class MixerCfg(NamedTuple):
    bt: int
    t: int
    ch: int
    kg: int
    kp: int
    d: int
    start_pos: int
    emit_v: bool


def _rms(x, g):
    return x * lax.rsqrt(jnp.mean(x * x, axis=-1, keepdims=True) + EPS) * g


def _ln(x, g, b):
    mu = jnp.mean(x, axis=-1, keepdims=True)
    xc = x - mu
    var = jnp.mean(xc * xc, axis=-1, keepdims=True)
    return xc * lax.rsqrt(var + EPS) * g + b


def _dot(a, b):
    return jnp.dot(a, b, preferred_element_type=F32)


def _mixer_kernel(cfg, x_ref, mod_ref, pp_ref, pc_ref, nmg_ref, nfg_ref, win_ref, lnvg_ref, lnvb_ref,
                  wsp_ref, bsf_ref, wao_ref, wpool_ref, bpool_ref, pscale_ref, wdw_ref, bdw_ref,
                  lncg_ref, lncb_ref, wco_ref, wout_ref, wrh_ref, wrl_ref, br_ref, pmat_ref, ltri_ref, cnt0_ref,
                  xo_ref, h2_ref, ti_ref, tg_ref, rk_ref, cnt_ref, npool_ref, nconv_ref, *rest):
    if cfg.emit_v:
        v_ref, hp_scr, xc_scr, run_scr = rest
    else:
        hp_scr, xc_scr, run_scr = rest
    bt, t, d = cfg.bt, cfg.t, cfg.d
    m = bt * t
    n_slab = d // LANES
    gd = d // len(POOL_WINDOWS)
    hd = d // GMLP_HEADS
    j = pl.program_id(1)

    def mod_rows(k):
        r = mod_ref[:, :, k * d:(k + 1) * d]
        if bt == 1:
            return r.reshape(1, d)
        return jnp.broadcast_to(r, (bt, t, d)).reshape(m, d)

    @pl.when(j == 0)
    def _load_state():
        hp_scr[...] = pp_ref[...]
        for bi in range(bt):
            for c in range(n_slab):
                xc_scr[bi, c, 0:CONV_HIST, :] = pc_ref[bi, :, c * LANES:(c + 1) * LANES]

    x = x_ref[...].reshape(m, d)
    sh1, sc1, g1 = mod_rows(0), mod_rows(1), mod_rows(2)
    h = (_rms(x, nmg_ref[...]) * (1.0 + sc1) + sh1).astype(BF16)

    def zcols(lo, hi):
        return _dot(h, win_ref[:, lo:hi])

    off_u, off_v, off_b, off_c, off_g = 0, d, 2 * d, 3 * d, 5 * d

    zb = zcols(off_b, off_b + d)
    lane = lax.broadcasted_iota(jnp.int32, (1, d), 1)
    win_lane = jnp.left_shift(2, lane // gd).astype(F32)
    pos1 = (cfg.start_pos + 1 + j * t + lax.broadcasted_iota(jnp.int32, (t, 1), 0)).astype(F32)
    cnt = jnp.minimum(pos1, win_lane)
    pooled_parts = []
    for bi in range(bt):
        zb_b = zb[bi * t:(bi + 1) * t]
        pieces = [hp_scr[bi], zb_b]
        if cfg.kp > POOL_HIST + t:
            pieces = [jnp.zeros((cfg.kp - POOL_HIST - t, d), F32)] + pieces
        full = jnp.concatenate(pieces, axis=0)
        fullb = full.astype(BF16)
        sums = jnp.concatenate(
            [_dot(pmat_ref[g], fullb[:, g * gd:(g + 1) * gd]) for g in range(len(POOL_WINDOWS))], axis=1)
        pooled_parts.append(sums / cnt - zb_b)
        hp_scr[bi] = full[cfg.kp - POOL_HIST:]
        npool_ref[bi] = hp_scr[bi, pl.ds(POOL_HIST - POOL_STATE, POOL_STATE), :]
    pooled = (pooled_parts[0] if bt == 1 else jnp.concatenate(pooled_parts, axis=0)).astype(BF16)
    y_b = jnp.concatenate(
        [_dot(pooled[:, g * gd:(g + 1) * gd], wpool_ref[g]) for g in range(len(POOL_WINDOWS))], axis=1)
    y_b = (y_b + bpool_ref[...]) * pscale_ref[...]
    merged = jax.nn.sigmoid(zcols(off_g + d, off_g + 2 * d)) * y_b

    u = jax.nn.gelu(zcols(off_u, off_u + d))
    v = _ln(jax.nn.gelu(zcols(off_v, off_v + d)), lnvg_ref[...], lnvb_ref[...])
    if cfg.emit_v:
        v_ref[...] = v.reshape(bt, t, d)
    vb = v.astype(BF16)
    mixed_rows = []
    for r0 in range(0, m, cfg.ch):
        vc = vb[r0:r0 + cfg.ch]
        if cfg.kg > cfg.ch:
            vc = jnp.concatenate([vc, jnp.zeros((cfg.kg - cfg.ch, d), BF16)], axis=0)
        mixed_rows.append(jnp.concatenate(
            [_dot(wsp_ref[hh], vc[:, hh * hd:(hh + 1) * hd]) for hh in range(GMLP_HEADS)], axis=1) + bsf_ref[...])
    mixed = mixed_rows[0] if len(mixed_rows) == 1 else jnp.concatenate(mixed_rows, axis=0)
    y_a = _dot((u * mixed).astype(BF16), wao_ref[...])
    merged = merged + jax.nn.sigmoid(zcols(off_g, off_g + d)) * y_a

    x_c = zcols(off_c, off_c + d) * jax.nn.sigmoid(zcols(off_c + d, off_c + 2 * d))
    conv_parts = []
    for bi in range(bt):
        accs = []
        for c in range(n_slab):
            cs = slice(c * LANES, (c + 1) * LANES)
            xc_scr[bi, c, CONV_HIST:CONV_HIST + t, :] = x_c[bi * t:(bi + 1) * t, cs]
            acc = jnp.broadcast_to(bdw_ref[:, cs], (t, LANES))
            for k in range(CONV_K):
                acc = acc + wdw_ref[k:k + 1, cs] * xc_scr[bi, c, pl.ds(CONV_HIST - CONV_STATE + k, t), :]
            accs.append(acc)
            tail = xc_scr[bi, c, t:t + CONV_HIST, :]
            nconv_ref[bi, :, cs] = xc_scr[bi, c, pl.ds(t + CONV_HIST - CONV_STATE, CONV_STATE), :]
            xc_scr[bi, c, 0:CONV_HIST, :] = tail
        conv_parts.append(jnp.concatenate(accs, axis=1))
    conv = conv_parts[0] if bt == 1 else jnp.concatenate(conv_parts, axis=0)
    y_c = _dot(jax.nn.silu(_ln(conv, lncg_ref[...], lncb_ref[...])).astype(BF16), wco_ref[...])
    merged = merged + jax.nn.sigmoid(zcols(off_g + 2 * d, off_g + 3 * d)) * y_c

    x_new = x + g1 * _dot(merged.astype(BF16), wout_ref[...])
    xo_ref[...] = x_new.reshape(bt, t, d)
    sh2, sc2 = mod_rows(3), mod_rows(4)
    h2 = _rms(x_new, nfg_ref[...]) * (1.0 + sc2) + sh2
    h2_ref[...] = h2.reshape(bt, t, d)
    h2_hi = h2.astype(BF16)
    h2_lo = (h2 - h2_hi.astype(F32)).astype(BF16)
    logits = _dot(h2_hi, wrh_ref[...]) + _dot(h2_lo, wrh_ref[...]) + _dot(h2_hi, wrl_ref[...]) + br_ref[...]
    lane_r = lax.broadcasted_iota(jnp.int32, (m, ROUTER_LANES), 1).astype(F32)
    lane_k = lax.broadcasted_iota(jnp.int32, (m, TOPK_LANES), 1)
    work = logits
    top_vals, onehots = [], []
    ti = jnp.zeros((m, TOPK_LANES), jnp.int32)
    for r in range(TOP_K):
        mx = jnp.max(work, axis=-1, keepdims=True)
        idx = jnp.min(jnp.where(work == mx, lane_r, float(ROUTER_LANES)), axis=-1, keepdims=True)
        ti = jnp.where(lane_k == r, idx.astype(jnp.int32), ti)
        top_vals.append(mx)
        work = jnp.where(lane_r == idx, -jnp.inf, work)
        onehots.append((lane_r == idx).astype(F32))
    exps = [jnp.exp(tv - top_vals[0]) for tv in top_vals]
    denom = exps[0] + exps[1] + exps[2] + exps[3]
    tg = jnp.zeros((m, TOPK_LANES), F32)
    for r in range(TOP_K):
        tg = jnp.where(lane_k == r, exps[r] / denom, tg)
    ti_ref[...] = ti
    tg_ref[...] = tg

    @pl.when(jnp.logical_and(pl.program_id(0) == 0, j == 0))
    def _init_counts():
        run_scr[...] = cnt0_ref[...]

    base = run_scr[...]
    rk = jnp.zeros((m, TOPK_LANES), jnp.int32)
    for r in range(TOP_K):
        before = _dot(ltri_ref[...], onehots[r].astype(BF16)) + base
        rank = jnp.sum(onehots[r] * before, axis=-1, keepdims=True)
        rk = jnp.where(lane_k == r, rank.astype(jnp.int32), rk)
        base = base + jnp.sum(onehots[r], axis=0, keepdims=True)
    run_scr[...] = base
    rk_ref[...] = rk
    cnt_ref[...] = jnp.broadcast_to(base, cnt_ref.shape)


def _const_spec(shape):
    nd = len(shape)
    return pl.BlockSpec(shape, lambda b, j, _nd=nd: (0,) * _nd, pipeline_mode=pl.Buffered(1))


def _pool_band(t, kp):
    col = jnp.arange(kp)[None, :]
    end = (kp - t) + jnp.arange(t)[:, None]
    return jnp.stack([((col <= end) & (col > end - w)) for w in POOL_WINDOWS]).astype(BF16)


def _mixer_call(cfg, x, mod, prev_pool, prev_conv, lw, cnt0):
    b, s, d = x.shape
    bt, t = cfg.bt, cfg.t
    m = bt * t
    grid = (b // bt, s // t)
    n_slab = d // LANES

    def tile_spec():
        return pl.BlockSpec((bt, t, d), lambda bi, j: (bi, j, 0))

    def batch_spec(rows, width):
        return pl.BlockSpec((bt, rows, width), lambda bi, j: (bi, 0, 0))

    def rows_spec(width):
        return pl.BlockSpec((m, width), lambda bi, j: (bi * (s // t) + j, 0))

    consts = [lw["norm_mix_g"], lw["norm_ffn_g"], lw["w_in"], lw["ln_v_g"], lw["ln_v_b"], lw["w_sp"], lw["bs_full"],
              lw["w_a_out"], lw["w_pool"], lw["b_pool"], lw["pool_scale"], lw["w_dw"], lw["b_dw"], lw["ln_c_g"],
              lw["ln_c_b"], lw["w_c_out"], lw["w_out"], lw["wr_hi"], lw["wr_lo"], lw["b_router"], lw["pmat"],
              lw["ltri"], cnt0]
    in_specs = [tile_spec(), batch_spec(1, 6 * d), batch_spec(POOL_HIST, d), batch_spec(CONV_HIST, d)]
    in_specs += [_const_spec(c.shape) for c in consts]
    out_shape = [jax.ShapeDtypeStruct((b, s, d), F32), jax.ShapeDtypeStruct((b, s, d), F32),
                 jax.ShapeDtypeStruct((b * s, TOPK_LANES), jnp.int32), jax.ShapeDtypeStruct((b * s, TOPK_LANES), F32),
                 jax.ShapeDtypeStruct((b * s, TOPK_LANES), jnp.int32),
                 jax.ShapeDtypeStruct((SUBLANES, ROUTER_LANES), F32),
                 jax.ShapeDtypeStruct((b, POOL_STATE, d), F32), jax.ShapeDtypeStruct((b, CONV_STATE, d), F32)]
    out_specs = [tile_spec(), tile_spec(), rows_spec(TOPK_LANES), rows_spec(TOPK_LANES), rows_spec(TOPK_LANES),
                 pl.BlockSpec((SUBLANES, ROUTER_LANES), lambda bi, j: (0, 0)),
                 batch_spec(POOL_STATE, d), batch_spec(CONV_STATE, d)]
    if cfg.emit_v:
        out_shape.append(jax.ShapeDtypeStruct((b, s, d), F32))
        out_specs.append(tile_spec())
    return pl.pallas_call(
        functools.partial(_mixer_kernel, cfg),
        grid=grid,
        in_specs=in_specs,
        out_specs=out_specs,
        out_shape=out_shape,
        scratch_shapes=[pltpu.VMEM((bt, POOL_HIST, d), F32),
                        pltpu.VMEM((bt, n_slab, CONV_HIST + t, LANES), F32),
                        pltpu.VMEM((1, ROUTER_LANES), F32)],
        compiler_params=pltpu.CompilerParams(dimension_semantics=("arbitrary", "arbitrary"),
                                             vmem_limit_bytes=V7X_VMEM_LIMIT_BYTES),
        name=f"mixer_t{t}",
    )(x, mod, prev_pool, prev_conv, *consts)


def _ada_kernel(c_ref, w_ref, b_ref, o_ref):
    o_ref[0] = _dot(jax.nn.silu(c_ref[...]).astype(BF16), w_ref[0].astype(BF16)) + b_ref[0]


def _ada_call(c_all, w_ada, b_ada):
    n_layers, d, six_d = w_ada.shape
    rows = c_all.shape[0]
    bn = six_d // 6
    return pl.pallas_call(
        _ada_kernel,
        grid=(n_layers, six_d // bn),
        in_specs=[pl.BlockSpec((rows, d), lambda l, n: (0, 0)),
                  pl.BlockSpec((1, d, bn), lambda l, n: (l, 0, n)),
                  pl.BlockSpec((1, 1, bn), lambda l, n: (l, 0, n))],
        out_specs=pl.BlockSpec((1, rows, bn), lambda l, n: (l, 0, n)),
        out_shape=jax.ShapeDtypeStruct((n_layers, rows, six_d), F32),
        compiler_params=pltpu.CompilerParams(dimension_semantics=("arbitrary", "arbitrary")),
        name="adaln",
    )(c_all, w_ada, b_ada.reshape(n_layers, 1, six_d))


def _moe_kernel(be_ref, nused_ref, x_ref, wgu_ref, bgu_ref, wdn_ref, bdn_ref, y_ref, wgu_bf, wdn_bf):
    i = pl.program_id(0)
    nused = nused_ref[0]
    dff = wdn_bf.shape[0]

    @pl.when(i < nused)
    def _body():
        changed = jnp.logical_or(i == 0, be_ref[i] != be_ref[jnp.maximum(i - 1, 0)])

        @pl.when(changed)
        def _cast_weights():
            wgu_bf[...] = wgu_ref[0].astype(BF16)
            wdn_bf[...] = wdn_ref[0].astype(BF16)

        gu = _dot(x_ref[...].astype(BF16), wgu_bf[...]) + bgu_ref[0]
        g = jnp.minimum(gu[:, :dff], SWIGLU_LIMIT)
        u = jnp.clip(gu[:, dff:], -SWIGLU_LIMIT, SWIGLU_LIMIT)
        act = (u + 1.0) * g * jax.nn.sigmoid(SWIGLU_ALPHA * g)
        y_ref[...] = _dot(act.astype(BF16), wdn_bf[...]) + bdn_ref[0]

    @pl.when(i >= nused)
    def _unused_block():
        y_ref[...] = jnp.zeros_like(y_ref)


def _moe_call(xs, block_e, nused, layer, w_gu_all, b_gu_all, w_dn_all, b_dn_all):
    n_blocks = block_e.shape[0]
    n_layers, n_exp, d, two_f = w_gu_all.shape
    dff = two_f // 2
    tm = MOE_TM
    e0 = layer * n_exp
    w_gu = w_gu_all.reshape(n_layers * n_exp, d, two_f)
    b_gu = b_gu_all.reshape(n_layers * n_exp, 1, two_f)
    w_dn = w_dn_all.reshape(n_layers * n_exp, dff, d)
    b_dn = b_dn_all.reshape(n_layers * n_exp, 1, d)
    grid_spec = pltpu.PrefetchScalarGridSpec(
        num_scalar_prefetch=2,
        grid=(n_blocks,),
        in_specs=[
            pl.BlockSpec((tm, d), lambda i, be, nu: (i, 0)),
            pl.BlockSpec((1, d, two_f), lambda i, be, nu: (e0 + be[i], 0, 0)),
            pl.BlockSpec((1, 1, two_f), lambda i, be, nu: (e0 + be[i], 0, 0)),
            pl.BlockSpec((1, dff, d), lambda i, be, nu: (e0 + be[i], 0, 0)),
            pl.BlockSpec((1, 1, d), lambda i, be, nu: (e0 + be[i], 0, 0)),
        ],
        out_specs=pl.BlockSpec((tm, d), lambda i, be, nu: (i, 0)),
        scratch_shapes=[pltpu.VMEM((d, two_f), BF16), pltpu.VMEM((dff, d), BF16)],
    )
    return pl.pallas_call(
        _moe_kernel,
        grid_spec=grid_spec,
        out_shape=jax.ShapeDtypeStruct((n_blocks * tm, d), F32),
        compiler_params=pltpu.CompilerParams(dimension_semantics=("arbitrary",),
                                             vmem_limit_bytes=V7X_VMEM_LIMIT_BYTES),
        name="moe_experts",
    )(block_e, nused, xs, w_gu, b_gu, w_dn, b_dn)


def _route(top_i, rank, counts):
    tm = MOE_TM
    n_all = top_i.shape[0]
    n_blocks = n_all * TOP_K // tm + N_EXPERTS
    padded = (counts + tm - 1) // tm * tm
    pad_ends = jnp.cumsum(padded)
    pad_starts = pad_ends - padded
    blk = jnp.arange(n_blocks, dtype=jnp.int32)[:, None] * tm
    block_e = jnp.minimum(jnp.sum((pad_ends[None, :] <= blk).astype(jnp.int32), axis=1), N_EXPERTS - 1)
    nused = (pad_ends[-1] // tm).astype(jnp.int32).reshape(1)
    onehot = top_i[:, :, None] == jnp.arange(N_EXPERTS, dtype=jnp.int32)[None, None, :]
    slot_of = jnp.sum(jnp.where(onehot, pad_starts[None, None, :], 0), axis=-1) + rank
    return slot_of.T.astype(jnp.int32), block_e.astype(jnp.int32), nused


SC_CORES = 2
SC_SUBCORES = 16
SC_WORKERS = SC_CORES * SC_SUBCORES
SC_CHUNK = 32


def _sc_worker_id():
    return lax.axis_index("s") * SC_CORES + lax.axis_index("c")


def _sc_dispatch(hp, hs, slot_p, slot_s, n_slots):
    (n_p, d), n_s = hp.shape, hs.shape[0]
    tp, ts = n_p // SC_WORKERS, n_s // SC_WORKERS
    cp, cs = min(SC_CHUNK, tp), min(SC_CHUNK, ts)
    ncp, ncs = tp // cp, ts // cs
    assert tp * SC_WORKERS == n_p and ts * SC_WORKERS == n_s and ncp * cp == tp and ncs * cs == ts
    assert ncp % 2 == 0

    def body(hp_hbm, hs_hbm, idxp_hbm, idxs_hbm, out_hbm, idxp_v, idxs_v, buf, sbuf, rsem, wsem):
        wid = _sc_worker_id()
        pltpu.sync_copy(idxp_hbm.at[wid], idxp_v)
        pltpu.sync_copy(idxs_hbm.at[wid], idxs_v)

        def read(c, b):
            return pltpu.make_async_copy(hp_hbm.at[pl.ds(wid * tp + c * cp, cp)], buf.at[b], rsem.at[b])

        def write(c, k, b):
            return pltpu.make_async_copy(buf.at[b], out_hbm.at[idxp_v.at[c * TOP_K + k]], wsem.at[b])

        read(0, 0).start()

        @pl.loop(0, ncp, step=2)
        def _(c0):
            for b in range(2):
                c = c0 + b

                @pl.when(c + 1 < ncp)
                def _():
                    @pl.when(c >= 1)
                    def _():
                        for k in range(TOP_K):
                            write(c - 1, k, 1 - b).wait()
                    read(c + 1, 1 - b).start()

                read(c, b).wait()
                for k in range(TOP_K):
                    write(c, k, b).start()

        for c in (ncp - 2, ncp - 1):
            for k in range(TOP_K):
                write(c, k, c % 2).wait()

        for c in range(ncs):
            pltpu.sync_copy(hs_hbm.at[pl.ds(wid * ts + c * cs, cs)], sbuf)
            for k in range(TOP_K):
                pltpu.sync_copy(sbuf, out_hbm.at[idxs_v.at[c * TOP_K + k]])

    call = pl.kernel(
        body,
        out_type=jax.ShapeDtypeStruct((n_slots, d), F32),
        mesh=plsc.VectorSubcoreMesh(core_axis_name="c", subcore_axis_name="s"),
        scratch_types=[pltpu.VMEM((ncp * TOP_K, cp), jnp.int32), pltpu.VMEM((ncs * TOP_K, cs), jnp.int32),
                       pltpu.VMEM((2, cp, d), F32), pltpu.VMEM((cs, d), F32),
                       pltpu.SemaphoreType.DMA((2,)), pltpu.SemaphoreType.DMA((2,))],
        name="sc_dispatch",
    )
    idx_p = slot_p.reshape(TOP_K, SC_WORKERS, ncp, cp).transpose(1, 2, 0, 3).reshape(SC_WORKERS, ncp * TOP_K, cp)
    idx_s = slot_s.reshape(TOP_K, SC_WORKERS, ncs, cs).transpose(1, 2, 0, 3).reshape(SC_WORKERS, ncs * TOP_K, cs)
    return call(hp, hs, idx_p, idx_s)


def _sc_collect(ys, slot_p, slot_s):
    d = ys.shape[1]
    rows_p, rows_s = slot_p.size, slot_s.size
    per_p, per_s = rows_p // SC_WORKERS, rows_s // SC_WORKERS
    ch = SC_CHUNK
    assert per_p * SC_WORKERS == rows_p and per_s * SC_WORKERS == rows_s
    assert per_p % (2 * ch) == 0 and per_s % (2 * ch) == 0

    def body(ys_hbm, idxp_hbm, idxs_hbm, outp_hbm, outs_hbm, idxp_v, idxs_v, rows_v, gsem, wsem):
        wid = _sc_worker_id()

        def segment(idx_hbm, idx_v, out_hbm, per_w):
            n_chunks = per_w // ch
            base = wid * per_w
            pltpu.sync_copy(idx_hbm.at[pl.ds(base, per_w)], idx_v)

            def gather(g, b):
                return pltpu.make_async_copy(ys_hbm.at[idx_v.at[pl.ds(g * ch, ch)]], rows_v.at[b], gsem.at[b])

            def put(g, b):
                return pltpu.make_async_copy(rows_v.at[b], out_hbm.at[pl.ds(base + g * ch, ch)], wsem.at[b])

            gather(0, 0).start()

            @pl.loop(0, n_chunks, step=2)
            def _(g0):
                for b in range(2):
                    g = g0 + b

                    @pl.when(g + 1 < n_chunks)
                    def _():
                        @pl.when(g >= 1)
                        def _():
                            put(g - 1, 1 - b).wait()
                        gather(g + 1, 1 - b).start()

                    gather(g, b).wait()
                    put(g, b).start()

            for g in (n_chunks - 2, n_chunks - 1):
                put(g, g % 2).wait()

        segment(idxp_hbm, idxp_v, outp_hbm, per_p)
        segment(idxs_hbm, idxs_v, outs_hbm, per_s)

    call = pl.kernel(
        body,
        out_type=(jax.ShapeDtypeStruct((rows_p, d), F32), jax.ShapeDtypeStruct((rows_s, d), F32)),
        mesh=plsc.VectorSubcoreMesh(core_axis_name="c", subcore_axis_name="s"),
        scratch_types=[pltpu.VMEM((per_p,), jnp.int32), pltpu.VMEM((per_s,), jnp.int32),
                       pltpu.VMEM((2, ch, d), F32),
                       pltpu.SemaphoreType.DMA((2,)), pltpu.SemaphoreType.DMA((2,))],
        name="sc_collect",
    )
    return call(ys, slot_p.reshape(rows_p), slot_s.reshape(rows_s))


def _combine_kernel(final, x_ref, mod_ref, y_ref, tg_ref, fg_ref, o_ref):
    bt, t, d = x_ref.shape
    m = bt * t
    g2 = mod_ref[:, :, 5 * d:6 * d]
    g2 = g2.reshape(1, d) if bt == 1 else jnp.broadcast_to(g2, (bt, t, d)).reshape(m, d)
    tg = tg_ref[...]
    f = tg[:, 0:1] * y_ref[0]
    for k in range(1, TOP_K):
        f = f + tg[:, k:k + 1] * y_ref[k]
    out = x_ref[...].reshape(m, d) + g2 * f
    if final:
        out = _rms(out, fg_ref[...])
    o_ref[...] = out.reshape(bt, t, d)


def _combine_call(x_new, mod, y4, tg, final_g, bt, t, final):
    b, s, d = x_new.shape
    m = bt * t
    steps = s // t
    return pl.pallas_call(
        functools.partial(_combine_kernel, final),
        grid=(b // bt, steps),
        in_specs=[pl.BlockSpec((bt, t, d), lambda bi, j: (bi, j, 0)),
                  pl.BlockSpec((bt, 1, 6 * d), lambda bi, j: (bi, 0, 0)),
                  pl.BlockSpec((TOP_K, m, d), lambda bi, j: (0, bi * steps + j, 0)),
                  pl.BlockSpec((m, TOPK_LANES), lambda bi, j: (bi * steps + j, 0)),
                  pl.BlockSpec((1, d), lambda bi, j: (0, 0))],
        out_specs=pl.BlockSpec((bt, t, d), lambda bi, j: (bi, j, 0)),
        out_shape=jax.ShapeDtypeStruct((b, s, d), F32),
        compiler_params=pltpu.CompilerParams(dimension_semantics=("arbitrary", "arbitrary")),
        name=f"combine_t{t}",
    )(x_new, mod, y4, tg, final_g)


def _layer_weights(l, p, cfgs):
    d = p["w_out"].shape[-1]
    row = lambda v: v[l].reshape(1, d)
    wr = p["w_router"][l]
    wr_pad = jnp.zeros((d, ROUTER_LANES), F32).at[:, :N_EXPERTS].set(wr)
    wr_hi = wr_pad.astype(BF16)
    wr_lo = (wr_pad - wr_hi.astype(F32)).astype(BF16)
    b_router = jnp.full((1, ROUTER_LANES), -1e30, F32).at[0, :N_EXPERTS].set(p["b_router"][l])
    mask = jnp.tril(jnp.ones((GMLP_CHUNK, GMLP_CHUNK), dtype=bool))
    w_sp_full = jnp.where(mask[None], p["w_spatial"][l], 0)
    hd = d // GMLP_HEADS
    base = dict(norm_mix_g=row(p["norm_mix_g"]), norm_ffn_g=row(p["norm_ffn_g"]), w_in=p["w_in"][l].astype(BF16),
                ln_v_g=row(p["ln_v_g"]), ln_v_b=row(p["ln_v_b"]), w_a_out=p["w_a_out"][l].astype(BF16),
                w_pool=p["w_pool"][l].astype(BF16), b_pool=row(p["b_pool"]), pool_scale=row(p["pool_scale"]),
                w_dw=jnp.zeros((_round_up(CONV_K, SUBLANES), d), F32).at[:CONV_K].set(p["w_dw"][l]),
                b_dw=row(p["b_dw"]), ln_c_g=row(p["ln_c_g"]), ln_c_b=row(p["ln_c_b"]),
                w_c_out=p["w_c_out"][l].astype(BF16), w_out=p["w_out"][l].astype(BF16),
                wr_hi=wr_hi, wr_lo=wr_lo, b_router=b_router)
    out = []
    for cfg in cfgs:
        w_sp = jnp.zeros((GMLP_HEADS, cfg.ch, cfg.kg), F32).at[:, :, :cfg.ch].set(w_sp_full[:, :cfg.ch, :cfg.ch])
        bs_full = jnp.repeat(p["b_spatial"][l][:, :cfg.ch].T, hd, axis=1)
        m = cfg.bt * cfg.t
        ltri = (jnp.arange(m)[:, None] > jnp.arange(m)[None, :]).astype(BF16)
        out.append(dict(base, w_sp=w_sp.astype(BF16), bs_full=bs_full, pmat=_pool_band(cfg.t, cfg.kp), ltri=ltri))
    return out


def _pad_front(state, rows):
    pad = rows - state.shape[-2]
    return jnp.pad(state, ((0, 0),) * (state.ndim - 2) + ((pad, 0), (0, 0)))


def kernel(x_prompt, x_sample, c_prompt, c_sample, state_pool, state_conv, norm_mix_g, norm_ffn_g, w_ada, b_ada, w_in, ln_v_g, ln_v_b, w_spatial, b_spatial, w_a_out, w_pool, b_pool, pool_scale, w_dw, b_dw, ln_c_g, ln_c_b, w_c_out, w_out, w_router, b_router, w_gate_up, b_gate_up, w_down, b_down, final_norm_g):
    p = dict(norm_mix_g=norm_mix_g, norm_ffn_g=norm_ffn_g, w_in=w_in, ln_v_g=ln_v_g, ln_v_b=ln_v_b,
             w_spatial=w_spatial, b_spatial=b_spatial, w_a_out=w_a_out, w_pool=w_pool, b_pool=b_pool,
             pool_scale=pool_scale, w_dw=w_dw, b_dw=b_dw, ln_c_g=ln_c_g, ln_c_b=ln_c_b, w_c_out=w_c_out,
             w_out=w_out, w_router=w_router, b_router=b_router)
    n_layers = w_in.shape[0]
    bp, sp, d = x_prompt.shape
    bs, ss, _ = x_sample.shape
    tp = min(PROMPT_TILE, sp)
    bts = min(SAMPLE_BT, bs)
    cfg_p = MixerCfg(bt=1, t=tp, ch=GMLP_CHUNK, kg=GMLP_CHUNK, kp=_round_up(POOL_HIST + tp, LANES), d=d,
                     start_pos=0, emit_v=False)
    cfg_s = MixerCfg(bt=bts, t=ss, ch=ss, kg=_round_up(ss, LANES), kp=_round_up(POOL_HIST + ss, LANES), d=d,
                     start_pos=PAST_LEN, emit_v=True)
    n_p, n_s = bp * sp, bs * ss
    n_slots = (n_p + n_s) * TOP_K + N_EXPERTS * MOE_TM

    mod_all = _ada_call(jnp.concatenate([c_prompt, c_sample], axis=0), w_ada, b_ada)
    mod_p = mod_all[:, :bp].reshape(n_layers, bp, 1, 6 * d)
    mod_s = mod_all[:, bp:].reshape(n_layers, bs, 1, 6 * d)
    final_g = final_norm_g.reshape(1, d)

    xp, xs = x_prompt, x_sample
    zero_pool = jnp.zeros((bp, POOL_HIST, d), F32)
    zero_conv = jnp.zeros((bp, CONV_HIST, d), F32)
    zero_cnt = jnp.zeros((1, ROUTER_LANES), F32)
    pools_p, convs_p, pools_s, convs_s, vs = [], [], [], [], []
    for l in range(n_layers):
        lw_p, lw_s = _layer_weights(l, p, (cfg_p, cfg_s))
        xp_new, h2p, tip, tgp, rkp, cnt_p, npool_p, nconv_p = _mixer_call(
            cfg_p, xp, mod_p[l], zero_pool, zero_conv, lw_p, zero_cnt)
        xs_new, h2s, tis, tgs, rks, cnt_all, npool_s, nconv_s, v_s = _mixer_call(
            cfg_s, xs, mod_s[l], _pad_front(state_pool[l], POOL_HIST), _pad_front(state_conv[l], CONV_HIST),
            lw_s, cnt_p[0:1])
        top_i = jnp.concatenate([tip, tis], axis=0)[:, :TOP_K]
        rank = jnp.concatenate([rkp, rks], axis=0)[:, :TOP_K]
        counts = cnt_all[0, :N_EXPERTS].astype(jnp.int32)
        slot_of, block_e, nused = _route(top_i, rank, counts)
        slot_p, slot_s = slot_of[:, :n_p], slot_of[:, n_p:]
        xs_sorted = _sc_dispatch(h2p.reshape(n_p, d), h2s.reshape(n_s, d), slot_p, slot_s, n_slots)
        ys = _moe_call(xs_sorted, block_e, nused, l, w_gate_up, b_gate_up, w_down, b_down)
        y4p, y4s = _sc_collect(ys, slot_p, slot_s)
        final = l == n_layers - 1
        xp = _combine_call(xp_new, mod_p[l], y4p.reshape(TOP_K, n_p, d), tgp, final_g, 1, tp, final)
        xs = _combine_call(xs_new, mod_s[l], y4s.reshape(TOP_K, n_s, d), tgs, final_g, bts, ss, final)
        pools_p.append(npool_p)
        convs_p.append(nconv_p)
        pools_s.append(npool_s)
        convs_s.append(nconv_s)
        vs.append(v_s)
    return (xp, xs, jnp.stack(pools_p), jnp.stack(convs_p), jnp.stack(pools_s), jnp.stack(convs_s), jnp.stack(vs))
```

```python
import functools
from typing import NamedTuple

import jax
import jax.numpy as jnp
from jax import lax
from jax.experimental import pallas as pl
from jax.experimental.pallas import tpu as pltpu
from jax.experimental.pallas import tpu_sc as plsc

GMLP_CHUNK = 128
GMLP_HEADS = 4
POOL_WINDOWS = (2, 4, 8, 16)
POOL_STATE = max(POOL_WINDOWS) - 1
CONV_K = 31
CONV_STATE = CONV_K - 1
N_BRANCH = 3
N_EXPERTS = 32
TOP_K = 4
SWIGLU_LIMIT = 7.0
SWIGLU_ALPHA = 1.702
EPS = 1e-6
PAST_LEN = 2048

LANES = 128
SUBLANES = 8
V7X_VMEM_LIMIT_BYTES = 56 * 2**20

POOL_HIST = 16
CONV_HIST = 32
ROUTER_LANES = LANES
TOPK_LANES = 8
MOE_TM = 512
PROMPT_TILE = 256
SAMPLE_BT = 8

F32 = jnp.float32
BF16 = jnp.bfloat16


def _round_up(a, m):
    return (a + m - 1) // m * m


class MixerCfg(NamedTuple):
    bt: int
    t: int
    ch: int
    kg: int
    kp: int
    d: int
    start_pos: int
    emit_v: bool


def _rms(x, g):
    return x * lax.rsqrt(jnp.mean(x * x, axis=-1, keepdims=True) + EPS) * g


def _ln(x, g, b):
    mu = jnp.mean(x, axis=-1, keepdims=True)
    xc = x - mu
    var = jnp.mean(xc * xc, axis=-1, keepdims=True)
    return xc * lax.rsqrt(var + EPS) * g + b


def _dot(a, b):
    return jnp.dot(a, b, preferred_element_type=F32)


def _mixer_kernel(cfg, x_ref, mod_ref, pp_ref, pc_ref, nmg_ref, nfg_ref, win_ref, lnvg_ref, lnvb_ref,
                  wsp_ref, bsf_ref, wao_ref, wpool_ref, bpool_ref, pscale_ref, wdw_ref, bdw_ref,
                  lncg_ref, lncb_ref, wco_ref, wout_ref, wrh_ref, wrl_ref, br_ref, pmat_ref, ltri_ref, cnt0_ref,
                  after_ref, xo_ref, h2_ref, ti_ref, tg_ref, rk_ref, cnt_ref, npool_ref, nconv_ref, *rest):
    if cfg.emit_v:
        v_ref, hp_scr, xc_scr, run_scr = rest
    else:
        hp_scr, xc_scr, run_scr = rest
    bt, t, d = cfg.bt, cfg.t, cfg.d
    m = bt * t
    n_slab = d // LANES
    gd = d // len(POOL_WINDOWS)
    hd = d // GMLP_HEADS
    j = pl.program_id(1)

    def mod_rows(k):
        r = mod_ref[:, :, k * d:(k + 1) * d]
        if bt == 1:
            return r.reshape(1, d)
        return jnp.broadcast_to(r, (bt, t, d)).reshape(m, d)

    @pl.when(j == 0)
    def _load_state():
        hp_scr[...] = pp_ref[...]
        for bi in range(bt):
            for c in range(n_slab):
                xc_scr[bi, c, 0:CONV_HIST, :] = pc_ref[bi, :, c * LANES:(c + 1) * LANES]

    x = x_ref[...].reshape(m, d)
    sh1, sc1, g1 = mod_rows(0), mod_rows(1), mod_rows(2)
    h = (_rms(x, nmg_ref[...]) * (1.0 + sc1) + sh1).astype(BF16)

    def zcols(lo, hi):
        return _dot(h, win_ref[:, lo:hi])

    off_u, off_v, off_b, off_c, off_g = 0, d, 2 * d, 3 * d, 5 * d

    zb = zcols(off_b, off_b + d)
    lane = lax.broadcasted_iota(jnp.int32, (1, d), 1)
    win_lane = jnp.left_shift(2, lane // gd).astype(F32)
    pos1 = (cfg.start_pos + 1 + j * t + lax.broadcasted_iota(jnp.int32, (t, 1), 0)).astype(F32)
    cnt = jnp.minimum(pos1, win_lane)
    pooled_parts = []
    for bi in range(bt):
        zb_b = zb[bi * t:(bi + 1) * t]
        pieces = [hp_scr[bi], zb_b]
        if cfg.kp > POOL_HIST + t:
            pieces = [jnp.zeros((cfg.kp - POOL_HIST - t, d), F32)] + pieces
        full = jnp.concatenate(pieces, axis=0)
        fullb = full.astype(BF16)
        sums = jnp.concatenate(
            [_dot(pmat_ref[g], fullb[:, g * gd:(g + 1) * gd]) for g in range(len(POOL_WINDOWS))], axis=1)
        pooled_parts.append(sums / cnt - zb_b)
        hp_scr[bi] = full[cfg.kp - POOL_HIST:]
        npool_ref[bi] = hp_scr[bi, pl.ds(POOL_HIST - POOL_STATE, POOL_STATE), :]
    pooled = (pooled_parts[0] if bt == 1 else jnp.concatenate(pooled_parts, axis=0)).astype(BF16)
    y_b = jnp.concatenate(
        [_dot(pooled[:, g * gd:(g + 1) * gd], wpool_ref[g]) for g in range(len(POOL_WINDOWS))], axis=1)
    y_b = (y_b + bpool_ref[...]) * pscale_ref[...]
    merged = jax.nn.sigmoid(zcols(off_g + d, off_g + 2 * d)) * y_b

    u = jax.nn.gelu(zcols(off_u, off_u + d))
    v = _ln(jax.nn.gelu(zcols(off_v, off_v + d)), lnvg_ref[...], lnvb_ref[...])
    if cfg.emit_v:
        v_ref[...] = v.reshape(bt, t, d)
    vb = v.astype(BF16)
    mixed_rows = []
    for r0 in range(0, m, cfg.ch):
        vc = vb[r0:r0 + cfg.ch]
        if cfg.kg > cfg.ch:
            vc = jnp.concatenate([vc, jnp.zeros((cfg.kg - cfg.ch, d), BF16)], axis=0)
        mixed_rows.append(jnp.concatenate(
            [_dot(wsp_ref[hh], vc[:, hh * hd:(hh + 1) * hd]) for hh in range(GMLP_HEADS)], axis=1) + bsf_ref[...])
    mixed = mixed_rows[0] if len(mixed_rows) == 1 else jnp.concatenate(mixed_rows, axis=0)
    y_a = _dot((u * mixed).astype(BF16), wao_ref[...])
    merged = merged + jax.nn.sigmoid(zcols(off_g, off_g + d)) * y_a

    x_c = zcols(off_c, off_c + d) * jax.nn.sigmoid(zcols(off_c + d, off_c + 2 * d))
    conv_parts = []
    for bi in range(bt):
        accs = []
        for c in range(n_slab):
            cs = slice(c * LANES, (c + 1) * LANES)
            xc_scr[bi, c, CONV_HIST:CONV_HIST + t, :] = x_c[bi * t:(bi + 1) * t, cs]
            acc = jnp.broadcast_to(bdw_ref[:, cs], (t, LANES))
            for k in range(CONV_K):
                acc = acc + wdw_ref[k:k + 1, cs] * xc_scr[bi, c, pl.ds(CONV_HIST - CONV_STATE + k, t), :]
            accs.append(acc)
            tail = xc_scr[bi, c, t:t + CONV_HIST, :]
            nconv_ref[bi, :, cs] = xc_scr[bi, c, pl.ds(t + CONV_HIST - CONV_STATE, CONV_STATE), :]
            xc_scr[bi, c, 0:CONV_HIST, :] = tail
        conv_parts.append(jnp.concatenate(accs, axis=1))
    conv = conv_parts[0] if bt == 1 else jnp.concatenate(conv_parts, axis=0)
    y_c = _dot(jax.nn.silu(_ln(conv, lncg_ref[...], lncb_ref[...])).astype(BF16), wco_ref[...])
    merged = merged + jax.nn.sigmoid(zcols(off_g + 2 * d, off_g + 3 * d)) * y_c

    x_new = x + g1 * _dot(merged.astype(BF16), wout_ref[...])
    xo_ref[...] = x_new.reshape(bt, t, d)
    sh2, sc2 = mod_rows(3), mod_rows(4)
    h2 = _rms(x_new, nfg_ref[...]) * (1.0 + sc2) + sh2
    h2_ref[...] = h2.reshape(bt, t, d)
    h2_hi = h2.astype(BF16)
    h2_lo = (h2 - h2_hi.astype(F32)).astype(BF16)
    logits = _dot(h2_hi, wrh_ref[...]) + _dot(h2_lo, wrh_ref[...]) + _dot(h2_hi, wrl_ref[...]) + br_ref[...]
    lane_r = lax.broadcasted_iota(jnp.int32, (m, ROUTER_LANES), 1).astype(F32)
    lane_k = lax.broadcasted_iota(jnp.int32, (m, TOPK_LANES), 1)
    work = logits
    top_vals, onehots = [], []
    ti = jnp.zeros((m, TOPK_LANES), jnp.int32)
    for r in range(TOP_K):
        mx = jnp.max(work, axis=-1, keepdims=True)
        idx = jnp.min(jnp.where(work == mx, lane_r, float(ROUTER_LANES)), axis=-1, keepdims=True)
        ti = jnp.where(lane_k == r, idx.astype(jnp.int32), ti)
        top_vals.append(mx)
        work = jnp.where(lane_r == idx, -jnp.inf, work)
        onehots.append((lane_r == idx).astype(F32))
    exps = [jnp.exp(tv - top_vals[0]) for tv in top_vals]
    denom = exps[0] + exps[1] + exps[2] + exps[3]
    tg = jnp.zeros((m, TOPK_LANES), F32)
    for r in range(TOP_K):
        tg = jnp.where(lane_k == r, exps[r] / denom, tg)
    ti_ref[...] = ti
    tg_ref[...] = tg

    @pl.when(jnp.logical_and(pl.program_id(0) == 0, j == 0))
    def _init_counts():
        run_scr[...] = cnt0_ref[...]

    base = run_scr[...]
    rk = jnp.zeros((m, TOPK_LANES), jnp.int32)
    for r in range(TOP_K):
        before = _dot(ltri_ref[...], onehots[r].astype(BF16)) + base
        rank = jnp.sum(onehots[r] * before, axis=-1, keepdims=True)
        rk = jnp.where(lane_k == r, rank.astype(jnp.int32), rk)
        base = base + jnp.sum(onehots[r], axis=0, keepdims=True)
    run_scr[...] = base
    rk_ref[...] = rk
    cnt_ref[...] = jnp.broadcast_to(base, cnt_ref.shape)


def _const_spec(shape):
    nd = len(shape)
    return pl.BlockSpec(shape, lambda b, j, _nd=nd: (0,) * _nd, pipeline_mode=pl.Buffered(1))


def _pool_band(t, kp):
    col = jnp.arange(kp)[None, :]
    end = (kp - t) + jnp.arange(t)[:, None]
    return jnp.stack([((col <= end) & (col > end - w)) for w in POOL_WINDOWS]).astype(BF16)


def _mixer_call(cfg, x, mod, prev_pool, prev_conv, lw, cnt0, after):
    b, s, d = x.shape
    bt, t = cfg.bt, cfg.t
    m = bt * t
    grid = (b // bt, s // t)
    n_slab = d // LANES

    def tile_spec():
        return pl.BlockSpec((bt, t, d), lambda bi, j: (bi, j, 0))

    def batch_spec(rows, width):
        return pl.BlockSpec((bt, rows, width), lambda bi, j: (bi, 0, 0))

    def rows_spec(width):
        return pl.BlockSpec((m, width), lambda bi, j: (bi * (s // t) + j, 0))

    consts = [lw["norm_mix_g"], lw["norm_ffn_g"], lw["w_in"], lw["ln_v_g"], lw["ln_v_b"], lw["w_sp"], lw["bs_full"],
              lw["w_a_out"], lw["w_pool"], lw["b_pool"], lw["pool_scale"], lw["w_dw"], lw["b_dw"], lw["ln_c_g"],
              lw["ln_c_b"], lw["w_c_out"], lw["w_out"], lw["wr_hi"], lw["wr_lo"], lw["b_router"], lw["pmat"],
              lw["ltri"], cnt0]
    in_specs = [tile_spec(), batch_spec(1, 6 * d), batch_spec(POOL_HIST, d), batch_spec(CONV_HIST, d)]
    in_specs += [_const_spec(c.shape) for c in consts]
    in_specs.append(pl.BlockSpec(memory_space=pl.ANY))
    out_shape = [jax.ShapeDtypeStruct((b, s, d), F32), jax.ShapeDtypeStruct((b, s, d), F32),
                 jax.ShapeDtypeStruct((b * s, TOPK_LANES), jnp.int32), jax.ShapeDtypeStruct((b * s, TOPK_LANES), F32),
                 jax.ShapeDtypeStruct((b * s, TOPK_LANES), jnp.int32),
                 jax.ShapeDtypeStruct((SUBLANES, ROUTER_LANES), F32),
                 jax.ShapeDtypeStruct((b, POOL_STATE, d), F32), jax.ShapeDtypeStruct((b, CONV_STATE, d), F32)]
    out_specs = [tile_spec(), tile_spec(), rows_spec(TOPK_LANES), rows_spec(TOPK_LANES), rows_spec(TOPK_LANES),
                 pl.BlockSpec((SUBLANES, ROUTER_LANES), lambda bi, j: (0, 0)),
                 batch_spec(POOL_STATE, d), batch_spec(CONV_STATE, d)]
    if cfg.emit_v:
        out_shape.append(jax.ShapeDtypeStruct((b, s, d), F32))
        out_specs.append(tile_spec())
    return pl.pallas_call(
        functools.partial(_mixer_kernel, cfg),
        grid=grid,
        in_specs=in_specs,
        out_specs=out_specs,
        out_shape=out_shape,
        scratch_shapes=[pltpu.VMEM((bt, POOL_HIST, d), F32),
                        pltpu.VMEM((bt, n_slab, CONV_HIST + t, LANES), F32),
                        pltpu.VMEM((1, ROUTER_LANES), F32)],
        compiler_params=pltpu.CompilerParams(dimension_semantics=("arbitrary", "arbitrary"),
                                             vmem_limit_bytes=V7X_VMEM_LIMIT_BYTES),
        name=f"mixer_t{t}",
    )(x, mod, prev_pool, prev_conv, *consts, after)


def _ada_kernel(c_ref, w_ref, b_ref, o_ref):
    o_ref[0] = _dot(jax.nn.silu(c_ref[...]).astype(BF16), w_ref[0].astype(BF16)) + b_ref[0]


def _ada_call(c_all, w_ada, b_ada):
    n_layers, d, six_d = w_ada.shape
    rows = c_all.shape[0]
    bn = six_d // 6
    return pl.pallas_call(
        _ada_kernel,
        grid=(n_layers, six_d // bn),
        in_specs=[pl.BlockSpec((rows, d), lambda l, n: (0, 0)),
                  pl.BlockSpec((1, d, bn), lambda l, n: (l, 0, n)),
                  pl.BlockSpec((1, 1, bn), lambda l, n: (l, 0, n))],
        out_specs=pl.BlockSpec((1, rows, bn), lambda l, n: (l, 0, n)),
        out_shape=jax.ShapeDtypeStruct((n_layers, rows, six_d), F32),
        compiler_params=pltpu.CompilerParams(dimension_semantics=("arbitrary", "arbitrary")),
        name="adaln",
    )(c_all, w_ada, b_ada.reshape(n_layers, 1, six_d))


def _moe_kernel(be_ref, nused_ref, x_ref, wgu_ref, bgu_ref, wdn_ref, bdn_ref, after_ref, y_ref, wgu_bf, wdn_bf):
    i = pl.program_id(0)
    nused = nused_ref[0]
    dff = wdn_bf.shape[0]

    @pl.when(i < nused)
    def _body():
        changed = jnp.logical_or(i == 0, be_ref[i] != be_ref[jnp.maximum(i - 1, 0)])

        @pl.when(changed)
        def _cast_weights():
            wgu_bf[...] = wgu_ref[0].astype(BF16)
            wdn_bf[...] = wdn_ref[0].astype(BF16)

        gu = _dot(x_ref[...].astype(BF16), wgu_bf[...]) + bgu_ref[0]
        g = jnp.minimum(gu[:, :dff], SWIGLU_LIMIT)
        u = jnp.clip(gu[:, dff:], -SWIGLU_LIMIT, SWIGLU_LIMIT)
        act = (u + 1.0) * g * jax.nn.sigmoid(SWIGLU_ALPHA * g)
        y_ref[...] = _dot(act.astype(BF16), wdn_bf[...]) + bdn_ref[0]

    @pl.when(i >= nused)
    def _unused_block():
        y_ref[...] = jnp.zeros_like(y_ref)


def _moe_call(xs, block_e, nused, layer, w_gu_all, b_gu_all, w_dn_all, b_dn_all, after):
    n_blocks = block_e.shape[0]
    n_layers, n_exp, d, two_f = w_gu_all.shape
    dff = two_f // 2
    tm = MOE_TM
    e0 = layer * n_exp
    w_gu = w_gu_all.reshape(n_layers * n_exp, d, two_f)
    b_gu = b_gu_all.reshape(n_layers * n_exp, 1, two_f)
    w_dn = w_dn_all.reshape(n_layers * n_exp, dff, d)
    b_dn = b_dn_all.reshape(n_layers * n_exp, 1, d)
    grid_spec = pltpu.PrefetchScalarGridSpec(
        num_scalar_prefetch=2,
        grid=(n_blocks,),
        in_specs=[
            pl.BlockSpec((tm, d), lambda i, be, nu: (i, 0)),
            pl.BlockSpec((1, d, two_f), lambda i, be, nu: (e0 + be[i], 0, 0)),
            pl.BlockSpec((1, 1, two_f), lambda i, be, nu: (e0 + be[i], 0, 0)),
            pl.BlockSpec((1, dff, d), lambda i, be, nu: (e0 + be[i], 0, 0)),
            pl.BlockSpec((1, 1, d), lambda i, be, nu: (e0 + be[i], 0, 0)),
            pl.BlockSpec(memory_space=pl.ANY),
        ],
        out_specs=pl.BlockSpec((tm, d), lambda i, be, nu: (i, 0)),
        scratch_shapes=[pltpu.VMEM((d, two_f), BF16), pltpu.VMEM((dff, d), BF16)],
    )
    return pl.pallas_call(
        _moe_kernel,
        grid_spec=grid_spec,
        out_shape=jax.ShapeDtypeStruct((n_blocks * tm, d), F32),
        compiler_params=pltpu.CompilerParams(dimension_semantics=("arbitrary",),
                                             vmem_limit_bytes=V7X_VMEM_LIMIT_BYTES),
        name="moe_experts",
    )(block_e, nused, xs, w_gu, b_gu, w_dn, b_dn, after)


def _route(top_i, rank, counts):
    tm = MOE_TM
    n_all = top_i.shape[0]
    n_blocks = n_all * TOP_K // tm + N_EXPERTS
    padded = (counts + tm - 1) // tm * tm
    pad_ends = jnp.cumsum(padded)
    pad_starts = pad_ends - padded
    blk = jnp.arange(n_blocks, dtype=jnp.int32)[:, None] * tm
    block_e = jnp.minimum(jnp.sum((pad_ends[None, :] <= blk).astype(jnp.int32), axis=1), N_EXPERTS - 1)
    nused = (pad_ends[-1] // tm).astype(jnp.int32).reshape(1)
    onehot = top_i[:, :, None] == jnp.arange(N_EXPERTS, dtype=jnp.int32)[None, None, :]
    slot_of = jnp.sum(jnp.where(onehot, pad_starts[None, None, :], 0), axis=-1) + rank
    return slot_of.T.astype(jnp.int32), block_e.astype(jnp.int32), nused


SC_CORES = 2
SC_SUBCORES = 16
SC_WORKERS = SC_CORES * SC_SUBCORES
SC_CHUNK = 32


def _sc_worker_id():
    return lax.axis_index("s") * SC_CORES + lax.axis_index("c")


def _sc_plan(n_rows):
    per_w = n_rows // SC_WORKERS
    ch = min(SC_CHUNK, per_w)
    n_ch = per_w // ch
    assert per_w * SC_WORKERS == n_rows and n_ch * ch == per_w
    return per_w, ch, n_ch, (n_ch >= 2 and n_ch % 2 == 0)


def _sc_dispatch(tables, slots, n_slots):
    nt = len(tables)
    d = tables[0].shape[1]
    plans = [_sc_plan(h.shape[0]) for h in tables]

    def body(*refs):
        h_refs, idx_refs, out_hbm = refs[:nt], refs[nt:2 * nt], refs[2 * nt]
        scratch = refs[2 * nt + 1:]
        idx_vs, bufs, (rsem, wsem) = scratch[:nt], scratch[nt:2 * nt], scratch[2 * nt:]
        wid = _sc_worker_id()
        for h_hbm, idx_hbm, idx_v, buf, (tpw, ch, n_ch, piped) in zip(h_refs, idx_refs, idx_vs, bufs, plans):
            pltpu.sync_copy(idx_hbm.at[wid], idx_v)

            def read(c, b, h_hbm=h_hbm, buf=buf, tpw=tpw, ch=ch):
                return pltpu.make_async_copy(h_hbm.at[pl.ds(wid * tpw + c * ch, ch)], buf.at[b], rsem.at[b])

            def write(c, k, b, buf=buf, idx_v=idx_v):
                return pltpu.make_async_copy(buf.at[b], out_hbm.at[idx_v.at[c * TOP_K + k]], wsem.at[b])

            if piped:
                read(0, 0).start()

                @pl.loop(0, n_ch, step=2)
                def _(c0, read=read, write=write, n_ch=n_ch):
                    for b in range(2):
                        c = c0 + b

                        @pl.when(c + 1 < n_ch)
                        def _():
                            @pl.when(c >= 1)
                            def _():
                                for k in range(TOP_K):
                                    write(c - 1, k, 1 - b).wait()
                            read(c + 1, 1 - b).start()

                        read(c, b).wait()
                        for k in range(TOP_K):
                            write(c, k, b).start()

                for c in (n_ch - 2, n_ch - 1):
                    for k in range(TOP_K):
                        write(c, k, c % 2).wait()
            else:
                for c in range(n_ch):
                    cp_in = read(c, 0)
                    cp_in.start()
                    cp_in.wait()
                    for k in range(TOP_K):
                        write(c, k, 0).start()
                    for k in range(TOP_K):
                        write(c, k, 0).wait()

    scratch_types = [pltpu.VMEM((n_ch * TOP_K, ch), jnp.int32) for (_, ch, n_ch, _) in plans]
    scratch_types += [pltpu.VMEM((2 if piped else 1, ch, d), F32) for (_, ch, _, piped) in plans]
    scratch_types += [pltpu.SemaphoreType.DMA((2,)), pltpu.SemaphoreType.DMA((2,))]
    call = pl.kernel(
        body,
        out_type=jax.ShapeDtypeStruct((n_slots, d), F32),
        mesh=plsc.VectorSubcoreMesh(core_axis_name="c", subcore_axis_name="s"),
        scratch_types=scratch_types,
        name="sc_dispatch",
    )
    idx = [s.reshape(TOP_K, SC_WORKERS, n_ch, ch).transpose(1, 2, 0, 3).reshape(SC_WORKERS, n_ch * TOP_K, ch)
           for s, (_, ch, n_ch, _) in zip(slots, plans)]
    return call(*tables, *idx)


def _sc_collect(ys, slots):
    nt = len(slots)
    d = ys.shape[1]
    plans = [_sc_plan(s.size) for s in slots]
    assert all(piped for (_, _, _, piped) in plans)

    def body(*refs):
        ys_hbm, idx_refs, out_refs = refs[0], refs[1:1 + nt], refs[1 + nt:1 + 2 * nt]
        scratch = refs[1 + 2 * nt:]
        idx_vs, (rows_v, gsem, wsem) = scratch[:nt], scratch[nt:]
        wid = _sc_worker_id()
        for idx_hbm, out_hbm, idx_v, (per_w, ch, n_ch, _) in zip(idx_refs, out_refs, idx_vs, plans):
            base = wid * per_w
            pltpu.sync_copy(idx_hbm.at[pl.ds(base, per_w)], idx_v)

            def gather(g, b, idx_v=idx_v, ch=ch):
                return pltpu.make_async_copy(ys_hbm.at[idx_v.at[pl.ds(g * ch, ch)]], rows_v.at[b], gsem.at[b])

            def put(g, b, out_hbm=out_hbm, base=base, ch=ch):
                return pltpu.make_async_copy(rows_v.at[b], out_hbm.at[pl.ds(base + g * ch, ch)], wsem.at[b])

            gather(0, 0).start()

            @pl.loop(0, n_ch, step=2)
            def _(g0, gather=gather, put=put, n_ch=n_ch):
                for b in range(2):
                    g = g0 + b

                    @pl.when(g + 1 < n_ch)
                    def _():
                        @pl.when(g >= 1)
                        def _():
                            put(g - 1, 1 - b).wait()
                        gather(g + 1, 1 - b).start()

                    gather(g, b).wait()
                    put(g, b).start()

            for g in (n_ch - 2, n_ch - 1):
                put(g, g % 2).wait()

    call = pl.kernel(
        body,
        out_type=tuple(jax.ShapeDtypeStruct((s.size, d), F32) for s in slots),
        mesh=plsc.VectorSubcoreMesh(core_axis_name="c", subcore_axis_name="s"),
        scratch_types=[pltpu.VMEM((per_w,), jnp.int32) for (per_w, _, _, _) in plans]
        + [pltpu.VMEM((2, SC_CHUNK, d), F32), pltpu.SemaphoreType.DMA((2,)), pltpu.SemaphoreType.DMA((2,))],
        name="sc_collect",
    )
    out = call(ys, *[s.reshape(s.size) for s in slots])
    return out if isinstance(out, (tuple, list)) else (out,)


def _combine_kernel(final, x_ref, mod_ref, y_ref, tg_ref, fg_ref, after_ref, o_ref):
    bt, t, d = x_ref.shape
    m = bt * t
    g2 = mod_ref[:, :, 5 * d:6 * d]
    g2 = g2.reshape(1, d) if bt == 1 else jnp.broadcast_to(g2, (bt, t, d)).reshape(m, d)
    tg = tg_ref[...]
    f = tg[:, 0:1] * y_ref[0]
    for k in range(1, TOP_K):
        f = f + tg[:, k:k + 1] * y_ref[k]
    out = x_ref[...].reshape(m, d) + g2 * f
    if final:
        out = _rms(out, fg_ref[...])
    o_ref[...] = out.reshape(bt, t, d)


def _combine_call(x_new, mod, y4, tg, final_g, bt, t, final, after):
    b, s, d = x_new.shape
    m = bt * t
    steps = s // t
    return pl.pallas_call(
        functools.partial(_combine_kernel, final),
        grid=(b // bt, steps),
        in_specs=[pl.BlockSpec((bt, t, d), lambda bi, j: (bi, j, 0)),
                  pl.BlockSpec((bt, 1, 6 * d), lambda bi, j: (bi, 0, 0)),
                  pl.BlockSpec((TOP_K, m, d), lambda bi, j: (0, bi * steps + j, 0)),
                  pl.BlockSpec((m, TOPK_LANES), lambda bi, j: (bi * steps + j, 0)),
                  pl.BlockSpec((1, d), lambda bi, j: (0, 0)),
                  pl.BlockSpec(memory_space=pl.ANY)],
        out_specs=pl.BlockSpec((bt, t, d), lambda bi, j: (bi, j, 0)),
        out_shape=jax.ShapeDtypeStruct((b, s, d), F32),
        compiler_params=pltpu.CompilerParams(dimension_semantics=("arbitrary", "arbitrary")),
        name=f"combine_t{t}",
    )(x_new, mod, y4, tg, final_g, after)


def _layer_weights(l, p, cfgs):
    d = p["w_out"].shape[-1]
    row = lambda v: v[l].reshape(1, d)
    wr = p["w_router"][l]
    wr_pad = jnp.zeros((d, ROUTER_LANES), F32).at[:, :N_EXPERTS].set(wr)
    wr_hi = wr_pad.astype(BF16)
    wr_lo = (wr_pad - wr_hi.astype(F32)).astype(BF16)
    b_router = jnp.full((1, ROUTER_LANES), -1e30, F32).at[0, :N_EXPERTS].set(p["b_router"][l])
    mask = jnp.tril(jnp.ones((GMLP_CHUNK, GMLP_CHUNK), dtype=bool))
    w_sp_full = jnp.where(mask[None], p["w_spatial"][l], 0)
    hd = d // GMLP_HEADS
    base = dict(norm_mix_g=row(p["norm_mix_g"]), norm_ffn_g=row(p["norm_ffn_g"]), w_in=p["w_in"][l].astype(BF16),
                ln_v_g=row(p["ln_v_g"]), ln_v_b=row(p["ln_v_b"]), w_a_out=p["w_a_out"][l].astype(BF16),
                w_pool=p["w_pool"][l].astype(BF16), b_pool=row(p["b_pool"]), pool_scale=row(p["pool_scale"]),
                w_dw=jnp.zeros((_round_up(CONV_K, SUBLANES), d), F32).at[:CONV_K].set(p["w_dw"][l]),
                b_dw=row(p["b_dw"]), ln_c_g=row(p["ln_c_g"]), ln_c_b=row(p["ln_c_b"]),
                w_c_out=p["w_c_out"][l].astype(BF16), w_out=p["w_out"][l].astype(BF16),
                wr_hi=wr_hi, wr_lo=wr_lo, b_router=b_router)
    out = []
    for cfg in cfgs:
        w_sp = jnp.zeros((GMLP_HEADS, cfg.ch, cfg.kg), F32).at[:, :, :cfg.ch].set(w_sp_full[:, :cfg.ch, :cfg.ch])
        bs_full = jnp.repeat(p["b_spatial"][l][:, :cfg.ch].T, hd, axis=1)
        m = cfg.bt * cfg.t
        ltri = (jnp.arange(m)[:, None] > jnp.arange(m)[None, :]).astype(BF16)
        out.append(dict(base, w_sp=w_sp.astype(BF16), bs_full=bs_full, pmat=_pool_band(cfg.t, cfg.kp), ltri=ltri))
    return out


def _pad_front(state, rows):
    pad = rows - state.shape[-2]
    return jnp.pad(state, ((0, 0),) * (state.ndim - 2) + ((pad, 0), (0, 0)))


def kernel(x_prompt, x_sample, c_prompt, c_sample, state_pool, state_conv, norm_mix_g, norm_ffn_g, w_ada, b_ada, w_in, ln_v_g, ln_v_b, w_spatial, b_spatial, w_a_out, w_pool, b_pool, pool_scale, w_dw, b_dw, ln_c_g, ln_c_b, w_c_out, w_out, w_router, b_router, w_gate_up, b_gate_up, w_down, b_down, final_norm_g):
    p = dict(norm_mix_g=norm_mix_g, norm_ffn_g=norm_ffn_g, w_in=w_in, ln_v_g=ln_v_g, ln_v_b=ln_v_b,
             w_spatial=w_spatial, b_spatial=b_spatial, w_a_out=w_a_out, w_pool=w_pool, b_pool=b_pool,
             pool_scale=pool_scale, w_dw=w_dw, b_dw=b_dw, ln_c_g=ln_c_g, ln_c_b=ln_c_b, w_c_out=w_c_out,
             w_out=w_out, w_router=w_router, b_router=b_router)
    n_layers = w_in.shape[0]
    bp, sp, d = x_prompt.shape
    bs, ss, _ = x_sample.shape
    tp = min(PROMPT_TILE, sp)
    bts = min(SAMPLE_BT, bs)
    cfg_p = MixerCfg(bt=1, t=tp, ch=GMLP_CHUNK, kg=GMLP_CHUNK, kp=_round_up(POOL_HIST + tp, LANES), d=d,
                     start_pos=0, emit_v=False)
    cfg_s = MixerCfg(bt=bts, t=ss, ch=ss, kg=_round_up(ss, LANES), kp=_round_up(POOL_HIST + ss, LANES), d=d,
                     start_pos=PAST_LEN, emit_v=True)
    n_p, n_s = bp * sp, bs * ss
    n_slots = (n_p + n_s) * TOP_K + N_EXPERTS * MOE_TM

    mod_all = _ada_call(jnp.concatenate([c_prompt, c_sample], axis=0), w_ada, b_ada)
    mod_p = mod_all[:, :bp].reshape(n_layers, bp, 1, 6 * d)
    mod_s = mod_all[:, bp:].reshape(n_layers, bs, 1, 6 * d)
    final_g = final_norm_g.reshape(1, d)

    hb = bp // 2
    xa, xb, xs = x_prompt[:hb], x_prompt[hb:], x_sample
    n_a, n_b = hb * sp, (bp - hb) * sp
    zero_pool = jnp.zeros((bp, POOL_HIST, d), F32)
    zero_conv = jnp.zeros((bp, CONV_HIST, d), F32)
    zero_cnt = jnp.zeros((1, ROUTER_LANES), F32)
    token = zero_cnt
    pending = None
    pools_p, convs_p, pools_s, convs_s, vs = [], [], [], [], []

    def moe_problem(l, hs, tis, rks, counts, after):
        n_rows = sum(h.shape[0] for h in hs)
        top_i = jnp.concatenate(tis, axis=0)[:, :TOP_K]
        rank = jnp.concatenate(rks, axis=0)[:, :TOP_K]
        slot_of, block_e, nused = _route(top_i, rank, counts[0, :N_EXPERTS].astype(jnp.int32))
        slots, o = [], 0
        for h in hs:
            slots.append(slot_of[:, o:o + h.shape[0]])
            o += h.shape[0]
        x_sorted = _sc_dispatch(hs, slots, n_rows * TOP_K + N_EXPERTS * MOE_TM)
        ys = _moe_call(x_sorted, block_e, nused, l, w_gate_up, b_gate_up, w_down, b_down, after)
        return ys, slots

    for l in range(n_layers):
        final = l == n_layers - 1
        lw_p, lw_s = _layer_weights(l, p, (cfg_p, cfg_s))
        xa_new, h2a, tia, tga, rka, cnt_a, npool_a, nconv_a = _mixer_call(
            cfg_p, xa, mod_p[l, :hb], zero_pool[:hb], zero_conv[:hb], lw_p, zero_cnt, token)
        token = cnt_a
        if pending is not None:
            xb, xs = pending(token)
            token = xs
        xb_new, h2b, tib, tgb, rkb, cnt_b, npool_b, nconv_b = _mixer_call(
            cfg_p, xb, mod_p[l, hb:], zero_pool[hb:], zero_conv[hb:], lw_p, zero_cnt, token)
        xs_new, h2s, tis, tgs, rks, cnt_bs, npool_s, nconv_s, v_s = _mixer_call(
            cfg_s, xs, mod_s[l], _pad_front(state_pool[l], POOL_HIST), _pad_front(state_conv[l], CONV_HIST),
            lw_s, cnt_b[0:1], cnt_b)
        ys_a, (slot_a,) = moe_problem(l, [h2a.reshape(n_a, d)], [tia], [rka], cnt_a, cnt_bs)
        ys_b, (slot_b, slot_s) = moe_problem(l, [h2b.reshape(n_b, d), h2s.reshape(n_s, d)], [tib, tis], [rkb, rks],
                                             cnt_bs, ys_a)
        (y4a,) = _sc_collect(ys_a, [slot_a])
        y4b, y4s = _sc_collect(ys_b, [slot_b, slot_s])
        xa = _combine_call(xa_new, mod_p[l, :hb], y4a.reshape(TOP_K, n_a, d), tga, final_g, 1, tp, final, ys_b)
        token = xa

        def pending(after, l=l, final=final, xb_new=xb_new, xs_new=xs_new, y4b=y4b, y4s=y4s, tgb=tgb, tgs=tgs):
            xb_out = _combine_call(xb_new, mod_p[l, hb:], y4b.reshape(TOP_K, n_b, d), tgb, final_g, 1, tp, final,
                                   after)
            xs_out = _combine_call(xs_new, mod_s[l], y4s.reshape(TOP_K, n_s, d), tgs, final_g, bts, ss, final,
                                   xb_out)
            return xb_out, xs_out

        pools_p.append(jnp.concatenate([npool_a, npool_b], axis=0))
        convs_p.append(jnp.concatenate([nconv_a, nconv_b], axis=0))
        pools_s.append(npool_s)
        convs_s.append(nconv_s)
        vs.append(v_s)
    xb, xs = pending(token)
    xp = jnp.concatenate([xa, xb], axis=0)
    return (xp, xs, jnp.stack(pools_p), jnp.stack(convs_p), jnp.stack(pools_s), jnp.stack(convs_s), jnp.stack(vs))
```

```python
import functools
from typing import NamedTuple

import jax
import jax.numpy as jnp
from jax import lax
from jax.experimental import pallas as pl
from jax.experimental.pallas import tpu as pltpu
from jax.experimental.pallas import tpu_sc as plsc

GMLP_CHUNK = 128
GMLP_HEADS = 4
POOL_WINDOWS = (2, 4, 8, 16)
POOL_STATE = max(POOL_WINDOWS) - 1
CONV_K = 31
CONV_STATE = CONV_K - 1
N_BRANCH = 3
N_EXPERTS = 32
TOP_K = 4
SWIGLU_LIMIT = 7.0
SWIGLU_ALPHA = 1.702
EPS = 1e-6
PAST_LEN = 2048

LANES = 128
SUBLANES = 8
V7X_VMEM_LIMIT_BYTES = 56 * 2**20

POOL_HIST = 16
CONV_HIST = 32
ROUTER_LANES = LANES
TOPK_LANES = 8
MOE_TM = 512
PROMPT_TILE = 256
SAMPLE_BT = 8

F32 = jnp.float32
BF16 = jnp.bfloat16


def _round_up(a, m):
    return (a + m - 1) // m * m


class MixerCfg(NamedTuple):
    bt: int
    t: int
    ch: int
    kg: int
    kp: int
    d: int
    start_pos: int
    emit_v: bool
    fuse_in: bool


def _rms(x, g):
    return x * lax.rsqrt(jnp.mean(x * x, axis=-1, keepdims=True) + EPS) * g


def _ln(x, g, b):
    mu = jnp.mean(x, axis=-1, keepdims=True)
    xc = x - mu
    var = jnp.mean(xc * xc, axis=-1, keepdims=True)
    return xc * lax.rsqrt(var + EPS) * g + b


def _dot(a, b):
    return jnp.dot(a, b, preferred_element_type=F32)


def _pack_rows(x):
    half = x.shape[1] // 2
    hi = lax.bitcast_convert_type(x[:, :half].astype(BF16).astype(F32), jnp.int32)
    lo = lax.bitcast_convert_type(x[:, half:].astype(BF16).astype(F32), jnp.int32)
    return hi | lax.shift_right_logical(lo, 16)


def _unpack_rows(w):
    hi = lax.bitcast_convert_type(w & jnp.int32(-65536), F32)
    lo = lax.bitcast_convert_type(lax.shift_left(w, 16), F32)
    return jnp.concatenate([hi, lo], axis=1)


def _moe_mix(y4_ref, tg):
    f = tg[:, 0:1] * _unpack_rows(y4_ref[0])
    for k in range(1, TOP_K):
        f = f + tg[:, k:k + 1] * _unpack_rows(y4_ref[k])
    return f


def _mixer_kernel(cfg, x_ref, mod_ref, pp_ref, pc_ref, nmg_ref, nfg_ref, win_ref, lnvg_ref, lnvb_ref,
                  wsp_ref, bsf_ref, wao_ref, wpool_ref, bpool_ref, pscale_ref, wdw_ref, bdw_ref,
                  lncg_ref, lncb_ref, wco_ref, wout_ref, wrh_ref, wrl_ref, br_ref, pmat_ref, ltri_ref, cnt0_ref,
                  *rest):
    rest = list(rest)
    if cfg.fuse_in:
        y4_ref, tgp_ref, modp_ref = rest[:3]
        rest = rest[3:]
    xo_ref, h2_ref, ti_ref, tg_ref, rk_ref, cnt_ref, npool_ref, nconv_ref = rest[:8]
    rest = rest[8:]
    if cfg.emit_v:
        v_ref, hp_scr, xc_scr, run_scr = rest
    else:
        hp_scr, xc_scr, run_scr = rest
    bt, t, d = cfg.bt, cfg.t, cfg.d
    m = bt * t
    n_slab = d // LANES
    gd = d // len(POOL_WINDOWS)
    hd = d // GMLP_HEADS
    j = pl.program_id(1)

    def mod_rows(k, ref=mod_ref):
        r = ref[:, :, k * d:(k + 1) * d]
        if bt == 1:
            return r.reshape(1, d)
        return jnp.broadcast_to(r, (bt, t, d)).reshape(m, d)

    @pl.when(j == 0)
    def _load_state():
        hp_scr[...] = pp_ref[...]
        for bi in range(bt):
            for c in range(n_slab):
                xc_scr[bi, c, 0:CONV_HIST, :] = pc_ref[bi, :, c * LANES:(c + 1) * LANES]

    x = x_ref[...].reshape(m, d)
    if cfg.fuse_in:
        x = x + mod_rows(5, modp_ref) * _moe_mix(y4_ref, tgp_ref[...])
    sh1, sc1, g1 = mod_rows(0), mod_rows(1), mod_rows(2)
    h = (_rms(x, nmg_ref[...]) * (1.0 + sc1) + sh1).astype(BF16)

    def zcols(lo, hi):
        return _dot(h, win_ref[:, lo:hi])

    off_u, off_v, off_b, off_c, off_g = 0, d, 2 * d, 3 * d, 5 * d

    zb = zcols(off_b, off_b + d)
    lane = lax.broadcasted_iota(jnp.int32, (1, d), 1)
    win_lane = jnp.left_shift(2, lane // gd).astype(F32)
    pos1 = (cfg.start_pos + 1 + j * t + lax.broadcasted_iota(jnp.int32, (t, 1), 0)).astype(F32)
    cnt = jnp.minimum(pos1, win_lane)
    pooled_parts = []
    for bi in range(bt):
        zb_b = zb[bi * t:(bi + 1) * t]
        pieces = [hp_scr[bi], zb_b]
        if cfg.kp > POOL_HIST + t:
            pieces = [jnp.zeros((cfg.kp - POOL_HIST - t, d), F32)] + pieces
        full = jnp.concatenate(pieces, axis=0)
        fullb = full.astype(BF16)
        sums = jnp.concatenate(
            [_dot(pmat_ref[g], fullb[:, g * gd:(g + 1) * gd]) for g in range(len(POOL_WINDOWS))], axis=1)
        pooled_parts.append(sums / cnt - zb_b)
        hp_scr[bi] = full[cfg.kp - POOL_HIST:]
        npool_ref[bi] = hp_scr[bi, pl.ds(POOL_HIST - POOL_STATE, POOL_STATE), :]
    pooled = (pooled_parts[0] if bt == 1 else jnp.concatenate(pooled_parts, axis=0)).astype(BF16)
    y_b = jnp.concatenate(
        [_dot(pooled[:, g * gd:(g + 1) * gd], wpool_ref[g]) for g in range(len(POOL_WINDOWS))], axis=1)
    y_b = (y_b + bpool_ref[...]) * pscale_ref[...]
    merged = jax.nn.sigmoid(zcols(off_g + d, off_g + 2 * d)) * y_b

    u = jax.nn.gelu(zcols(off_u, off_u + d))
    v = _ln(jax.nn.gelu(zcols(off_v, off_v + d)), lnvg_ref[...], lnvb_ref[...])
    if cfg.emit_v:
        v_ref[...] = v.reshape(bt, t, d)
    vb = v.astype(BF16)
    mixed_rows = []
    for r0 in range(0, m, cfg.ch):
        vc = vb[r0:r0 + cfg.ch]
        if cfg.kg > cfg.ch:
            vc = jnp.concatenate([vc, jnp.zeros((cfg.kg - cfg.ch, d), BF16)], axis=0)
        mixed_rows.append(jnp.concatenate(
            [_dot(wsp_ref[hh], vc[:, hh * hd:(hh + 1) * hd]) for hh in range(GMLP_HEADS)], axis=1) + bsf_ref[...])
    mixed = mixed_rows[0] if len(mixed_rows) == 1 else jnp.concatenate(mixed_rows, axis=0)
    y_a = _dot((u * mixed).astype(BF16), wao_ref[...])
    merged = merged + jax.nn.sigmoid(zcols(off_g, off_g + d)) * y_a

    x_c = zcols(off_c, off_c + d) * jax.nn.sigmoid(zcols(off_c + d, off_c + 2 * d))
    conv_parts = []
    for bi in range(bt):
        accs = []
        for c in range(n_slab):
            cs = slice(c * LANES, (c + 1) * LANES)
            xc_scr[bi, c, CONV_HIST:CONV_HIST + t, :] = x_c[bi * t:(bi + 1) * t, cs]
            acc = jnp.broadcast_to(bdw_ref[:, cs], (t, LANES))
            for k in range(CONV_K):
                acc = acc + wdw_ref[k:k + 1, cs] * xc_scr[bi, c, pl.ds(CONV_HIST - CONV_STATE + k, t), :]
            accs.append(acc)
            tail = xc_scr[bi, c, t:t + CONV_HIST, :]
            nconv_ref[bi, :, cs] = xc_scr[bi, c, pl.ds(t + CONV_HIST - CONV_STATE, CONV_STATE), :]
            xc_scr[bi, c, 0:CONV_HIST, :] = tail
        conv_parts.append(jnp.concatenate(accs, axis=1))
    conv = conv_parts[0] if bt == 1 else jnp.concatenate(conv_parts, axis=0)
    y_c = _dot(jax.nn.silu(_ln(conv, lncg_ref[...], lncb_ref[...])).astype(BF16), wco_ref[...])
    merged = merged + jax.nn.sigmoid(zcols(off_g + 2 * d, off_g + 3 * d)) * y_c

    x_new = x + g1 * _dot(merged.astype(BF16), wout_ref[...])
    xo_ref[...] = x_new.reshape(bt, t, d)
    sh2, sc2 = mod_rows(3), mod_rows(4)
    h2 = _rms(x_new, nfg_ref[...]) * (1.0 + sc2) + sh2
    h2_ref[...] = _pack_rows(h2).reshape(bt, t, d // 2)
    h2_hi = h2.astype(BF16)
    h2_lo = (h2 - h2_hi.astype(F32)).astype(BF16)
    logits = _dot(h2_hi, wrh_ref[...]) + _dot(h2_lo, wrh_ref[...]) + _dot(h2_hi, wrl_ref[...]) + br_ref[...]
    lane_r = lax.broadcasted_iota(jnp.int32, (m, ROUTER_LANES), 1).astype(F32)
    lane_k = lax.broadcasted_iota(jnp.int32, (m, TOPK_LANES), 1)
    work = logits
    top_vals, onehots = [], []
    ti = jnp.zeros((m, TOPK_LANES), jnp.int32)
    for r in range(TOP_K):
        mx = jnp.max(work, axis=-1, keepdims=True)
        idx = jnp.min(jnp.where(work == mx, lane_r, float(ROUTER_LANES)), axis=-1, keepdims=True)
        ti = jnp.where(lane_k == r, idx.astype(jnp.int32), ti)
        top_vals.append(mx)
        work = jnp.where(lane_r == idx, -jnp.inf, work)
        onehots.append((lane_r == idx).astype(F32))
    exps = [jnp.exp(tv - top_vals[0]) for tv in top_vals]
    denom = exps[0] + exps[1] + exps[2] + exps[3]
    tg = jnp.zeros((m, TOPK_LANES), F32)
    for r in range(TOP_K):
        tg = jnp.where(lane_k == r, exps[r] / denom, tg)
    ti_ref[...] = ti
    tg_ref[...] = tg

    @pl.when(jnp.logical_and(pl.program_id(0) == 0, j == 0))
    def _init_counts():
        run_scr[...] = cnt0_ref[...]

    base = run_scr[...]
    rk = jnp.zeros((m, TOPK_LANES), jnp.int32)
    for r in range(TOP_K):
        before = _dot(ltri_ref[...], onehots[r].astype(BF16)) + base
        rank = jnp.sum(onehots[r] * before, axis=-1, keepdims=True)
        rk = jnp.where(lane_k == r, rank.astype(jnp.int32), rk)
        base = base + jnp.sum(onehots[r], axis=0, keepdims=True)
    run_scr[...] = base
    rk_ref[...] = rk
    cnt_ref[...] = jnp.broadcast_to(base, cnt_ref.shape)


def _const_spec(shape):
    nd = len(shape)
    return pl.BlockSpec(shape, lambda b, j, _nd=nd: (0,) * _nd, pipeline_mode=pl.Buffered(1))


def _pool_band(t, kp):
    col = jnp.arange(kp)[None, :]
    end = (kp - t) + jnp.arange(t)[:, None]
    return jnp.stack([((col <= end) & (col > end - w)) for w in POOL_WINDOWS]).astype(BF16)


def _mixer_call(cfg, x, mod, prev_pool, prev_conv, lw, cnt0, fuse=()):
    b, s, d = x.shape
    bt, t = cfg.bt, cfg.t
    m = bt * t
    grid = (b // bt, s // t)
    n_slab = d // LANES

    def tile_spec():
        return pl.BlockSpec((bt, t, d), lambda bi, j: (bi, j, 0))

    def batch_spec(rows, width):
        return pl.BlockSpec((bt, rows, width), lambda bi, j: (bi, 0, 0))

    def rows_spec(width):
        return pl.BlockSpec((m, width), lambda bi, j: (bi * (s // t) + j, 0))

    consts = [lw["norm_mix_g"], lw["norm_ffn_g"], lw["w_in"], lw["ln_v_g"], lw["ln_v_b"], lw["w_sp"], lw["bs_full"],
              lw["w_a_out"], lw["w_pool"], lw["b_pool"], lw["pool_scale"], lw["w_dw"], lw["b_dw"], lw["ln_c_g"],
              lw["ln_c_b"], lw["w_c_out"], lw["w_out"], lw["wr_hi"], lw["wr_lo"], lw["b_router"], lw["pmat"],
              lw["ltri"], cnt0]
    in_specs = [tile_spec(), batch_spec(1, 6 * d), batch_spec(POOL_HIST, d), batch_spec(CONV_HIST, d)]
    in_specs += [_const_spec(c.shape) for c in consts]
    if cfg.fuse_in:
        in_specs += [pl.BlockSpec((TOP_K, m, d // 2), lambda bi, j: (0, bi * (s // t) + j, 0)),
                     rows_spec(TOPK_LANES), batch_spec(1, 6 * d)]
    out_shape = [jax.ShapeDtypeStruct((b, s, d), F32), jax.ShapeDtypeStruct((b, s, d // 2), jnp.int32),
                 jax.ShapeDtypeStruct((b * s, TOPK_LANES), jnp.int32), jax.ShapeDtypeStruct((b * s, TOPK_LANES), F32),
                 jax.ShapeDtypeStruct((b * s, TOPK_LANES), jnp.int32),
                 jax.ShapeDtypeStruct((SUBLANES, ROUTER_LANES), F32),
                 jax.ShapeDtypeStruct((b, POOL_STATE, d), F32), jax.ShapeDtypeStruct((b, CONV_STATE, d), F32)]
    out_specs = [tile_spec(), pl.BlockSpec((bt, t, d // 2), lambda bi, j: (bi, j, 0)),
                 rows_spec(TOPK_LANES), rows_spec(TOPK_LANES), rows_spec(TOPK_LANES),
                 pl.BlockSpec((SUBLANES, ROUTER_LANES), lambda bi, j: (0, 0)),
                 batch_spec(POOL_STATE, d), batch_spec(CONV_STATE, d)]
    if cfg.emit_v:
        out_shape.append(jax.ShapeDtypeStruct((b, s, d), F32))
        out_specs.append(tile_spec())
    return pl.pallas_call(
        functools.partial(_mixer_kernel, cfg),
        grid=grid,
        in_specs=in_specs,
        out_specs=out_specs,
        out_shape=out_shape,
        scratch_shapes=[pltpu.VMEM((bt, POOL_HIST, d), F32),
                        pltpu.VMEM((bt, n_slab, CONV_HIST + t, LANES), F32),
                        pltpu.VMEM((1, ROUTER_LANES), F32)],
        compiler_params=pltpu.CompilerParams(dimension_semantics=("arbitrary", "arbitrary"),
                                             vmem_limit_bytes=V7X_VMEM_LIMIT_BYTES),
        name=f"mixer_t{t}",
    )(x, mod, prev_pool, prev_conv, *consts, *fuse)


def _ada_kernel(c_ref, w_ref, b_ref, o_ref):
    o_ref[0] = _dot(jax.nn.silu(c_ref[...]).astype(BF16), w_ref[0].astype(BF16)) + b_ref[0]


def _ada_call(c_all, w_ada, b_ada):
    n_layers, d, six_d = w_ada.shape
    rows = c_all.shape[0]
    bn = six_d // 6
    return pl.pallas_call(
        _ada_kernel,
        grid=(n_layers, six_d // bn),
        in_specs=[pl.BlockSpec((rows, d), lambda l, n: (0, 0)),
                  pl.BlockSpec((1, d, bn), lambda l, n: (l, 0, n)),
                  pl.BlockSpec((1, 1, bn), lambda l, n: (l, 0, n))],
        out_specs=pl.BlockSpec((1, rows, bn), lambda l, n: (l, 0, n)),
        out_shape=jax.ShapeDtypeStruct((n_layers, rows, six_d), F32),
        compiler_params=pltpu.CompilerParams(dimension_semantics=("arbitrary", "arbitrary")),
        name="adaln",
    )(c_all, w_ada, b_ada.reshape(n_layers, 1, six_d))


def _moe_kernel(be_ref, nused_ref, x_ref, wgu_ref, bgu_ref, wdn_ref, bdn_ref, y_ref, wgu_bf, wdn_bf):
    i = pl.program_id(0)
    nused = nused_ref[0]
    dff = wdn_bf.shape[0]

    @pl.when(i < nused)
    def _body():
        changed = jnp.logical_or(i == 0, be_ref[i] != be_ref[jnp.maximum(i - 1, 0)])

        @pl.when(changed)
        def _cast_weights():
            wgu_bf[...] = wgu_ref[0].astype(BF16)
            wdn_bf[...] = wdn_ref[0].astype(BF16)

        gu = _dot(_unpack_rows(x_ref[...]).astype(BF16), wgu_bf[...]) + bgu_ref[0]
        g = jnp.minimum(gu[:, :dff], SWIGLU_LIMIT)
        u = jnp.clip(gu[:, dff:], -SWIGLU_LIMIT, SWIGLU_LIMIT)
        act = (u + 1.0) * g * jax.nn.sigmoid(SWIGLU_ALPHA * g)
        y_ref[...] = _pack_rows(_dot(act.astype(BF16), wdn_bf[...]) + bdn_ref[0])

    @pl.when(i >= nused)
    def _unused_block():
        y_ref[...] = jnp.zeros_like(y_ref)


def _moe_call(xs, block_e, nused, layer, w_gu_all, b_gu_all, w_dn_all, b_dn_all):
    n_blocks = block_e.shape[0]
    n_layers, n_exp, d, two_f = w_gu_all.shape
    dff = two_f // 2
    tm = MOE_TM
    e0 = layer * n_exp
    w_gu = w_gu_all.reshape(n_layers * n_exp, d, two_f)
    b_gu = b_gu_all.reshape(n_layers * n_exp, 1, two_f)
    w_dn = w_dn_all.reshape(n_layers * n_exp, dff, d)
    b_dn = b_dn_all.reshape(n_layers * n_exp, 1, d)
    grid_spec = pltpu.PrefetchScalarGridSpec(
        num_scalar_prefetch=2,
        grid=(n_blocks,),
        in_specs=[
            pl.BlockSpec((tm, d // 2), lambda i, be, nu: (i, 0)),
            pl.BlockSpec((1, d, two_f), lambda i, be, nu: (e0 + be[i], 0, 0)),
            pl.BlockSpec((1, 1, two_f), lambda i, be, nu: (e0 + be[i], 0, 0)),
            pl.BlockSpec((1, dff, d), lambda i, be, nu: (e0 + be[i], 0, 0)),
            pl.BlockSpec((1, 1, d), lambda i, be, nu: (e0 + be[i], 0, 0)),
        ],
        out_specs=pl.BlockSpec((tm, d // 2), lambda i, be, nu: (i, 0)),
        scratch_shapes=[pltpu.VMEM((d, two_f), BF16), pltpu.VMEM((dff, d), BF16)],
    )
    return pl.pallas_call(
        _moe_kernel,
        grid_spec=grid_spec,
        out_shape=jax.ShapeDtypeStruct((n_blocks * tm, d // 2), jnp.int32),
        compiler_params=pltpu.CompilerParams(dimension_semantics=("arbitrary",),
                                             vmem_limit_bytes=V7X_VMEM_LIMIT_BYTES),
        name="moe_experts",
    )(block_e, nused, xs, w_gu, b_gu, w_dn, b_dn)


def _route(top_i, rank, counts):
    tm = MOE_TM
    n_all = top_i.shape[0]
    n_blocks = n_all * TOP_K // tm + N_EXPERTS
    padded = (counts + tm - 1) // tm * tm
    pad_ends = jnp.cumsum(padded)
    pad_starts = pad_ends - padded
    blk = jnp.arange(n_blocks, dtype=jnp.int32)[:, None] * tm
    block_e = jnp.minimum(jnp.sum((pad_ends[None, :] <= blk).astype(jnp.int32), axis=1), N_EXPERTS - 1)
    nused = (pad_ends[-1] // tm).astype(jnp.int32).reshape(1)
    onehot = top_i[:, :, None] == jnp.arange(N_EXPERTS, dtype=jnp.int32)[None, None, :]
    slot_of = jnp.sum(jnp.where(onehot, pad_starts[None, None, :], 0), axis=-1) + rank
    return slot_of.T.astype(jnp.int32), block_e.astype(jnp.int32), nused


SC_CORES = 2
SC_SUBCORES = 16
SC_WORKERS = SC_CORES * SC_SUBCORES
SC_CHUNK = 64


def _sc_worker_id():
    return lax.axis_index("s") * SC_CORES + lax.axis_index("c")


def _sc_plan(n_rows, split=False):
    per_w = n_rows // SC_WORKERS
    ch = min(SC_CHUNK, per_w // 2 if split else per_w)
    n_ch = per_w // ch
    assert per_w * SC_WORKERS == n_rows and n_ch * ch == per_w
    return per_w, ch, n_ch, (n_ch >= 2 and n_ch % 2 == 0)


def _sc_dispatch(tables, slots, n_slots):
    nt = len(tables)
    d, dtype = tables[0].shape[1], tables[0].dtype
    plans = [_sc_plan(h.shape[0]) for h in tables]

    def body(*refs):
        h_refs, idx_refs, out_hbm = refs[:nt], refs[nt:2 * nt], refs[2 * nt]
        scratch = refs[2 * nt + 1:]
        idx_vs, bufs, (rsem, wsem) = scratch[:nt], scratch[nt:2 * nt], scratch[2 * nt:]
        wid = _sc_worker_id()
        for h_hbm, idx_hbm, idx_v, buf, (tpw, ch, n_ch, piped) in zip(h_refs, idx_refs, idx_vs, bufs, plans):
            pltpu.sync_copy(idx_hbm.at[wid], idx_v)

            def read(c, b, h_hbm=h_hbm, buf=buf, tpw=tpw, ch=ch):
                return pltpu.make_async_copy(h_hbm.at[pl.ds(wid * tpw + c * ch, ch)], buf.at[b], rsem.at[b])

            def write(c, k, b, buf=buf, idx_v=idx_v):
                return pltpu.make_async_copy(buf.at[b], out_hbm.at[idx_v.at[c * TOP_K + k]], wsem.at[b])

            if piped:
                read(0, 0).start()

                @pl.loop(0, n_ch, step=2)
                def _(c0, read=read, write=write, n_ch=n_ch):
                    for b in range(2):
                        c = c0 + b

                        @pl.when(c + 1 < n_ch)
                        def _():
                            @pl.when(c >= 1)
                            def _():
                                for k in range(TOP_K):
                                    write(c - 1, k, 1 - b).wait()
                            read(c + 1, 1 - b).start()

                        read(c, b).wait()
                        for k in range(TOP_K):
                            write(c, k, b).start()

                for c in (n_ch - 2, n_ch - 1):
                    for k in range(TOP_K):
                        write(c, k, c % 2).wait()
            else:
                for c in range(n_ch):
                    cp_in = read(c, 0)
                    cp_in.start()
                    cp_in.wait()
                    for k in range(TOP_K):
                        write(c, k, 0).start()
                    for k in range(TOP_K):
                        write(c, k, 0).wait()

    scratch_types = [pltpu.VMEM((n_ch * TOP_K, ch), jnp.int32) for (_, ch, n_ch, _) in plans]
    scratch_types += [pltpu.VMEM((2 if piped else 1, ch, d), dtype) for (_, ch, _, piped) in plans]
    scratch_types += [pltpu.SemaphoreType.DMA((2,)), pltpu.SemaphoreType.DMA((2,))]
    call = pl.kernel(
        body,
        out_type=jax.ShapeDtypeStruct((n_slots, d), dtype),
        mesh=plsc.VectorSubcoreMesh(core_axis_name="c", subcore_axis_name="s"),
        scratch_types=scratch_types,
        name="sc_dispatch",
    )
    idx = [s.reshape(TOP_K, SC_WORKERS, n_ch, ch).transpose(1, 2, 0, 3).reshape(SC_WORKERS, n_ch * TOP_K, ch)
           for s, (_, ch, n_ch, _) in zip(slots, plans)]
    return call(*tables, *idx)


def _sc_collect(ys, slots):
    nt = len(slots)
    d, dtype = ys.shape[1], ys.dtype
    plans = [_sc_plan(s.size, split=True) for s in slots]
    assert all(piped for (_, _, _, piped) in plans)

    def body(*refs):
        ys_hbm, idx_refs, out_refs = refs[0], refs[1:1 + nt], refs[1 + nt:1 + 2 * nt]
        scratch = refs[1 + 2 * nt:]
        idx_vs, bufs, (gsem, wsem) = scratch[:nt], scratch[nt:2 * nt], scratch[2 * nt:]
        wid = _sc_worker_id()
        for idx_hbm, out_hbm, idx_v, rows_v, (per_w, ch, n_ch, _) in zip(idx_refs, out_refs, idx_vs, bufs, plans):
            base = wid * per_w
            pltpu.sync_copy(idx_hbm.at[pl.ds(base, per_w)], idx_v)

            def gather(g, b, idx_v=idx_v, rows_v=rows_v, ch=ch):
                return pltpu.make_async_copy(ys_hbm.at[idx_v.at[pl.ds(g * ch, ch)]], rows_v.at[b], gsem.at[b])

            def put(g, b, out_hbm=out_hbm, rows_v=rows_v, base=base, ch=ch):
                return pltpu.make_async_copy(rows_v.at[b], out_hbm.at[pl.ds(base + g * ch, ch)], wsem.at[b])

            gather(0, 0).start()

            @pl.loop(0, n_ch, step=2)
            def _(g0, gather=gather, put=put, n_ch=n_ch):
                for b in range(2):
                    g = g0 + b

                    @pl.when(g + 1 < n_ch)
                    def _():
                        @pl.when(g >= 1)
                        def _():
                            put(g - 1, 1 - b).wait()
                        gather(g + 1, 1 - b).start()

                    gather(g, b).wait()
                    put(g, b).start()

            for g in (n_ch - 2, n_ch - 1):
                put(g, g % 2).wait()

    call = pl.kernel(
        body,
        out_type=tuple(jax.ShapeDtypeStruct((s.size, d), dtype) for s in slots),
        mesh=plsc.VectorSubcoreMesh(core_axis_name="c", subcore_axis_name="s"),
        scratch_types=[pltpu.VMEM((per_w,), jnp.int32) for (per_w, _, _, _) in plans]
        + [pltpu.VMEM((2, ch, d), dtype) for (_, ch, _, _) in plans]
        + [pltpu.SemaphoreType.DMA((2,)), pltpu.SemaphoreType.DMA((2,))],
        name="sc_collect",
    )
    out = call(ys, *[s.reshape(s.size) for s in slots])
    return out if isinstance(out, (tuple, list)) else (out,)


def _combine_kernel(x_ref, mod_ref, y_ref, tg_ref, fg_ref, o_ref):
    bt, t, d = x_ref.shape
    m = bt * t
    g2 = mod_ref[:, :, 5 * d:6 * d]
    g2 = g2.reshape(1, d) if bt == 1 else jnp.broadcast_to(g2, (bt, t, d)).reshape(m, d)
    out = x_ref[...].reshape(m, d) + g2 * _moe_mix(y_ref, tg_ref[...])
    o_ref[...] = _rms(out, fg_ref[...]).reshape(bt, t, d)


def _combine_call(x_new, mod, y4, tg, final_g, bt, t):
    b, s, d = x_new.shape
    m = bt * t
    steps = s // t
    return pl.pallas_call(
        _combine_kernel,
        grid=(b // bt, steps),
        in_specs=[pl.BlockSpec((bt, t, d), lambda bi, j: (bi, j, 0)),
                  pl.BlockSpec((bt, 1, 6 * d), lambda bi, j: (bi, 0, 0)),
                  pl.BlockSpec((TOP_K, m, d // 2), lambda bi, j: (0, bi * steps + j, 0)),
                  pl.BlockSpec((m, TOPK_LANES), lambda bi, j: (bi * steps + j, 0)),
                  pl.BlockSpec((1, d), lambda bi, j: (0, 0))],
        out_specs=pl.BlockSpec((bt, t, d), lambda bi, j: (bi, j, 0)),
        out_shape=jax.ShapeDtypeStruct((b, s, d), F32),
        compiler_params=pltpu.CompilerParams(dimension_semantics=("arbitrary", "arbitrary")),
        name=f"combine_t{t}",
    )(x_new, mod, y4, tg, final_g)


def _layer_weights(l, p, cfgs):
    d = p["w_out"].shape[-1]
    row = lambda v: v[l].reshape(1, d)
    wr = p["w_router"][l]
    wr_pad = jnp.zeros((d, ROUTER_LANES), F32).at[:, :N_EXPERTS].set(wr)
    wr_hi = wr_pad.astype(BF16)
    wr_lo = (wr_pad - wr_hi.astype(F32)).astype(BF16)
    b_router = jnp.full((1, ROUTER_LANES), -1e30, F32).at[0, :N_EXPERTS].set(p["b_router"][l])
    mask = jnp.tril(jnp.ones((GMLP_CHUNK, GMLP_CHUNK), dtype=bool))
    w_sp_full = jnp.where(mask[None], p["w_spatial"][l], 0)
    hd = d // GMLP_HEADS
    base = dict(norm_mix_g=row(p["norm_mix_g"]), norm_ffn_g=row(p["norm_ffn_g"]), w_in=p["w_in"][l].astype(BF16),
                ln_v_g=row(p["ln_v_g"]), ln_v_b=row(p["ln_v_b"]), w_a_out=p["w_a_out"][l].astype(BF16),
                w_pool=p["w_pool"][l].astype(BF16), b_pool=row(p["b_pool"]), pool_scale=row(p["pool_scale"]),
                w_dw=jnp.zeros((_round_up(CONV_K, SUBLANES), d), F32).at[:CONV_K].set(p["w_dw"][l]),
                b_dw=row(p["b_dw"]), ln_c_g=row(p["ln_c_g"]), ln_c_b=row(p["ln_c_b"]),
                w_c_out=p["w_c_out"][l].astype(BF16), w_out=p["w_out"][l].astype(BF16),
                wr_hi=wr_hi, wr_lo=wr_lo, b_router=b_router)
    out = []
    for cfg in cfgs:
        w_sp = jnp.zeros((GMLP_HEADS, cfg.ch, cfg.kg), F32).at[:, :, :cfg.ch].set(w_sp_full[:, :cfg.ch, :cfg.ch])
        bs_full = jnp.repeat(p["b_spatial"][l][:, :cfg.ch].T, hd, axis=1)
        m = cfg.bt * cfg.t
        ltri = (jnp.arange(m)[:, None] > jnp.arange(m)[None, :]).astype(BF16)
        out.append(dict(base, w_sp=w_sp.astype(BF16), bs_full=bs_full, pmat=_pool_band(cfg.t, cfg.kp), ltri=ltri))
    return out


def _pad_front(state, rows):
    pad = rows - state.shape[-2]
    return jnp.pad(state, ((0, 0),) * (state.ndim - 2) + ((pad, 0), (0, 0)))


def kernel(x_prompt, x_sample, c_prompt, c_sample, state_pool, state_conv, norm_mix_g, norm_ffn_g, w_ada, b_ada, w_in, ln_v_g, ln_v_b, w_spatial, b_spatial, w_a_out, w_pool, b_pool, pool_scale, w_dw, b_dw, ln_c_g, ln_c_b, w_c_out, w_out, w_router, b_router, w_gate_up, b_gate_up, w_down, b_down, final_norm_g):
    p = dict(norm_mix_g=norm_mix_g, norm_ffn_g=norm_ffn_g, w_in=w_in, ln_v_g=ln_v_g, ln_v_b=ln_v_b,
             w_spatial=w_spatial, b_spatial=b_spatial, w_a_out=w_a_out, w_pool=w_pool, b_pool=b_pool,
             pool_scale=pool_scale, w_dw=w_dw, b_dw=b_dw, ln_c_g=ln_c_g, ln_c_b=ln_c_b, w_c_out=w_c_out,
             w_out=w_out, w_router=w_router, b_router=b_router)
    n_layers = w_in.shape[0]
    bp, sp, d = x_prompt.shape
    bs, ss, _ = x_sample.shape
    tp = min(PROMPT_TILE, sp)
    bts = min(SAMPLE_BT, bs)
    cfg_p = MixerCfg(bt=1, t=tp, ch=GMLP_CHUNK, kg=GMLP_CHUNK, kp=_round_up(POOL_HIST + tp, LANES), d=d,
                     start_pos=0, emit_v=False, fuse_in=False)
    cfg_s = MixerCfg(bt=bts, t=ss, ch=ss, kg=_round_up(ss, LANES), kp=_round_up(POOL_HIST + ss, LANES), d=d,
                     start_pos=PAST_LEN, emit_v=True, fuse_in=False)
    n_p, n_s = bp * sp, bs * ss
    n_slots = (n_p + n_s) * TOP_K + N_EXPERTS * MOE_TM

    mod_all = _ada_call(jnp.concatenate([c_prompt, c_sample], axis=0), w_ada, b_ada)
    mod_p = mod_all[:, :bp].reshape(n_layers, bp, 1, 6 * d)
    mod_s = mod_all[:, bp:].reshape(n_layers, bs, 1, 6 * d)
    final_g = final_norm_g.reshape(1, d)

    xp, xs = x_prompt, x_sample
    zero_pool = jnp.zeros((bp, POOL_HIST, d), F32)
    zero_conv = jnp.zeros((bp, CONV_HIST, d), F32)
    zero_cnt = jnp.zeros((1, ROUTER_LANES), F32)
    fuse_p = fuse_s = ()
    pools_p, convs_p, pools_s, convs_s, vs = [], [], [], [], []
    for l in range(n_layers):
        fused = l > 0
        lw_p, lw_s = _layer_weights(l, p, (cfg_p, cfg_s))
        xp, h2p, tip, tgp, rkp, cnt_p, npool_p, nconv_p = _mixer_call(
            cfg_p._replace(fuse_in=fused), xp, mod_p[l], zero_pool, zero_conv, lw_p, zero_cnt, fuse_p)
        xs, h2s, tis, tgs, rks, cnt_all, npool_s, nconv_s, v_s = _mixer_call(
            cfg_s._replace(fuse_in=fused), xs, mod_s[l], _pad_front(state_pool[l], POOL_HIST),
            _pad_front(state_conv[l], CONV_HIST), lw_s, cnt_p[0:1], fuse_s)
        top_i = jnp.concatenate([tip, tis], axis=0)[:, :TOP_K]
        rank = jnp.concatenate([rkp, rks], axis=0)[:, :TOP_K]
        slot_of, block_e, nused = _route(top_i, rank, cnt_all[0, :N_EXPERTS].astype(jnp.int32))
        slots = [slot_of[:, :n_p], slot_of[:, n_p:]]
        x_sorted = _sc_dispatch([h2p.reshape(n_p, d // 2), h2s.reshape(n_s, d // 2)], slots, n_slots)
        ys = _moe_call(x_sorted, block_e, nused, l, w_gate_up, b_gate_up, w_down, b_down)
        y4p, y4s = _sc_collect(ys, slots)
        fuse_p = (y4p.reshape(TOP_K, n_p, d // 2), tgp, mod_p[l])
        fuse_s = (y4s.reshape(TOP_K, n_s, d // 2), tgs, mod_s[l])
        pools_p.append(npool_p)
        convs_p.append(nconv_p)
        pools_s.append(npool_s)
        convs_s.append(nconv_s)
        vs.append(v_s)
    yp = _combine_call(xp, mod_p[-1], *fuse_p[:2], final_g, 1, tp)
    ys_out = _combine_call(xs, mod_s[-1], *fuse_s[:2], final_g, bts, ss)
    return (yp, ys_out, jnp.stack(pools_p), jnp.stack(convs_p), jnp.stack(pools_s), jnp.stack(convs_s),
            jnp.stack(vs))
```

```python
import functools
from typing import NamedTuple

import jax
import jax.numpy as jnp
from jax import lax
from jax.experimental import pallas as pl
from jax.experimental.pallas import tpu as pltpu
from jax.experimental.pallas import tpu_sc as plsc

GMLP_CHUNK = 128
GMLP_HEADS = 4
POOL_WINDOWS = (2, 4, 8, 16)
POOL_STATE = max(POOL_WINDOWS) - 1
CONV_K = 31
CONV_STATE = CONV_K - 1
N_BRANCH = 3
N_EXPERTS = 32
TOP_K = 4
SWIGLU_LIMIT = 7.0
SWIGLU_ALPHA = 1.702
EPS = 1e-6
PAST_LEN = 2048

LANES = 128
SUBLANES = 8
V7X_VMEM_LIMIT_BYTES = 56 * 2**20

POOL_HIST = 16
CONV_HIST = 32
ROUTER_LANES = LANES
TOPK_LANES = 8
MOE_TM = 512
PROMPT_TILE = 256
SAMPLE_BT = 8

F32 = jnp.float32
BF16 = jnp.bfloat16


def _round_up(a, m):
    return (a + m - 1) // m * m


class MixerCfg(NamedTuple):
    bt: int
    t: int
    ch: int
    kg: int
    kp: int
    d: int
    start_pos: int
    emit_v: bool
    fuse_in: bool


def _rms(x, g):
    return x * lax.rsqrt(jnp.mean(x * x, axis=-1, keepdims=True) + EPS) * g


def _ln(x, g, b):
    mu = jnp.mean(x, axis=-1, keepdims=True)
    xc = x - mu
    var = jnp.mean(xc * xc, axis=-1, keepdims=True)
    return xc * lax.rsqrt(var + EPS) * g + b


def _dot(a, b):
    return jnp.dot(a, b, preferred_element_type=F32)


def _pack_rows(x):
    half = x.shape[1] // 2
    hi = lax.bitcast_convert_type(x[:, :half].astype(BF16).astype(F32), jnp.int32)
    lo = lax.bitcast_convert_type(x[:, half:].astype(BF16).astype(F32), jnp.int32)
    return hi | lax.shift_right_logical(lo, 16)


def _unpack_rows(w):
    hi = lax.bitcast_convert_type(w & jnp.int32(-65536), F32)
    lo = lax.bitcast_convert_type(lax.shift_left(w, 16), F32)
    return jnp.concatenate([hi, lo], axis=1)


def _moe_mix(y4_ref, tg):
    f = tg[:, 0:1] * _unpack_rows(y4_ref[0])
    for k in range(1, TOP_K):
        f = f + tg[:, k:k + 1] * _unpack_rows(y4_ref[k])
    return f


def _mixer_kernel(cfg, x_ref, mod_ref, pp_ref, pc_ref, nmg_ref, nfg_ref, win_ref, lnvg_ref, lnvb_ref,
                  wsp_ref, bsf_ref, wao_ref, wpool_ref, bpool_ref, pscale_ref, wdw_ref, bdw_ref,
                  lncg_ref, lncb_ref, wco_ref, wout_ref, wrh_ref, wrl_ref, br_ref, pmat_ref, ltri_ref, cnt0_ref,
                  *rest):
    rest = list(rest)
    if cfg.fuse_in:
        y4_ref, tgp_ref, modp_ref = rest[:3]
        rest = rest[3:]
    xo_ref, h2_ref, ti_ref, tg_ref, rk_ref, cnt_ref, npool_ref, nconv_ref = rest[:8]
    rest = rest[8:]
    if cfg.emit_v:
        v_ref, hp_scr, xc_scr, run_scr = rest
    else:
        hp_scr, xc_scr, run_scr = rest
    bt, t, d = cfg.bt, cfg.t, cfg.d
    m = bt * t
    n_slab = d // LANES
    gd = d // len(POOL_WINDOWS)
    hd = d // GMLP_HEADS
    j = pl.program_id(1)

    def mod_rows(k, ref=mod_ref):
        r = ref[:, :, k * d:(k + 1) * d]
        if bt == 1:
            return r.reshape(1, d)
        return jnp.broadcast_to(r, (bt, t, d)).reshape(m, d)

    @pl.when(j == 0)
    def _load_state():
        hp_scr[...] = pp_ref[...]
        for bi in range(bt):
            for c in range(n_slab):
                xc_scr[bi, c, 0:CONV_HIST, :] = pc_ref[bi, :, c * LANES:(c + 1) * LANES]

    x = x_ref[...].reshape(m, d)
    if cfg.fuse_in:
        x = x + mod_rows(5, modp_ref) * _moe_mix(y4_ref, tgp_ref[...])
    sh1, sc1, g1 = mod_rows(0), mod_rows(1), mod_rows(2)
    h = (_rms(x, nmg_ref[...]) * (1.0 + sc1) + sh1).astype(BF16)

    def zcols(lo, hi):
        return _dot(h, win_ref[:, lo:hi])

    off_u, off_v, off_b, off_c, off_g = 0, d, 2 * d, 3 * d, 5 * d

    x_c = zcols(off_c, off_c + d) * jax.nn.sigmoid(zcols(off_c + d, off_c + 2 * d))
    for bi in range(bt):
        for c in range(n_slab):
            xc_scr[bi, c, CONV_HIST:CONV_HIST + t, :] = x_c[bi * t:(bi + 1) * t, c * LANES:(c + 1) * LANES]
    proj_cols = dict(zb=off_b, zu=off_u, zv=off_v, ga=off_g, gb=off_g + d, gc=off_g + 2 * d)
    pending_proj = list(proj_cols)
    proj = {}

    def next_proj():
        if pending_proj:
            name = pending_proj.pop(0)
            proj[name] = zcols(proj_cols[name], proj_cols[name] + d)

    conv_parts = []
    for bi in range(bt):
        accs = []
        for c in range(n_slab):
            cs = slice(c * LANES, (c + 1) * LANES)
            acc = jnp.broadcast_to(bdw_ref[:, cs], (t, LANES))
            for k in range(CONV_K):
                acc = acc + wdw_ref[k:k + 1, cs] * xc_scr[bi, c, pl.ds(CONV_HIST - CONV_STATE + k, t), :]
            accs.append(acc)
            tail = xc_scr[bi, c, t:t + CONV_HIST, :]
            nconv_ref[bi, :, cs] = xc_scr[bi, c, pl.ds(t + CONV_HIST - CONV_STATE, CONV_STATE), :]
            xc_scr[bi, c, 0:CONV_HIST, :] = tail
            next_proj()
        conv_parts.append(jnp.concatenate(accs, axis=1))
    while pending_proj:
        next_proj()
    conv = conv_parts[0] if bt == 1 else jnp.concatenate(conv_parts, axis=0)

    zb = proj["zb"]
    lane = lax.broadcasted_iota(jnp.int32, (1, d), 1)
    win_lane = jnp.left_shift(2, lane // gd).astype(F32)
    pos1 = (cfg.start_pos + 1 + j * t + lax.broadcasted_iota(jnp.int32, (t, 1), 0)).astype(F32)
    cnt = jnp.minimum(pos1, win_lane)
    pooled_parts = []
    for bi in range(bt):
        zb_b = zb[bi * t:(bi + 1) * t]
        pieces = [hp_scr[bi], zb_b]
        if cfg.kp > POOL_HIST + t:
            pieces = [jnp.zeros((cfg.kp - POOL_HIST - t, d), F32)] + pieces
        full = jnp.concatenate(pieces, axis=0)
        fullb = full.astype(BF16)
        sums = jnp.concatenate(
            [_dot(pmat_ref[g], fullb[:, g * gd:(g + 1) * gd]) for g in range(len(POOL_WINDOWS))], axis=1)
        pooled_parts.append(sums / cnt - zb_b)
        hp_scr[bi] = full[cfg.kp - POOL_HIST:]
        npool_ref[bi] = hp_scr[bi, pl.ds(POOL_HIST - POOL_STATE, POOL_STATE), :]
    pooled = (pooled_parts[0] if bt == 1 else jnp.concatenate(pooled_parts, axis=0)).astype(BF16)
    y_b = jnp.concatenate(
        [_dot(pooled[:, g * gd:(g + 1) * gd], wpool_ref[g]) for g in range(len(POOL_WINDOWS))], axis=1)
    y_b = (y_b + bpool_ref[...]) * pscale_ref[...]

    v = _ln(jax.nn.gelu(proj["zv"]), lnvg_ref[...], lnvb_ref[...])
    if cfg.emit_v:
        v_ref[...] = v.reshape(bt, t, d)
    vb = v.astype(BF16)
    mixed_rows = []
    for r0 in range(0, m, cfg.ch):
        vc = vb[r0:r0 + cfg.ch]
        if cfg.kg > cfg.ch:
            vc = jnp.concatenate([vc, jnp.zeros((cfg.kg - cfg.ch, d), BF16)], axis=0)
        mixed_rows.append(jnp.concatenate(
            [_dot(wsp_ref[hh], vc[:, hh * hd:(hh + 1) * hd]) for hh in range(GMLP_HEADS)], axis=1) + bsf_ref[...])
    mixed = mixed_rows[0] if len(mixed_rows) == 1 else jnp.concatenate(mixed_rows, axis=0)

    y_c = _dot(jax.nn.silu(_ln(conv, lncg_ref[...], lncb_ref[...])).astype(BF16), wco_ref[...])
    u = jax.nn.gelu(proj["zu"])
    y_a = _dot((u * mixed).astype(BF16), wao_ref[...])
    merged = jax.nn.sigmoid(proj["gb"]) * y_b
    merged = merged + jax.nn.sigmoid(proj["ga"]) * y_a
    merged = merged + jax.nn.sigmoid(proj["gc"]) * y_c

    x_new = x + g1 * _dot(merged.astype(BF16), wout_ref[...])
    xo_ref[...] = x_new.reshape(bt, t, d)
    sh2, sc2 = mod_rows(3), mod_rows(4)
    h2 = _rms(x_new, nfg_ref[...]) * (1.0 + sc2) + sh2
    h2_ref[...] = _pack_rows(h2).reshape(bt, t, d // 2)
    h2_hi = h2.astype(BF16)
    h2_lo = (h2 - h2_hi.astype(F32)).astype(BF16)
    logits = _dot(h2_hi, wrh_ref[...]) + _dot(h2_lo, wrh_ref[...]) + _dot(h2_hi, wrl_ref[...]) + br_ref[...]
    lane_r = lax.broadcasted_iota(jnp.int32, (m, ROUTER_LANES), 1).astype(F32)
    lane_k = lax.broadcasted_iota(jnp.int32, (m, TOPK_LANES), 1)
    work = logits
    top_vals, onehots = [], []
    ti = jnp.zeros((m, TOPK_LANES), jnp.int32)
    for r in range(TOP_K):
        mx = jnp.max(work, axis=-1, keepdims=True)
        idx = jnp.min(jnp.where(work == mx, lane_r, float(ROUTER_LANES)), axis=-1, keepdims=True)
        ti = jnp.where(lane_k == r, idx.astype(jnp.int32), ti)
        top_vals.append(mx)
        work = jnp.where(lane_r == idx, -jnp.inf, work)
        onehots.append((lane_r == idx).astype(F32))
    exps = [jnp.exp(tv - top_vals[0]) for tv in top_vals]
    denom = exps[0] + exps[1] + exps[2] + exps[3]
    tg = jnp.zeros((m, TOPK_LANES), F32)
    for r in range(TOP_K):
        tg = jnp.where(lane_k == r, exps[r] / denom, tg)
    ti_ref[...] = ti
    tg_ref[...] = tg

    @pl.when(jnp.logical_and(pl.program_id(0) == 0, j == 0))
    def _init_counts():
        run_scr[...] = cnt0_ref[...]

    base = run_scr[...]
    rk = jnp.zeros((m, TOPK_LANES), jnp.int32)
    for r in range(TOP_K):
        before = _dot(ltri_ref[...], onehots[r].astype(BF16)) + base
        rank = jnp.sum(onehots[r] * before, axis=-1, keepdims=True)
        rk = jnp.where(lane_k == r, rank.astype(jnp.int32), rk)
        base = base + jnp.sum(onehots[r], axis=0, keepdims=True)
    run_scr[...] = base
    rk_ref[...] = rk
    cnt_ref[...] = jnp.broadcast_to(base, cnt_ref.shape)


def _const_spec(shape):
    nd = len(shape)
    return pl.BlockSpec(shape, lambda b, j, _nd=nd: (0,) * _nd, pipeline_mode=pl.Buffered(1))


def _pool_band(t, kp):
    col = jnp.arange(kp)[None, :]
    end = (kp - t) + jnp.arange(t)[:, None]
    return jnp.stack([((col <= end) & (col > end - w)) for w in POOL_WINDOWS]).astype(BF16)


def _mixer_call(cfg, x, mod, prev_pool, prev_conv, lw, cnt0, fuse=()):
    b, s, d = x.shape
    bt, t = cfg.bt, cfg.t
    m = bt * t
    grid = (b // bt, s // t)
    n_slab = d // LANES

    def tile_spec():
        return pl.BlockSpec((bt, t, d), lambda bi, j: (bi, j, 0))

    def batch_spec(rows, width):
        return pl.BlockSpec((bt, rows, width), lambda bi, j: (bi, 0, 0))

    def rows_spec(width):
        return pl.BlockSpec((m, width), lambda bi, j: (bi * (s // t) + j, 0))

    consts = [lw["norm_mix_g"], lw["norm_ffn_g"], lw["w_in"], lw["ln_v_g"], lw["ln_v_b"], lw["w_sp"], lw["bs_full"],
              lw["w_a_out"], lw["w_pool"], lw["b_pool"], lw["pool_scale"], lw["w_dw"], lw["b_dw"], lw["ln_c_g"],
              lw["ln_c_b"], lw["w_c_out"], lw["w_out"], lw["wr_hi"], lw["wr_lo"], lw["b_router"], lw["pmat"],
              lw["ltri"], cnt0]
    in_specs = [tile_spec(), batch_spec(1, 6 * d), batch_spec(POOL_HIST, d), batch_spec(CONV_HIST, d)]
    in_specs += [_const_spec(c.shape) for c in consts]
    if cfg.fuse_in:
        in_specs += [pl.BlockSpec((TOP_K, m, d // 2), lambda bi, j: (0, bi * (s // t) + j, 0)),
                     rows_spec(TOPK_LANES), batch_spec(1, 6 * d)]
    out_shape = [jax.ShapeDtypeStruct((b, s, d), F32), jax.ShapeDtypeStruct((b, s, d // 2), jnp.int32),
                 jax.ShapeDtypeStruct((b * s, TOPK_LANES), jnp.int32), jax.ShapeDtypeStruct((b * s, TOPK_LANES), F32),
                 jax.ShapeDtypeStruct((b * s, TOPK_LANES), jnp.int32),
                 jax.ShapeDtypeStruct((SUBLANES, ROUTER_LANES), F32),
                 jax.ShapeDtypeStruct((b, POOL_STATE, d), F32), jax.ShapeDtypeStruct((b, CONV_STATE, d), F32)]
    out_specs = [tile_spec(), pl.BlockSpec((bt, t, d // 2), lambda bi, j: (bi, j, 0)),
                 rows_spec(TOPK_LANES), rows_spec(TOPK_LANES), rows_spec(TOPK_LANES),
                 pl.BlockSpec((SUBLANES, ROUTER_LANES), lambda bi, j: (0, 0)),
                 batch_spec(POOL_STATE, d), batch_spec(CONV_STATE, d)]
    if cfg.emit_v:
        out_shape.append(jax.ShapeDtypeStruct((b, s, d), F32))
        out_specs.append(tile_spec())
    return pl.pallas_call(
        functools.partial(_mixer_kernel, cfg),
        grid=grid,
        in_specs=in_specs,
        out_specs=out_specs,
        out_shape=out_shape,
        scratch_shapes=[pltpu.VMEM((bt, POOL_HIST, d), F32),
                        pltpu.VMEM((bt, n_slab, CONV_HIST + t, LANES), F32),
                        pltpu.VMEM((1, ROUTER_LANES), F32)],
        compiler_params=pltpu.CompilerParams(dimension_semantics=("arbitrary", "arbitrary"),
                                             vmem_limit_bytes=V7X_VMEM_LIMIT_BYTES),
        name=f"mixer_t{t}",
    )(x, mod, prev_pool, prev_conv, *consts, *fuse)


def _ada_kernel(c_ref, w_ref, b_ref, o_ref):
    o_ref[0] = _dot(jax.nn.silu(c_ref[...]).astype(BF16), w_ref[0].astype(BF16)) + b_ref[0]


def _ada_call(c_all, w_ada, b_ada):
    n_layers, d, six_d = w_ada.shape
    rows = c_all.shape[0]
    bn = six_d // 6
    return pl.pallas_call(
        _ada_kernel,
        grid=(n_layers, six_d // bn),
        in_specs=[pl.BlockSpec((rows, d), lambda l, n: (0, 0)),
                  pl.BlockSpec((1, d, bn), lambda l, n: (l, 0, n)),
                  pl.BlockSpec((1, 1, bn), lambda l, n: (l, 0, n))],
        out_specs=pl.BlockSpec((1, rows, bn), lambda l, n: (l, 0, n)),
        out_shape=jax.ShapeDtypeStruct((n_layers, rows, six_d), F32),
        compiler_params=pltpu.CompilerParams(dimension_semantics=("arbitrary", "arbitrary")),
        name="adaln",
    )(c_all, w_ada, b_ada.reshape(n_layers, 1, six_d))


def _moe_kernel(be_ref, nused_ref, x_ref, wgu_ref, bgu_ref, wdn_ref, bdn_ref, y_ref, wgu_bf, wdn_bf):
    i = pl.program_id(0)
    nused = nused_ref[0]
    dff = wdn_bf.shape[0]

    @pl.when(i < nused)
    def _body():
        changed = jnp.logical_or(i == 0, be_ref[i] != be_ref[jnp.maximum(i - 1, 0)])

        @pl.when(changed)
        def _cast_weights():
            wgu_bf[...] = wgu_ref[0].astype(BF16)
            wdn_bf[...] = wdn_ref[0].astype(BF16)

        gu = _dot(_unpack_rows(x_ref[...]).astype(BF16), wgu_bf[...]) + bgu_ref[0]
        g = jnp.minimum(gu[:, :dff], SWIGLU_LIMIT)
        u = jnp.clip(gu[:, dff:], -SWIGLU_LIMIT, SWIGLU_LIMIT)
        act = (u + 1.0) * g * jax.nn.sigmoid(SWIGLU_ALPHA * g)
        y_ref[...] = _pack_rows(_dot(act.astype(BF16), wdn_bf[...]) + bdn_ref[0])

    @pl.when(i >= nused)
    def _unused_block():
        y_ref[...] = jnp.zeros_like(y_ref)


def _moe_call(xs, block_e, nused, layer, w_gu_all, b_gu_all, w_dn_all, b_dn_all):
    n_blocks = block_e.shape[0]
    n_layers, n_exp, d, two_f = w_gu_all.shape
    dff = two_f // 2
    tm = MOE_TM
    e0 = layer * n_exp
    w_gu = w_gu_all.reshape(n_layers * n_exp, d, two_f)
    b_gu = b_gu_all.reshape(n_layers * n_exp, 1, two_f)
    w_dn = w_dn_all.reshape(n_layers * n_exp, dff, d)
    b_dn = b_dn_all.reshape(n_layers * n_exp, 1, d)
    grid_spec = pltpu.PrefetchScalarGridSpec(
        num_scalar_prefetch=2,
        grid=(n_blocks,),
        in_specs=[
            pl.BlockSpec((tm, d // 2), lambda i, be, nu: (i, 0)),
            pl.BlockSpec((1, d, two_f), lambda i, be, nu: (e0 + be[i], 0, 0)),
            pl.BlockSpec((1, 1, two_f), lambda i, be, nu: (e0 + be[i], 0, 0)),
            pl.BlockSpec((1, dff, d), lambda i, be, nu: (e0 + be[i], 0, 0)),
            pl.BlockSpec((1, 1, d), lambda i, be, nu: (e0 + be[i], 0, 0)),
        ],
        out_specs=pl.BlockSpec((tm, d // 2), lambda i, be, nu: (i, 0)),
        scratch_shapes=[pltpu.VMEM((d, two_f), BF16), pltpu.VMEM((dff, d), BF16)],
    )
    return pl.pallas_call(
        _moe_kernel,
        grid_spec=grid_spec,
        out_shape=jax.ShapeDtypeStruct((n_blocks * tm, d // 2), jnp.int32),
        compiler_params=pltpu.CompilerParams(dimension_semantics=("arbitrary",),
                                             vmem_limit_bytes=V7X_VMEM_LIMIT_BYTES),
        name="moe_experts",
    )(block_e, nused, xs, w_gu, b_gu, w_dn, b_dn)


def _route(top_i, rank, counts):
    tm = MOE_TM
    n_all = top_i.shape[0]
    n_blocks = n_all * TOP_K // tm + N_EXPERTS
    padded = (counts + tm - 1) // tm * tm
    pad_ends = jnp.cumsum(padded)
    pad_starts = pad_ends - padded
    blk = jnp.arange(n_blocks, dtype=jnp.int32)[:, None] * tm
    block_e = jnp.minimum(jnp.sum((pad_ends[None, :] <= blk).astype(jnp.int32), axis=1), N_EXPERTS - 1)
    nused = (pad_ends[-1] // tm).astype(jnp.int32).reshape(1)
    onehot = top_i[:, :, None] == jnp.arange(N_EXPERTS, dtype=jnp.int32)[None, None, :]
    slot_of = jnp.sum(jnp.where(onehot, pad_starts[None, None, :], 0), axis=-1) + rank
    return slot_of.T.astype(jnp.int32), block_e.astype(jnp.int32), nused


SC_CORES = 2
SC_SUBCORES = 16
SC_WORKERS = SC_CORES * SC_SUBCORES
SC_CHUNK = 64


def _sc_worker_id():
    return lax.axis_index("s") * SC_CORES + lax.axis_index("c")


def _sc_plan(n_rows, split=False):
    per_w = n_rows // SC_WORKERS
    ch = min(SC_CHUNK, per_w // 2 if split else per_w)
    n_ch = per_w // ch
    assert per_w * SC_WORKERS == n_rows and n_ch * ch == per_w
    return per_w, ch, n_ch, (n_ch >= 2 and n_ch % 2 == 0)


def _sc_dispatch(tables, slots, n_slots):
    nt = len(tables)
    d, dtype = tables[0].shape[1], tables[0].dtype
    plans = [_sc_plan(h.shape[0]) for h in tables]

    def body(*refs):
        h_refs, idx_refs, out_hbm = refs[:nt], refs[nt:2 * nt], refs[2 * nt]
        scratch = refs[2 * nt + 1:]
        idx_vs, bufs, (rsem, wsem) = scratch[:nt], scratch[nt:2 * nt], scratch[2 * nt:]
        wid = _sc_worker_id()
        for h_hbm, idx_hbm, idx_v, buf, (tpw, ch, n_ch, piped) in zip(h_refs, idx_refs, idx_vs, bufs, plans):
            pltpu.sync_copy(idx_hbm.at[wid], idx_v)

            def read(c, b, h_hbm=h_hbm, buf=buf, tpw=tpw, ch=ch):
                return pltpu.make_async_copy(h_hbm.at[pl.ds(wid * tpw + c * ch, ch)], buf.at[b], rsem.at[b])

            def write(c, k, b, buf=buf, idx_v=idx_v):
                return pltpu.make_async_copy(buf.at[b], out_hbm.at[idx_v.at[c * TOP_K + k]], wsem.at[b])

            if piped:
                read(0, 0).start()

                @pl.loop(0, n_ch, step=2)
                def _(c0, read=read, write=write, n_ch=n_ch):
                    for b in range(2):
                        c = c0 + b

                        @pl.when(c + 1 < n_ch)
                        def _():
                            @pl.when(c >= 1)
                            def _():
                                for k in range(TOP_K):
                                    write(c - 1, k, 1 - b).wait()
                            read(c + 1, 1 - b).start()

                        read(c, b).wait()
                        for k in range(TOP_K):
                            write(c, k, b).start()

                for c in (n_ch - 2, n_ch - 1):
                    for k in range(TOP_K):
                        write(c, k, c % 2).wait()
            else:
                for c in range(n_ch):
                    cp_in = read(c, 0)
                    cp_in.start()
                    cp_in.wait()
                    for k in range(TOP_K):
                        write(c, k, 0).start()
                    for k in range(TOP_K):
                        write(c, k, 0).wait()

    scratch_types = [pltpu.VMEM((n_ch * TOP_K, ch), jnp.int32) for (_, ch, n_ch, _) in plans]
    scratch_types += [pltpu.VMEM((2 if piped else 1, ch, d), dtype) for (_, ch, _, piped) in plans]
    scratch_types += [pltpu.SemaphoreType.DMA((2,)), pltpu.SemaphoreType.DMA((2,))]
    call = pl.kernel(
        body,
        out_type=jax.ShapeDtypeStruct((n_slots, d), dtype),
        mesh=plsc.VectorSubcoreMesh(core_axis_name="c", subcore_axis_name="s"),
        scratch_types=scratch_types,
        name="sc_dispatch",
    )
    idx = [s.reshape(TOP_K, SC_WORKERS, n_ch, ch).transpose(1, 2, 0, 3).reshape(SC_WORKERS, n_ch * TOP_K, ch)
           for s, (_, ch, n_ch, _) in zip(slots, plans)]
    return call(*tables, *idx)


def _sc_collect(ys, slots):
    nt = len(slots)
    d, dtype = ys.shape[1], ys.dtype
    plans = [_sc_plan(s.size, split=True) for s in slots]
    assert all(piped for (_, _, _, piped) in plans)

    def body(*refs):
        ys_hbm, idx_refs, out_refs = refs[0], refs[1:1 + nt], refs[1 + nt:1 + 2 * nt]
        scratch = refs[1 + 2 * nt:]
        idx_vs, bufs, (gsem, wsem) = scratch[:nt], scratch[nt:2 * nt], scratch[2 * nt:]
        wid = _sc_worker_id()
        for idx_hbm, out_hbm, idx_v, rows_v, (per_w, ch, n_ch, _) in zip(idx_refs, out_refs, idx_vs, bufs, plans):
            base = wid * per_w
            pltpu.sync_copy(idx_hbm.at[pl.ds(base, per_w)], idx_v)

            def gather(g, b, idx_v=idx_v, rows_v=rows_v, ch=ch):
                return pltpu.make_async_copy(ys_hbm.at[idx_v.at[pl.ds(g * ch, ch)]], rows_v.at[b], gsem.at[b])

            def put(g, b, out_hbm=out_hbm, rows_v=rows_v, base=base, ch=ch):
                return pltpu.make_async_copy(rows_v.at[b], out_hbm.at[pl.ds(base + g * ch, ch)], wsem.at[b])

            gather(0, 0).start()

            @pl.loop(0, n_ch, step=2)
            def _(g0, gather=gather, put=put, n_ch=n_ch):
                for b in range(2):
                    g = g0 + b

                    @pl.when(g + 1 < n_ch)
                    def _():
                        @pl.when(g >= 1)
                        def _():
                            put(g - 1, 1 - b).wait()
                        gather(g + 1, 1 - b).start()

                    gather(g, b).wait()
                    put(g, b).start()

            for g in (n_ch - 2, n_ch - 1):
                put(g, g % 2).wait()

    call = pl.kernel(
        body,
        out_type=tuple(jax.ShapeDtypeStruct((s.size, d), dtype) for s in slots),
        mesh=plsc.VectorSubcoreMesh(core_axis_name="c", subcore_axis_name="s"),
        scratch_types=[pltpu.VMEM((per_w,), jnp.int32) for (per_w, _, _, _) in plans]
        + [pltpu.VMEM((2, ch, d), dtype) for (_, ch, _, _) in plans]
        + [pltpu.SemaphoreType.DMA((2,)), pltpu.SemaphoreType.DMA((2,))],
        name="sc_collect",
    )
    out = call(ys, *[s.reshape(s.size) for s in slots])
    return out if isinstance(out, (tuple, list)) else (out,)


def _combine_kernel(x_ref, mod_ref, y_ref, tg_ref, fg_ref, o_ref):
    bt, t, d = x_ref.shape
    m = bt * t
    g2 = mod_ref[:, :, 5 * d:6 * d]
    g2 = g2.reshape(1, d) if bt == 1 else jnp.broadcast_to(g2, (bt, t, d)).reshape(m, d)
    out = x_ref[...].reshape(m, d) + g2 * _moe_mix(y_ref, tg_ref[...])
    o_ref[...] = _rms(out, fg_ref[...]).reshape(bt, t, d)


def _combine_call(x_new, mod, y4, tg, final_g, bt, t):
    b, s, d = x_new.shape
    m = bt * t
    steps = s // t
    return pl.pallas_call(
        _combine_kernel,
        grid=(b // bt, steps),
        in_specs=[pl.BlockSpec((bt, t, d), lambda bi, j: (bi, j, 0)),
                  pl.BlockSpec((bt, 1, 6 * d), lambda bi, j: (bi, 0, 0)),
                  pl.BlockSpec((TOP_K, m, d // 2), lambda bi, j: (0, bi * steps + j, 0)),
                  pl.BlockSpec((m, TOPK_LANES), lambda bi, j: (bi * steps + j, 0)),
                  pl.BlockSpec((1, d), lambda bi, j: (0, 0))],
        out_specs=pl.BlockSpec((bt, t, d), lambda bi, j: (bi, j, 0)),
        out_shape=jax.ShapeDtypeStruct((b, s, d), F32),
        compiler_params=pltpu.CompilerParams(dimension_semantics=("arbitrary", "arbitrary")),
        name=f"combine_t{t}",
    )(x_new, mod, y4, tg, final_g)


def _layer_weights(l, p, cfgs):
    d = p["w_out"].shape[-1]
    row = lambda v: v[l].reshape(1, d)
    wr = p["w_router"][l]
    wr_pad = jnp.zeros((d, ROUTER_LANES), F32).at[:, :N_EXPERTS].set(wr)
    wr_hi = wr_pad.astype(BF16)
    wr_lo = (wr_pad - wr_hi.astype(F32)).astype(BF16)
    b_router = jnp.full((1, ROUTER_LANES), -1e30, F32).at[0, :N_EXPERTS].set(p["b_router"][l])
    mask = jnp.tril(jnp.ones((GMLP_CHUNK, GMLP_CHUNK), dtype=bool))
    w_sp_full = jnp.where(mask[None], p["w_spatial"][l], 0)
    hd = d // GMLP_HEADS
    base = dict(norm_mix_g=row(p["norm_mix_g"]), norm_ffn_g=row(p["norm_ffn_g"]), w_in=p["w_in"][l].astype(BF16),
                ln_v_g=row(p["ln_v_g"]), ln_v_b=row(p["ln_v_b"]), w_a_out=p["w_a_out"][l].astype(BF16),
                w_pool=p["w_pool"][l].astype(BF16), b_pool=row(p["b_pool"]), pool_scale=row(p["pool_scale"]),
                w_dw=jnp.zeros((_round_up(CONV_K, SUBLANES), d), F32).at[:CONV_K].set(p["w_dw"][l]),
                b_dw=row(p["b_dw"]), ln_c_g=row(p["ln_c_g"]), ln_c_b=row(p["ln_c_b"]),
                w_c_out=p["w_c_out"][l].astype(BF16), w_out=p["w_out"][l].astype(BF16),
                wr_hi=wr_hi, wr_lo=wr_lo, b_router=b_router)
    out = []
    for cfg in cfgs:
        w_sp = jnp.zeros((GMLP_HEADS, cfg.ch, cfg.kg), F32).at[:, :, :cfg.ch].set(w_sp_full[:, :cfg.ch, :cfg.ch])
        bs_full = jnp.repeat(p["b_spatial"][l][:, :cfg.ch].T, hd, axis=1)
        m = cfg.bt * cfg.t
        ltri = (jnp.arange(m)[:, None] > jnp.arange(m)[None, :]).astype(BF16)
        out.append(dict(base, w_sp=w_sp.astype(BF16), bs_full=bs_full, pmat=_pool_band(cfg.t, cfg.kp), ltri=ltri))
    return out


def _pad_front(state, rows):
    pad = rows - state.shape[-2]
    return jnp.pad(state, ((0, 0),) * (state.ndim - 2) + ((pad, 0), (0, 0)))


def kernel(x_prompt, x_sample, c_prompt, c_sample, state_pool, state_conv, norm_mix_g, norm_ffn_g, w_ada, b_ada, w_in, ln_v_g, ln_v_b, w_spatial, b_spatial, w_a_out, w_pool, b_pool, pool_scale, w_dw, b_dw, ln_c_g, ln_c_b, w_c_out, w_out, w_router, b_router, w_gate_up, b_gate_up, w_down, b_down, final_norm_g):
    p = dict(norm_mix_g=norm_mix_g, norm_ffn_g=norm_ffn_g, w_in=w_in, ln_v_g=ln_v_g, ln_v_b=ln_v_b,
             w_spatial=w_spatial, b_spatial=b_spatial, w_a_out=w_a_out, w_pool=w_pool, b_pool=b_pool,
             pool_scale=pool_scale, w_dw=w_dw, b_dw=b_dw, ln_c_g=ln_c_g, ln_c_b=ln_c_b, w_c_out=w_c_out,
             w_out=w_out, w_router=w_router, b_router=b_router)
    n_layers = w_in.shape[0]
    bp, sp, d = x_prompt.shape
    bs, ss, _ = x_sample.shape
    tp = min(PROMPT_TILE, sp)
    bts = min(SAMPLE_BT, bs)
    cfg_p = MixerCfg(bt=1, t=tp, ch=GMLP_CHUNK, kg=GMLP_CHUNK, kp=_round_up(POOL_HIST + tp, LANES), d=d,
                     start_pos=0, emit_v=False, fuse_in=False)
    cfg_s = MixerCfg(bt=bts, t=ss, ch=ss, kg=_round_up(ss, LANES), kp=_round_up(POOL_HIST + ss, LANES), d=d,
                     start_pos=PAST_LEN, emit_v=True, fuse_in=False)
    n_p, n_s = bp * sp, bs * ss
    n_slots = (n_p + n_s) * TOP_K + N_EXPERTS * MOE_TM

    mod_all = _ada_call(jnp.concatenate([c_prompt, c_sample], axis=0), w_ada, b_ada)
    mod_p = mod_all[:, :bp].reshape(n_layers, bp, 1, 6 * d)
    mod_s = mod_all[:, bp:].reshape(n_layers, bs, 1, 6 * d)
    final_g = final_norm_g.reshape(1, d)

    xp, xs = x_prompt, x_sample
    zero_pool = jnp.zeros((bp, POOL_HIST, d), F32)
    zero_conv = jnp.zeros((bp, CONV_HIST, d), F32)
    zero_cnt = jnp.zeros((1, ROUTER_LANES), F32)
    fuse_p = fuse_s = ()
    pools_p, convs_p, pools_s, convs_s, vs = [], [], [], [], []
    for l in range(n_layers):
        fused = l > 0
        lw_p, lw_s = _layer_weights(l, p, (cfg_p, cfg_s))
        xp, h2p, tip, tgp, rkp, cnt_p, npool_p, nconv_p = _mixer_call(
            cfg_p._replace(fuse_in=fused), xp, mod_p[l], zero_pool, zero_conv, lw_p, zero_cnt, fuse_p)
        xs, h2s, tis, tgs, rks, cnt_all, npool_s, nconv_s, v_s = _mixer_call(
            cfg_s._replace(fuse_in=fused), xs, mod_s[l], _pad_front(state_pool[l], POOL_HIST),
            _pad_front(state_conv[l], CONV_HIST), lw_s, cnt_p[0:1], fuse_s)
        top_i = jnp.concatenate([tip, tis], axis=0)[:, :TOP_K]
        rank = jnp.concatenate([rkp, rks], axis=0)[:, :TOP_K]
        slot_of, block_e, nused = _route(top_i, rank, cnt_all[0, :N_EXPERTS].astype(jnp.int32))
        slots = [slot_of[:, :n_p], slot_of[:, n_p:]]
        x_sorted = _sc_dispatch([h2p.reshape(n_p, d // 2), h2s.reshape(n_s, d // 2)], slots, n_slots)
        ys = _moe_call(x_sorted, block_e, nused, l, w_gate_up, b_gate_up, w_down, b_down)
        y4p, y4s = _sc_collect(ys, slots)
        fuse_p = (y4p.reshape(TOP_K, n_p, d // 2), tgp, mod_p[l])
        fuse_s = (y4s.reshape(TOP_K, n_s, d // 2), tgs, mod_s[l])
        pools_p.append(npool_p)
        convs_p.append(nconv_p)
        pools_s.append(npool_s)
        convs_s.append(nconv_s)
        vs.append(v_s)
    yp = _combine_call(xp, mod_p[-1], *fuse_p[:2], final_g, 1, tp)
    ys_out = _combine_call(xs, mod_s[-1], *fuse_s[:2], final_g, bts, ss)
    return (yp, ys_out, jnp.stack(pools_p), jnp.stack(convs_p), jnp.stack(pools_s), jnp.stack(convs_s),
            jnp.stack(vs))
```

```python
import functools
from typing import NamedTuple

import jax
import jax.numpy as jnp
from jax import lax
from jax.experimental import pallas as pl
from jax.experimental.pallas import tpu as pltpu
from jax.experimental.pallas import tpu_sc as plsc

GMLP_CHUNK = 128
GMLP_HEADS = 4
POOL_WINDOWS = (2, 4, 8, 16)
POOL_STATE = max(POOL_WINDOWS) - 1
CONV_K = 31
CONV_STATE = CONV_K - 1
N_BRANCH = 3
N_EXPERTS = 32
TOP_K = 4
SWIGLU_LIMIT = 7.0
SWIGLU_ALPHA = 1.702
EPS = 1e-6
PAST_LEN = 2048

LANES = 128
SUBLANES = 8
V7X_VMEM_LIMIT_BYTES = 56 * 2**20

POOL_HIST = 16
CONV_HIST = 32
ROUTER_LANES = LANES
TOPK_LANES = 8
MOE_TM = 512
PROMPT_TILE = 256
SAMPLE_BT = 8

F32 = jnp.float32
BF16 = jnp.bfloat16


def _round_up(a, m):
    return (a + m - 1) // m * m


class MixerCfg(NamedTuple):
    bt: int
    t: int
    ch: int
    kg: int
    kp: int
    d: int
    start_pos: int
    emit_v: bool
    fuse_in: bool
    steps: int
    n_tiles: int


def _rms(x, g):
    return x * lax.rsqrt(jnp.mean(x * x, axis=-1, keepdims=True) + EPS) * g


def _ln(x, g, b):
    mu = jnp.mean(x, axis=-1, keepdims=True)
    xc = x - mu
    var = jnp.mean(xc * xc, axis=-1, keepdims=True)
    return xc * lax.rsqrt(var + EPS) * g + b


def _dot(a, b):
    return jnp.dot(a, b, preferred_element_type=F32)


def _pack_rows(x):
    half = x.shape[1] // 2
    hi = lax.bitcast_convert_type(x[:, :half].astype(BF16).astype(F32), jnp.int32)
    lo = lax.bitcast_convert_type(x[:, half:].astype(BF16).astype(F32), jnp.int32)
    return hi | lax.shift_right_logical(lo, 16)


def _unpack_rows(w):
    hi = lax.bitcast_convert_type(w & jnp.int32(-65536), F32)
    lo = lax.bitcast_convert_type(lax.shift_left(w, 16), F32)
    return jnp.concatenate([hi, lo], axis=1)


def _moe_mix(y4_ref, tg):
    f = tg[:, 0:1] * _unpack_rows(y4_ref[0])
    for k in range(1, TOP_K):
        f = f + tg[:, k:k + 1] * _unpack_rows(y4_ref[k])
    return f


def _mixer_kernel(cfg, x_ref, mod_ref, modb_ref, pp_ref, pc_ref, nmg_ref, nfg_ref, win_ref, lnvg_ref, lnvb_ref,
                  wsp_ref, bsf_ref, wao_ref, wpool_ref, bpool_ref, pscale_ref, wdw_ref, bdw_ref,
                  lncg_ref, lncb_ref, wco_ref, wout_ref, wrh_ref, wrl_ref, br_ref, pmat_ref, ltri_ref, cnt0_ref,
                  *rest):
    rest = list(rest)
    if cfg.fuse_in:
        y4_ref, tgp_ref, modp_ref = rest[:3]
        rest = rest[3:]
    xo_ref, h2_ref, ti_ref, tg_ref, rk_ref, cnt_ref, npool_ref, nconv_ref = rest[:8]
    rest = rest[8:]
    if cfg.emit_v:
        v_ref, hp_scr, xc_scr, run_scr, mg_scr, xr_scr = rest
    else:
        hp_scr, xc_scr, run_scr, mg_scr, xr_scr = rest
    bt, t, d = cfg.bt, cfg.t, cfg.d
    m = bt * t
    n_slab = d // LANES
    gd = d // len(POOL_WINDOWS)
    hd = d // GMLP_HEADS
    q = pl.program_id(0)
    j = 0 if cfg.steps == 1 else lax.rem(jnp.minimum(q, cfg.n_tiles - 1), cfg.steps)

    def mod_rows(k, ref=mod_ref):
        r = ref[:, :, k * d:(k + 1) * d]
        if bt == 1:
            return r.reshape(1, d)
        return jnp.broadcast_to(r, (bt, t, d)).reshape(m, d)

    @pl.when(q == 0)
    def _init_carries():
        run_scr[...] = cnt0_ref[...]
        mg_scr[...] = jnp.zeros_like(mg_scr)
        xr_scr[...] = jnp.zeros_like(xr_scr)

    @pl.when(j == 0)
    def _load_state():
        hp_scr[...] = pp_ref[...]
        for bi in range(bt):
            for c in range(n_slab):
                xc_scr[bi, c, 0:CONV_HIST, :] = pc_ref[bi, :, c * LANES:(c + 1) * LANES]

    x_prev = xr_scr[...]
    out_prev = _dot(mg_scr[...], wout_ref[...])

    x = x_ref[...].reshape(m, d)
    if cfg.fuse_in:
        x = x + mod_rows(5, modp_ref) * _moe_mix(y4_ref, tgp_ref[...])
    sh1, sc1 = mod_rows(0), mod_rows(1)
    h = (_rms(x, nmg_ref[...]) * (1.0 + sc1) + sh1).astype(BF16)

    def zcols(lo, hi):
        return _dot(h, win_ref[:, lo:hi])

    off_u, off_v, off_b, off_c, off_g = 0, d, 2 * d, 3 * d, 5 * d
    zc_a = zcols(off_c, off_c + d)
    zc_b = zcols(off_c + d, off_c + 2 * d)

    x_new = x_prev + mod_rows(2, modb_ref) * out_prev
    xo_ref[...] = x_new.reshape(bt, t, d)
    h2 = _rms(x_new, nfg_ref[...]) * (1.0 + mod_rows(4, modb_ref)) + mod_rows(3, modb_ref)
    h2_ref[...] = _pack_rows(h2).reshape(bt, t, d // 2)
    h2_hi = h2.astype(BF16)
    h2_lo = (h2 - h2_hi.astype(F32)).astype(BF16)
    logits = _dot(h2_hi, wrh_ref[...]) + _dot(h2_lo, wrh_ref[...]) + _dot(h2_hi, wrl_ref[...]) + br_ref[...]

    x_c = zc_a * jax.nn.sigmoid(zc_b)
    for bi in range(bt):
        for c in range(n_slab):
            xc_scr[bi, c, CONV_HIST:CONV_HIST + t, :] = x_c[bi * t:(bi + 1) * t, c * LANES:(c + 1) * LANES]

    lane_r = lax.broadcasted_iota(jnp.int32, (m, ROUTER_LANES), 1).astype(F32)
    lane_k = lax.broadcasted_iota(jnp.int32, (m, TOPK_LANES), 1)
    work = logits
    top_vals, onehots = [], []
    ti = jnp.zeros((m, TOPK_LANES), jnp.int32)
    for r in range(TOP_K):
        mx = jnp.max(work, axis=-1, keepdims=True)
        idx = jnp.min(jnp.where(work == mx, lane_r, float(ROUTER_LANES)), axis=-1, keepdims=True)
        ti = jnp.where(lane_k == r, idx.astype(jnp.int32), ti)
        top_vals.append(mx)
        work = jnp.where(lane_r == idx, -jnp.inf, work)
        onehots.append((lane_r == idx).astype(F32))
    exps = [jnp.exp(tv - top_vals[0]) for tv in top_vals]
    denom = exps[0] + exps[1] + exps[2] + exps[3]
    tg = jnp.zeros((m, TOPK_LANES), F32)
    for r in range(TOP_K):
        tg = jnp.where(lane_k == r, exps[r] / denom, tg)
    ti_ref[...] = ti
    tg_ref[...] = tg
    proj_cols = dict(zb=off_b, zu=off_u, zv=off_v, ga=off_g, gb=off_g + d, gc=off_g + 2 * d)
    pending_proj = list(proj_cols)
    proj = {}

    def next_proj():
        if pending_proj:
            name = pending_proj.pop(0)
            proj[name] = zcols(proj_cols[name], proj_cols[name] + d)

    conv_parts = []
    for bi in range(bt):
        accs = []
        for c in range(n_slab):
            cs = slice(c * LANES, (c + 1) * LANES)
            acc = jnp.broadcast_to(bdw_ref[:, cs], (t, LANES))
            for k in range(CONV_K):
                acc = acc + wdw_ref[k:k + 1, cs] * xc_scr[bi, c, pl.ds(CONV_HIST - CONV_STATE + k, t), :]
            accs.append(acc)
            tail = xc_scr[bi, c, t:t + CONV_HIST, :]
            nconv_ref[bi, :, cs] = xc_scr[bi, c, pl.ds(t + CONV_HIST - CONV_STATE, CONV_STATE), :]
            xc_scr[bi, c, 0:CONV_HIST, :] = tail
            next_proj()
        conv_parts.append(jnp.concatenate(accs, axis=1))
    while pending_proj:
        next_proj()
    conv = conv_parts[0] if bt == 1 else jnp.concatenate(conv_parts, axis=0)

    base = run_scr[...]
    rk = jnp.zeros((m, TOPK_LANES), jnp.int32)
    new_base = base
    for r in range(TOP_K):
        before = _dot(ltri_ref[...], onehots[r].astype(BF16)) + new_base
        rank = jnp.sum(onehots[r] * before, axis=-1, keepdims=True)
        rk = jnp.where(lane_k == r, rank.astype(jnp.int32), rk)
        new_base = new_base + jnp.sum(onehots[r], axis=0, keepdims=True)
    new_base = jnp.where(q >= 1, new_base, base)
    run_scr[...] = new_base
    rk_ref[...] = rk
    cnt_ref[...] = jnp.broadcast_to(new_base, cnt_ref.shape)

    zb = proj["zb"]
    lane = lax.broadcasted_iota(jnp.int32, (1, d), 1)
    win_lane = jnp.left_shift(2, lane // gd).astype(F32)
    pos1 = (cfg.start_pos + 1 + j * t + lax.broadcasted_iota(jnp.int32, (t, 1), 0)).astype(F32)
    cnt = jnp.minimum(pos1, win_lane)
    pooled_parts = []
    for bi in range(bt):
        zb_b = zb[bi * t:(bi + 1) * t]
        pieces = [hp_scr[bi], zb_b]
        if cfg.kp > POOL_HIST + t:
            pieces = [jnp.zeros((cfg.kp - POOL_HIST - t, d), F32)] + pieces
        full = jnp.concatenate(pieces, axis=0)
        fullb = full.astype(BF16)
        sums = jnp.concatenate(
            [_dot(pmat_ref[g], fullb[:, g * gd:(g + 1) * gd]) for g in range(len(POOL_WINDOWS))], axis=1)
        pooled_parts.append(sums / cnt - zb_b)
        hp_scr[bi] = full[cfg.kp - POOL_HIST:]
        npool_ref[bi] = hp_scr[bi, pl.ds(POOL_HIST - POOL_STATE, POOL_STATE), :]
    pooled = (pooled_parts[0] if bt == 1 else jnp.concatenate(pooled_parts, axis=0)).astype(BF16)
    y_b = jnp.concatenate(
        [_dot(pooled[:, g * gd:(g + 1) * gd], wpool_ref[g]) for g in range(len(POOL_WINDOWS))], axis=1)
    y_b = (y_b + bpool_ref[...]) * pscale_ref[...]

    v = _ln(jax.nn.gelu(proj["zv"]), lnvg_ref[...], lnvb_ref[...])
    if cfg.emit_v:
        v_ref[...] = v.reshape(bt, t, d)
    vb = v.astype(BF16)
    mixed_rows = []
    for r0 in range(0, m, cfg.ch):
        vc = vb[r0:r0 + cfg.ch]
        if cfg.kg > cfg.ch:
            vc = jnp.concatenate([vc, jnp.zeros((cfg.kg - cfg.ch, d), BF16)], axis=0)
        mixed_rows.append(jnp.concatenate(
            [_dot(wsp_ref[hh], vc[:, hh * hd:(hh + 1) * hd]) for hh in range(GMLP_HEADS)], axis=1) + bsf_ref[...])
    mixed = mixed_rows[0] if len(mixed_rows) == 1 else jnp.concatenate(mixed_rows, axis=0)

    y_c = _dot(jax.nn.silu(_ln(conv, lncg_ref[...], lncb_ref[...])).astype(BF16), wco_ref[...])
    u = jax.nn.gelu(proj["zu"])
    y_a = _dot((u * mixed).astype(BF16), wao_ref[...])
    merged = jax.nn.sigmoid(proj["gb"]) * y_b
    merged = merged + jax.nn.sigmoid(proj["ga"]) * y_a
    merged = merged + jax.nn.sigmoid(proj["gc"]) * y_c

    mg_scr[...] = merged.astype(BF16)
    xr_scr[...] = x


def _const_spec(shape):
    nd = len(shape)
    return pl.BlockSpec(shape, lambda q, _nd=nd: (0,) * _nd, pipeline_mode=pl.Buffered(1))


def _pool_band(t, kp):
    col = jnp.arange(kp)[None, :]
    end = (kp - t) + jnp.arange(t)[:, None]
    return jnp.stack([((col <= end) & (col > end - w)) for w in POOL_WINDOWS]).astype(BF16)


def _mixer_call(cfg, x, mod, prev_pool, prev_conv, lw, cnt0, fuse=()):
    b, s, d = x.shape
    bt, t = cfg.bt, cfg.t
    m = bt * t
    steps, n_tiles = cfg.steps, cfg.n_tiles
    assert steps == s // t and n_tiles == (b // bt) * steps
    assert steps == 1 or t >= max(CONV_STATE, POOL_STATE)
    n_slab = d // LANES

    def front(q):
        return jnp.minimum(q, n_tiles - 1)

    def back(q):
        return jnp.maximum(q - 1, 0)

    def tile_spec(which, width=d):
        return pl.BlockSpec((bt, t, width), lambda q: (which(q) // steps, which(q) % steps, 0))

    def batch_spec(which, rows, width):
        return pl.BlockSpec((bt, rows, width), lambda q: (which(q) // steps, 0, 0))

    def rows_spec(which, width):
        return pl.BlockSpec((m, width), lambda q: (which(q), 0))

    consts = [lw["norm_mix_g"], lw["norm_ffn_g"], lw["w_in"], lw["ln_v_g"], lw["ln_v_b"], lw["w_sp"], lw["bs_full"],
              lw["w_a_out"], lw["w_pool"], lw["b_pool"], lw["pool_scale"], lw["w_dw"], lw["b_dw"], lw["ln_c_g"],
              lw["ln_c_b"], lw["w_c_out"], lw["w_out"], lw["wr_hi"], lw["wr_lo"], lw["b_router"], lw["pmat"],
              lw["ltri"], cnt0]
    in_specs = [tile_spec(front), batch_spec(front, 1, 6 * d), batch_spec(back, 1, 6 * d),
                batch_spec(front, POOL_HIST, d), batch_spec(front, CONV_HIST, d)]
    in_specs += [_const_spec(c.shape) for c in consts]
    if cfg.fuse_in:
        in_specs += [pl.BlockSpec((TOP_K, m, d // 2), lambda q: (0, front(q), 0)),
                     rows_spec(front, TOPK_LANES), batch_spec(front, 1, 6 * d)]
    out_shape = [jax.ShapeDtypeStruct((b, s, d), F32), jax.ShapeDtypeStruct((b, s, d // 2), jnp.int32),
                 jax.ShapeDtypeStruct((b * s, TOPK_LANES), jnp.int32), jax.ShapeDtypeStruct((b * s, TOPK_LANES), F32),
                 jax.ShapeDtypeStruct((b * s, TOPK_LANES), jnp.int32),
                 jax.ShapeDtypeStruct((SUBLANES, ROUTER_LANES), F32),
                 jax.ShapeDtypeStruct((b, POOL_STATE, d), F32), jax.ShapeDtypeStruct((b, CONV_STATE, d), F32)]
    out_specs = [tile_spec(back), tile_spec(back, d // 2),
                 rows_spec(back, TOPK_LANES), rows_spec(back, TOPK_LANES), rows_spec(back, TOPK_LANES),
                 pl.BlockSpec((SUBLANES, ROUTER_LANES), lambda q: (0, 0)),
                 batch_spec(front, POOL_STATE, d), batch_spec(front, CONV_STATE, d)]
    if cfg.emit_v:
        out_shape.append(jax.ShapeDtypeStruct((b, s, d), F32))
        out_specs.append(tile_spec(front))
    return pl.pallas_call(
        functools.partial(_mixer_kernel, cfg),
        grid=(n_tiles + 1,),
        in_specs=in_specs,
        out_specs=out_specs,
        out_shape=out_shape,
        scratch_shapes=[pltpu.VMEM((bt, POOL_HIST, d), F32),
                        pltpu.VMEM((bt, n_slab, CONV_HIST + t, LANES), F32),
                        pltpu.VMEM((1, ROUTER_LANES), F32),
                        pltpu.VMEM((m, d), BF16), pltpu.VMEM((m, d), F32)],
        compiler_params=pltpu.CompilerParams(dimension_semantics=("arbitrary",),
                                             vmem_limit_bytes=V7X_VMEM_LIMIT_BYTES),
        name=f"mixer_t{t}",
    )(x, mod, mod, prev_pool, prev_conv, *consts, *fuse)


def _ada_kernel(c_ref, w_ref, b_ref, o_ref):
    o_ref[0] = _dot(jax.nn.silu(c_ref[...]).astype(BF16), w_ref[0].astype(BF16)) + b_ref[0]


def _ada_call(c_all, w_ada, b_ada):
    n_layers, d, six_d = w_ada.shape
    rows = c_all.shape[0]
    bn = six_d // 6
    return pl.pallas_call(
        _ada_kernel,
        grid=(n_layers, six_d // bn),
        in_specs=[pl.BlockSpec((rows, d), lambda l, n: (0, 0)),
                  pl.BlockSpec((1, d, bn), lambda l, n: (l, 0, n)),
                  pl.BlockSpec((1, 1, bn), lambda l, n: (l, 0, n))],
        out_specs=pl.BlockSpec((1, rows, bn), lambda l, n: (l, 0, n)),
        out_shape=jax.ShapeDtypeStruct((n_layers, rows, six_d), F32),
        compiler_params=pltpu.CompilerParams(dimension_semantics=("arbitrary", "arbitrary")),
        name="adaln",
    )(c_all, w_ada, b_ada.reshape(n_layers, 1, six_d))


def _moe_kernel(be_ref, nused_ref, x_ref, wgu_ref, bgu_ref, wdn_ref, bdn_ref, y_ref, wgu_bf, wdn_bf):
    i = pl.program_id(0)
    nused = nused_ref[0]
    dff = wdn_bf.shape[0]

    @pl.when(i < nused)
    def _body():
        changed = jnp.logical_or(i == 0, be_ref[i] != be_ref[jnp.maximum(i - 1, 0)])

        @pl.when(changed)
        def _cast_weights():
            wgu_bf[...] = wgu_ref[0].astype(BF16)
            wdn_bf[...] = wdn_ref[0].astype(BF16)

        gu = _dot(_unpack_rows(x_ref[...]).astype(BF16), wgu_bf[...]) + bgu_ref[0]
        g = jnp.minimum(gu[:, :dff], SWIGLU_LIMIT)
        u = jnp.clip(gu[:, dff:], -SWIGLU_LIMIT, SWIGLU_LIMIT)
        act = (u + 1.0) * g * jax.nn.sigmoid(SWIGLU_ALPHA * g)
        y_ref[...] = _pack_rows(_dot(act.astype(BF16), wdn_bf[...]) + bdn_ref[0])

    @pl.when(i >= nused)
    def _unused_block():
        y_ref[...] = jnp.zeros_like(y_ref)


def _moe_call(xs, block_e, nused, layer, w_gu_all, b_gu_all, w_dn_all, b_dn_all):
    n_blocks = block_e.shape[0]
    n_layers, n_exp, d, two_f = w_gu_all.shape
    dff = two_f // 2
    tm = MOE_TM
    e0 = layer * n_exp
    w_gu = w_gu_all.reshape(n_layers * n_exp, d, two_f)
    b_gu = b_gu_all.reshape(n_layers * n_exp, 1, two_f)
    w_dn = w_dn_all.reshape(n_layers * n_exp, dff, d)
    b_dn = b_dn_all.reshape(n_layers * n_exp, 1, d)
    grid_spec = pltpu.PrefetchScalarGridSpec(
        num_scalar_prefetch=2,
        grid=(n_blocks,),
        in_specs=[
            pl.BlockSpec((tm, d // 2), lambda i, be, nu: (i, 0)),
            pl.BlockSpec((1, d, two_f), lambda i, be, nu: (e0 + be[i], 0, 0)),
            pl.BlockSpec((1, 1, two_f), lambda i, be, nu: (e0 + be[i], 0, 0)),
            pl.BlockSpec((1, dff, d), lambda i, be, nu: (e0 + be[i], 0, 0)),
            pl.BlockSpec((1, 1, d), lambda i, be, nu: (e0 + be[i], 0, 0)),
        ],
        out_specs=pl.BlockSpec((tm, d // 2), lambda i, be, nu: (i, 0)),
        scratch_shapes=[pltpu.VMEM((d, two_f), BF16), pltpu.VMEM((dff, d), BF16)],
    )
    return pl.pallas_call(
        _moe_kernel,
        grid_spec=grid_spec,
        out_shape=jax.ShapeDtypeStruct((n_blocks * tm, d // 2), jnp.int32),
        compiler_params=pltpu.CompilerParams(dimension_semantics=("arbitrary",),
                                             vmem_limit_bytes=V7X_VMEM_LIMIT_BYTES),
        name="moe_experts",
    )(block_e, nused, xs, w_gu, b_gu, w_dn, b_dn)


def _route(top_i, rank, counts):
    tm = MOE_TM
    n_all = top_i.shape[0]
    n_blocks = n_all * TOP_K // tm + N_EXPERTS
    padded = (counts + tm - 1) // tm * tm
    pad_ends = jnp.cumsum(padded)
    pad_starts = pad_ends - padded
    blk = jnp.arange(n_blocks, dtype=jnp.int32)[:, None] * tm
    block_e = jnp.minimum(jnp.sum((pad_ends[None, :] <= blk).astype(jnp.int32), axis=1), N_EXPERTS - 1)
    nused = (pad_ends[-1] // tm).astype(jnp.int32).reshape(1)
    onehot = top_i[:, :, None] == jnp.arange(N_EXPERTS, dtype=jnp.int32)[None, None, :]
    slot_of = jnp.sum(jnp.where(onehot, pad_starts[None, None, :], 0), axis=-1) + rank
    return slot_of.T.astype(jnp.int32), block_e.astype(jnp.int32), nused


SC_CORES = 2
SC_SUBCORES = 16
SC_WORKERS = SC_CORES * SC_SUBCORES
SC_CHUNK = 64


def _sc_worker_id():
    return lax.axis_index("s") * SC_CORES + lax.axis_index("c")


def _sc_plan(n_rows, split=False):
    per_w = n_rows // SC_WORKERS
    ch = min(SC_CHUNK, per_w // 2 if split else per_w)
    n_ch = per_w // ch
    assert per_w * SC_WORKERS == n_rows and n_ch * ch == per_w
    return per_w, ch, n_ch, (n_ch >= 2 and n_ch % 2 == 0)


def _sc_dispatch(tables, slots, n_slots):
    nt = len(tables)
    d, dtype = tables[0].shape[1], tables[0].dtype
    plans = [_sc_plan(h.shape[0]) for h in tables]

    def body(*refs):
        h_refs, idx_refs, out_hbm = refs[:nt], refs[nt:2 * nt], refs[2 * nt]
        scratch = refs[2 * nt + 1:]
        idx_vs, bufs, (rsem, wsem) = scratch[:nt], scratch[nt:2 * nt], scratch[2 * nt:]
        wid = _sc_worker_id()
        for h_hbm, idx_hbm, idx_v, buf, (tpw, ch, n_ch, piped) in zip(h_refs, idx_refs, idx_vs, bufs, plans):
            pltpu.sync_copy(idx_hbm.at[wid], idx_v)

            def read(c, b, h_hbm=h_hbm, buf=buf, tpw=tpw, ch=ch):
                return pltpu.make_async_copy(h_hbm.at[pl.ds(wid * tpw + c * ch, ch)], buf.at[b], rsem.at[b])

            def write(c, k, b, buf=buf, idx_v=idx_v):
                return pltpu.make_async_copy(buf.at[b], out_hbm.at[idx_v.at[c * TOP_K + k]], wsem.at[b])

            if piped:
                read(0, 0).start()

                @pl.loop(0, n_ch, step=2)
                def _(c0, read=read, write=write, n_ch=n_ch):
                    for b in range(2):
                        c = c0 + b

                        @pl.when(c + 1 < n_ch)
                        def _():
                            @pl.when(c >= 1)
                            def _():
                                for k in range(TOP_K):
                                    write(c - 1, k, 1 - b).wait()
                            read(c + 1, 1 - b).start()

                        read(c, b).wait()
                        for k in range(TOP_K):
                            write(c, k, b).start()

                for c in (n_ch - 2, n_ch - 1):
                    for k in range(TOP_K):
                        write(c, k, c % 2).wait()
            else:
                for c in range(n_ch):
                    cp_in = read(c, 0)
                    cp_in.start()
                    cp_in.wait()
                    for k in range(TOP_K):
                        write(c, k, 0).start()
                    for k in range(TOP_K):
                        write(c, k, 0).wait()

    scratch_types = [pltpu.VMEM((n_ch * TOP_K, ch), jnp.int32) for (_, ch, n_ch, _) in plans]
    scratch_types += [pltpu.VMEM((2 if piped else 1, ch, d), dtype) for (_, ch, _, piped) in plans]
    scratch_types += [pltpu.SemaphoreType.DMA((2,)), pltpu.SemaphoreType.DMA((2,))]
    call = pl.kernel(
        body,
        out_type=jax.ShapeDtypeStruct((n_slots, d), dtype),
        mesh=plsc.VectorSubcoreMesh(core_axis_name="c", subcore_axis_name="s"),
        scratch_types=scratch_types,
        name="sc_dispatch",
    )
    idx = [s.reshape(TOP_K, SC_WORKERS, n_ch, ch).transpose(1, 2, 0, 3).reshape(SC_WORKERS, n_ch * TOP_K, ch)
           for s, (_, ch, n_ch, _) in zip(slots, plans)]
    return call(*tables, *idx)


def _sc_collect(ys, slots):
    nt = len(slots)
    d, dtype = ys.shape[1], ys.dtype
    plans = [_sc_plan(s.size, split=True) for s in slots]
    assert all(piped for (_, _, _, piped) in plans)

    def body(*refs):
        ys_hbm, idx_refs, out_refs = refs[0], refs[1:1 + nt], refs[1 + nt:1 + 2 * nt]
        scratch = refs[1 + 2 * nt:]
        idx_vs, bufs, (gsem, wsem) = scratch[:nt], scratch[nt:2 * nt], scratch[2 * nt:]
        wid = _sc_worker_id()
        for idx_hbm, out_hbm, idx_v, rows_v, (per_w, ch, n_ch, _) in zip(idx_refs, out_refs, idx_vs, bufs, plans):
            base = wid * per_w
            pltpu.sync_copy(idx_hbm.at[pl.ds(base, per_w)], idx_v)

            def gather(g, b, idx_v=idx_v, rows_v=rows_v, ch=ch):
                return pltpu.make_async_copy(ys_hbm.at[idx_v.at[pl.ds(g * ch, ch)]], rows_v.at[b], gsem.at[b])

            def put(g, b, out_hbm=out_hbm, rows_v=rows_v, base=base, ch=ch):
                return pltpu.make_async_copy(rows_v.at[b], out_hbm.at[pl.ds(base + g * ch, ch)], wsem.at[b])

            gather(0, 0).start()

            @pl.loop(0, n_ch, step=2)
            def _(g0, gather=gather, put=put, n_ch=n_ch):
                for b in range(2):
                    g = g0 + b

                    @pl.when(g + 1 < n_ch)
                    def _():
                        @pl.when(g >= 1)
                        def _():
                            put(g - 1, 1 - b).wait()
                        gather(g + 1, 1 - b).start()

                    gather(g, b).wait()
                    put(g, b).start()

            for g in (n_ch - 2, n_ch - 1):
                put(g, g % 2).wait()

    call = pl.kernel(
        body,
        out_type=tuple(jax.ShapeDtypeStruct((s.size, d), dtype) for s in slots),
        mesh=plsc.VectorSubcoreMesh(core_axis_name="c", subcore_axis_name="s"),
        scratch_types=[pltpu.VMEM((per_w,), jnp.int32) for (per_w, _, _, _) in plans]
        + [pltpu.VMEM((2, ch, d), dtype) for (_, ch, _, _) in plans]
        + [pltpu.SemaphoreType.DMA((2,)), pltpu.SemaphoreType.DMA((2,))],
        name="sc_collect",
    )
    out = call(ys, *[s.reshape(s.size) for s in slots])
    return out if isinstance(out, (tuple, list)) else (out,)


def _combine_kernel(x_ref, mod_ref, y_ref, tg_ref, fg_ref, o_ref):
    bt, t, d = x_ref.shape
    m = bt * t
    g2 = mod_ref[:, :, 5 * d:6 * d]
    g2 = g2.reshape(1, d) if bt == 1 else jnp.broadcast_to(g2, (bt, t, d)).reshape(m, d)
    out = x_ref[...].reshape(m, d) + g2 * _moe_mix(y_ref, tg_ref[...])
    o_ref[...] = _rms(out, fg_ref[...]).reshape(bt, t, d)


def _combine_call(x_new, mod, y4, tg, final_g, bt, t):
    b, s, d = x_new.shape
    m = bt * t
    steps = s // t
    return pl.pallas_call(
        _combine_kernel,
        grid=(b // bt, steps),
        in_specs=[pl.BlockSpec((bt, t, d), lambda bi, j: (bi, j, 0)),
                  pl.BlockSpec((bt, 1, 6 * d), lambda bi, j: (bi, 0, 0)),
                  pl.BlockSpec((TOP_K, m, d // 2), lambda bi, j: (0, bi * steps + j, 0)),
                  pl.BlockSpec((m, TOPK_LANES), lambda bi, j: (bi * steps + j, 0)),
                  pl.BlockSpec((1, d), lambda bi, j: (0, 0))],
        out_specs=pl.BlockSpec((bt, t, d), lambda bi, j: (bi, j, 0)),
        out_shape=jax.ShapeDtypeStruct((b, s, d), F32),
        compiler_params=pltpu.CompilerParams(dimension_semantics=("arbitrary", "arbitrary")),
        name=f"combine_t{t}",
    )(x_new, mod, y4, tg, final_g)


def _layer_weights(l, p, cfgs):
    d = p["w_out"].shape[-1]
    row = lambda v: v[l].reshape(1, d)
    wr = p["w_router"][l]
    wr_pad = jnp.zeros((d, ROUTER_LANES), F32).at[:, :N_EXPERTS].set(wr)
    wr_hi = wr_pad.astype(BF16)
    wr_lo = (wr_pad - wr_hi.astype(F32)).astype(BF16)
    b_router = jnp.full((1, ROUTER_LANES), -1e30, F32).at[0, :N_EXPERTS].set(p["b_router"][l])
    mask = jnp.tril(jnp.ones((GMLP_CHUNK, GMLP_CHUNK), dtype=bool))
    w_sp_full = jnp.where(mask[None], p["w_spatial"][l], 0)
    hd = d // GMLP_HEADS
    base = dict(norm_mix_g=row(p["norm_mix_g"]), norm_ffn_g=row(p["norm_ffn_g"]), w_in=p["w_in"][l].astype(BF16),
                ln_v_g=row(p["ln_v_g"]), ln_v_b=row(p["ln_v_b"]), w_a_out=p["w_a_out"][l].astype(BF16),
                w_pool=p["w_pool"][l].astype(BF16), b_pool=row(p["b_pool"]), pool_scale=row(p["pool_scale"]),
                w_dw=jnp.zeros((_round_up(CONV_K, SUBLANES), d), F32).at[:CONV_K].set(p["w_dw"][l]),
                b_dw=row(p["b_dw"]), ln_c_g=row(p["ln_c_g"]), ln_c_b=row(p["ln_c_b"]),
                w_c_out=p["w_c_out"][l].astype(BF16), w_out=p["w_out"][l].astype(BF16),
                wr_hi=wr_hi, wr_lo=wr_lo, b_router=b_router)
    out = []
    for cfg in cfgs:
        w_sp = jnp.zeros((GMLP_HEADS, cfg.ch, cfg.kg), F32).at[:, :, :cfg.ch].set(w_sp_full[:, :cfg.ch, :cfg.ch])
        bs_full = jnp.repeat(p["b_spatial"][l][:, :cfg.ch].T, hd, axis=1)
        m = cfg.bt * cfg.t
        ltri = (jnp.arange(m)[:, None] > jnp.arange(m)[None, :]).astype(BF16)
        out.append(dict(base, w_sp=w_sp.astype(BF16), bs_full=bs_full, pmat=_pool_band(cfg.t, cfg.kp), ltri=ltri))
    return out


def _pad_front(state, rows):
    pad = rows - state.shape[-2]
    return jnp.pad(state, ((0, 0),) * (state.ndim - 2) + ((pad, 0), (0, 0)))


def kernel(x_prompt, x_sample, c_prompt, c_sample, state_pool, state_conv, norm_mix_g, norm_ffn_g, w_ada, b_ada, w_in, ln_v_g, ln_v_b, w_spatial, b_spatial, w_a_out, w_pool, b_pool, pool_scale, w_dw, b_dw, ln_c_g, ln_c_b, w_c_out, w_out, w_router, b_router, w_gate_up, b_gate_up, w_down, b_down, final_norm_g):
    p = dict(norm_mix_g=norm_mix_g, norm_ffn_g=norm_ffn_g, w_in=w_in, ln_v_g=ln_v_g, ln_v_b=ln_v_b,
             w_spatial=w_spatial, b_spatial=b_spatial, w_a_out=w_a_out, w_pool=w_pool, b_pool=b_pool,
             pool_scale=pool_scale, w_dw=w_dw, b_dw=b_dw, ln_c_g=ln_c_g, ln_c_b=ln_c_b, w_c_out=w_c_out,
             w_out=w_out, w_router=w_router, b_router=b_router)
    n_layers = w_in.shape[0]
    bp, sp, d = x_prompt.shape
    bs, ss, _ = x_sample.shape
    tp = min(PROMPT_TILE, sp)
    bts = min(SAMPLE_BT, bs)
    cfg_p = MixerCfg(bt=1, t=tp, ch=GMLP_CHUNK, kg=GMLP_CHUNK, kp=_round_up(POOL_HIST + tp, LANES), d=d,
                     start_pos=0, emit_v=False, fuse_in=False, steps=sp // tp, n_tiles=bp * (sp // tp))
    cfg_s = MixerCfg(bt=bts, t=ss, ch=ss, kg=_round_up(ss, LANES), kp=_round_up(POOL_HIST + ss, LANES), d=d,
                     start_pos=PAST_LEN, emit_v=True, fuse_in=False, steps=1, n_tiles=bs // bts)
    n_p, n_s = bp * sp, bs * ss
    n_slots = (n_p + n_s) * TOP_K + N_EXPERTS * MOE_TM

    mod_all = _ada_call(jnp.concatenate([c_prompt, c_sample], axis=0), w_ada, b_ada)
    mod_p = mod_all[:, :bp].reshape(n_layers, bp, 1, 6 * d)
    mod_s = mod_all[:, bp:].reshape(n_layers, bs, 1, 6 * d)
    final_g = final_norm_g.reshape(1, d)

    xp, xs = x_prompt, x_sample
    zero_pool = jnp.zeros((bp, POOL_HIST, d), F32)
    zero_conv = jnp.zeros((bp, CONV_HIST, d), F32)
    zero_cnt = jnp.zeros((1, ROUTER_LANES), F32)
    fuse_p = fuse_s = ()
    pools_p, convs_p, pools_s, convs_s, vs = [], [], [], [], []
    for l in range(n_layers):
        fused = l > 0
        lw_p, lw_s = _layer_weights(l, p, (cfg_p, cfg_s))
        xp, h2p, tip, tgp, rkp, cnt_p, npool_p, nconv_p = _mixer_call(
            cfg_p._replace(fuse_in=fused), xp, mod_p[l], zero_pool, zero_conv, lw_p, zero_cnt, fuse_p)
        xs, h2s, tis, tgs, rks, cnt_all, npool_s, nconv_s, v_s = _mixer_call(
            cfg_s._replace(fuse_in=fused), xs, mod_s[l], _pad_front(state_pool[l], POOL_HIST),
            _pad_front(state_conv[l], CONV_HIST), lw_s, cnt_p[0:1], fuse_s)
        top_i = jnp.concatenate([tip, tis], axis=0)[:, :TOP_K]
        rank = jnp.concatenate([rkp, rks], axis=0)[:, :TOP_K]
        slot_of, block_e, nused = _route(top_i, rank, cnt_all[0, :N_EXPERTS].astype(jnp.int32))
        slots = [slot_of[:, :n_p], slot_of[:, n_p:]]
        x_sorted = _sc_dispatch([h2p.reshape(n_p, d // 2), h2s.reshape(n_s, d // 2)], slots, n_slots)
        ys = _moe_call(x_sorted, block_e, nused, l, w_gate_up, b_gate_up, w_down, b_down)
        y4p, y4s = _sc_collect(ys, slots)
        fuse_p = (y4p.reshape(TOP_K, n_p, d // 2), tgp, mod_p[l])
        fuse_s = (y4s.reshape(TOP_K, n_s, d // 2), tgs, mod_s[l])
        pools_p.append(npool_p)
        convs_p.append(nconv_p)
        pools_s.append(npool_s)
        convs_s.append(nconv_s)
        vs.append(v_s)
    yp = _combine_call(xp, mod_p[-1], *fuse_p[:2], final_g, 1, tp)
    ys_out = _combine_call(xs, mod_s[-1], *fuse_s[:2], final_g, bts, ss)
    return (yp, ys_out, jnp.stack(pools_p), jnp.stack(convs_p), jnp.stack(pools_s), jnp.stack(convs_s),
            jnp.stack(vs))
```

```python
import functools
from typing import NamedTuple

import jax
import jax.numpy as jnp
from jax import lax
from jax.experimental import pallas as pl
from jax.experimental.pallas import tpu as pltpu
from jax.experimental.pallas import tpu_sc as plsc

GMLP_CHUNK = 128
GMLP_HEADS = 4
POOL_WINDOWS = (2, 4, 8, 16)
POOL_STATE = max(POOL_WINDOWS) - 1
CONV_K = 31
CONV_STATE = CONV_K - 1
N_BRANCH = 3
N_EXPERTS = 32
TOP_K = 4
SWIGLU_LIMIT = 7.0
SWIGLU_ALPHA = 1.702
EPS = 1e-6
PAST_LEN = 2048

LANES = 128
SUBLANES = 8
V7X_VMEM_LIMIT_BYTES = 56 * 2**20

POOL_HIST = 16
CONV_HIST = 32
ROUTER_LANES = LANES
TOPK_LANES = 8
MOE_TM = 512
PROMPT_TILE = 256
SAMPLE_BT = 8

F32 = jnp.float32
BF16 = jnp.bfloat16


def _round_up(a, m):
    return (a + m - 1) // m * m


class MixerCfg(NamedTuple):
    bt: int
    t: int
    ch: int
    kg: int
    kp: int
    d: int
    start_pos: int
    emit_v: bool
    fuse_in: bool
    steps: int
    n_tiles: int


def _rms(x, g):
    return x * lax.rsqrt(jnp.mean(x * x, axis=-1, keepdims=True) + EPS) * g


def _ln(x, g, b):
    mu = jnp.mean(x, axis=-1, keepdims=True)
    xc = x - mu
    var = jnp.mean(xc * xc, axis=-1, keepdims=True)
    return xc * lax.rsqrt(var + EPS) * g + b


def _dot(a, b):
    return jnp.dot(a, b, preferred_element_type=F32)


def _pack_rows(x):
    half = x.shape[1] // 2
    hi = lax.bitcast_convert_type(x[:, :half].astype(BF16).astype(F32), jnp.int32)
    lo = lax.bitcast_convert_type(x[:, half:].astype(BF16).astype(F32), jnp.int32)
    return hi | lax.shift_right_logical(lo, 16)


def _unpack_rows(w):
    hi = lax.bitcast_convert_type(w & jnp.int32(-65536), F32)
    lo = lax.bitcast_convert_type(lax.shift_left(w, 16), F32)
    return jnp.concatenate([hi, lo], axis=1)


def _moe_mix(y4_ref, tg):
    f = tg[:, 0:1] * _unpack_rows(y4_ref[0])
    for k in range(1, TOP_K):
        f = f + tg[:, k:k + 1] * _unpack_rows(y4_ref[k])
    return f


def _mixer_kernel(cfg, x_ref, mod_ref, modb_ref, pp_ref, pc_ref, nmg_ref, nfg_ref, win_ref, lnvg_ref, lnvb_ref,
                  wsp_ref, bsf_ref, wao_ref, wpool_ref, bpool_ref, pscale_ref, wdw_ref, bdw_ref,
                  lncg_ref, lncb_ref, wco_ref, wout_ref, wrh_ref, wrl_ref, br_ref, pmat_ref, ltri_ref, cnt0_ref,
                  *rest):
    rest = list(rest)
    if cfg.fuse_in:
        y4_ref, tgp_ref, modp_ref = rest[:3]
        rest = rest[3:]
    xo_ref, h2_ref, ti_ref, tg_ref, rk_ref, cnt_ref, npool_ref, nconv_ref = rest[:8]
    rest = rest[8:]
    if cfg.emit_v:
        v_ref, hp_scr, xc_scr, run_scr, mg_scr, xr_scr = rest
    else:
        hp_scr, xc_scr, run_scr, mg_scr, xr_scr = rest
    bt, t, d = cfg.bt, cfg.t, cfg.d
    m = bt * t
    n_slab = d // LANES
    gd = d // len(POOL_WINDOWS)
    hd = d // GMLP_HEADS
    q = pl.program_id(0)
    j = 0 if cfg.steps == 1 else lax.rem(jnp.minimum(q, cfg.n_tiles - 1), cfg.steps)

    def mod_rows(k, ref=mod_ref):
        r = ref[:, :, k * d:(k + 1) * d]
        if bt == 1:
            return r.reshape(1, d)
        return jnp.broadcast_to(r, (bt, t, d)).reshape(m, d)

    @pl.when(q == 0)
    def _init_carries():
        run_scr[...] = cnt0_ref[...]
        mg_scr[...] = jnp.zeros_like(mg_scr)
        xr_scr[...] = jnp.zeros_like(xr_scr)

    @pl.when(j == 0)
    def _load_state():
        hp_scr[...] = pp_ref[...]
        for bi in range(bt):
            for c in range(n_slab):
                xc_scr[bi, c, 0:CONV_HIST, :] = pc_ref[bi, :, c * LANES:(c + 1) * LANES]

    x_prev = xr_scr[...]
    out_prev = _dot(mg_scr[...], wout_ref[...])

    x = x_ref[...].reshape(m, d)
    if cfg.fuse_in:
        x = x + mod_rows(5, modp_ref) * _moe_mix(y4_ref, tgp_ref[...])
    sh1, sc1 = mod_rows(0), mod_rows(1)
    h = (_rms(x, nmg_ref[...]) * (1.0 + sc1) + sh1).astype(BF16)

    def zcols(lo, hi):
        return _dot(h, win_ref[:, lo:hi])

    off_u, off_v, off_b, off_c, off_g = 0, d, 2 * d, 3 * d, 5 * d
    zc_a = zcols(off_c, off_c + d)
    zc_b = zcols(off_c + d, off_c + 2 * d)

    x_new = x_prev + mod_rows(2, modb_ref) * out_prev
    xo_ref[...] = x_new.reshape(bt, t, d)
    h2 = _rms(x_new, nfg_ref[...]) * (1.0 + mod_rows(4, modb_ref)) + mod_rows(3, modb_ref)
    h2_ref[...] = _pack_rows(h2).reshape(bt, t, d // 2)
    h2_hi = h2.astype(BF16)
    h2_lo = (h2 - h2_hi.astype(F32)).astype(BF16)
    logits = _dot(h2_hi, wrh_ref[...]) + _dot(h2_lo, wrh_ref[...]) + _dot(h2_hi, wrl_ref[...]) + br_ref[...]

    x_c = zc_a * jax.nn.sigmoid(zc_b)
    for bi in range(bt):
        for c in range(n_slab):
            xc_scr[bi, c, CONV_HIST:CONV_HIST + t, :] = x_c[bi * t:(bi + 1) * t, c * LANES:(c + 1) * LANES]

    lane_r = lax.broadcasted_iota(jnp.int32, (m, ROUTER_LANES), 1).astype(F32)
    lane_k = lax.broadcasted_iota(jnp.int32, (m, TOPK_LANES), 1)
    work = logits
    top_vals, onehots = [], []
    ti = jnp.zeros((m, TOPK_LANES), jnp.int32)
    for r in range(TOP_K):
        mx = jnp.max(work, axis=-1, keepdims=True)
        idx = jnp.min(jnp.where(work == mx, lane_r, float(ROUTER_LANES)), axis=-1, keepdims=True)
        ti = jnp.where(lane_k == r, idx.astype(jnp.int32), ti)
        top_vals.append(mx)
        work = jnp.where(lane_r == idx, -jnp.inf, work)
        onehots.append((lane_r == idx).astype(F32))
    exps = [jnp.exp(tv - top_vals[0]) for tv in top_vals]
    denom = exps[0] + exps[1] + exps[2] + exps[3]
    tg = jnp.zeros((m, TOPK_LANES), F32)
    for r in range(TOP_K):
        tg = jnp.where(lane_k == r, exps[r] / denom, tg)
    ti_ref[...] = ti
    tg_ref[...] = tg
    proj_cols = dict(zb=off_b, zu=off_u, zv=off_v, ga=off_g, gb=off_g + d, gc=off_g + 2 * d)
    pending_proj = list(proj_cols)
    proj = {}

    def next_proj():
        if pending_proj:
            name = pending_proj.pop(0)
            proj[name] = zcols(proj_cols[name], proj_cols[name] + d)

    conv_parts = []
    for bi in range(bt):
        accs = []
        for c in range(n_slab):
            cs = slice(c * LANES, (c + 1) * LANES)
            acc = jnp.broadcast_to(bdw_ref[:, cs], (t, LANES))
            for k in range(CONV_K):
                acc = acc + wdw_ref[k:k + 1, cs] * xc_scr[bi, c, pl.ds(CONV_HIST - CONV_STATE + k, t), :]
            accs.append(acc)
            tail = xc_scr[bi, c, t:t + CONV_HIST, :]
            nconv_ref[bi, :, cs] = xc_scr[bi, c, pl.ds(t + CONV_HIST - CONV_STATE, CONV_STATE), :]
            xc_scr[bi, c, 0:CONV_HIST, :] = tail
            next_proj()
        conv_parts.append(jnp.concatenate(accs, axis=1))
    while pending_proj:
        next_proj()
    conv = conv_parts[0] if bt == 1 else jnp.concatenate(conv_parts, axis=0)

    base = run_scr[...]
    rk = jnp.zeros((m, TOPK_LANES), jnp.int32)
    new_base = base
    for r in range(TOP_K):
        before = _dot(ltri_ref[...], onehots[r].astype(BF16)) + new_base
        rank = jnp.sum(onehots[r] * before, axis=-1, keepdims=True)
        rk = jnp.where(lane_k == r, rank.astype(jnp.int32), rk)
        new_base = new_base + jnp.sum(onehots[r], axis=0, keepdims=True)
    new_base = jnp.where(q >= 1, new_base, base)
    run_scr[...] = new_base
    rk_ref[...] = rk
    cnt_ref[...] = jnp.broadcast_to(new_base, cnt_ref.shape)

    zb = proj["zb"]
    lane = lax.broadcasted_iota(jnp.int32, (1, d), 1)
    win_lane = jnp.left_shift(2, lane // gd).astype(F32)
    pos1 = (cfg.start_pos + 1 + j * t + lax.broadcasted_iota(jnp.int32, (t, 1), 0)).astype(F32)
    cnt = jnp.minimum(pos1, win_lane)
    pooled_parts = []
    for bi in range(bt):
        zb_b = zb[bi * t:(bi + 1) * t]
        pieces = [hp_scr[bi], zb_b]
        if cfg.kp > POOL_HIST + t:
            pieces = [jnp.zeros((cfg.kp - POOL_HIST - t, d), F32)] + pieces
        full = jnp.concatenate(pieces, axis=0)
        fullb = full.astype(BF16)
        sums = jnp.concatenate(
            [_dot(pmat_ref[g], fullb[:, g * gd:(g + 1) * gd]) for g in range(len(POOL_WINDOWS))], axis=1)
        pooled_parts.append(sums / cnt - zb_b)
        hp_scr[bi] = full[cfg.kp - POOL_HIST:]
        npool_ref[bi] = hp_scr[bi, pl.ds(POOL_HIST - POOL_STATE, POOL_STATE), :]
    pooled = (pooled_parts[0] if bt == 1 else jnp.concatenate(pooled_parts, axis=0)).astype(BF16)
    y_b = jnp.concatenate(
        [_dot(pooled[:, g * gd:(g + 1) * gd], wpool_ref[g]) for g in range(len(POOL_WINDOWS))], axis=1)
    y_b = (y_b + bpool_ref[...]) * pscale_ref[...]

    v = _ln(jax.nn.gelu(proj["zv"]), lnvg_ref[...], lnvb_ref[...])
    if cfg.emit_v:
        v_ref[...] = v.reshape(bt, t, d)
    vb = v.astype(BF16)
    mixed_rows = []
    for r0 in range(0, m, cfg.ch):
        vc = vb[r0:r0 + cfg.ch]
        if cfg.kg > cfg.ch:
            vc = jnp.concatenate([vc, jnp.zeros((cfg.kg - cfg.ch, d), BF16)], axis=0)
        mixed_rows.append(jnp.concatenate(
            [_dot(wsp_ref[hh], vc[:, hh * hd:(hh + 1) * hd]) for hh in range(GMLP_HEADS)], axis=1) + bsf_ref[...])
    mixed = mixed_rows[0] if len(mixed_rows) == 1 else jnp.concatenate(mixed_rows, axis=0)

    y_c = _dot(jax.nn.silu(_ln(conv, lncg_ref[...], lncb_ref[...])).astype(BF16), wco_ref[...])
    u = jax.nn.gelu(proj["zu"])
    y_a = _dot((u * mixed).astype(BF16), wao_ref[...])
    merged = jax.nn.sigmoid(proj["gb"]) * y_b
    merged = merged + jax.nn.sigmoid(proj["ga"]) * y_a
    merged = merged + jax.nn.sigmoid(proj["gc"]) * y_c

    mg_scr[...] = merged.astype(BF16)
    xr_scr[...] = x


def _const_spec(shape):
    nd = len(shape)
    return pl.BlockSpec(shape, lambda q, _nd=nd: (0,) * _nd, pipeline_mode=pl.Buffered(1))


def _pool_band(t, kp):
    col = jnp.arange(kp)[None, :]
    end = (kp - t) + jnp.arange(t)[:, None]
    return jnp.stack([((col <= end) & (col > end - w)) for w in POOL_WINDOWS]).astype(BF16)


def _mixer_call(cfg, x, b0, mod, prev_pool, prev_conv, lw, cnt0, fuse=()):
    _, s, d = x.shape
    b = mod.shape[0]
    bt, t = cfg.bt, cfg.t
    m = bt * t
    steps, n_tiles = cfg.steps, cfg.n_tiles
    assert steps == s // t and n_tiles == (b // bt) * steps and b0 % bt == 0
    assert steps == 1 or t >= max(CONV_STATE, POOL_STATE)
    n_slab = d // LANES

    def front(q):
        return jnp.minimum(q, n_tiles - 1)

    def back(q):
        return jnp.maximum(q - 1, 0)

    def tile_spec(which, width=d):
        return pl.BlockSpec((bt, t, width), lambda q: (which(q) // steps, which(q) % steps, 0))

    def batch_spec(which, rows, width):
        return pl.BlockSpec((bt, rows, width), lambda q: (which(q) // steps, 0, 0))

    def rows_spec(which, width):
        return pl.BlockSpec((m, width), lambda q: (which(q), 0))

    consts = [lw["norm_mix_g"], lw["norm_ffn_g"], lw["w_in"], lw["ln_v_g"], lw["ln_v_b"], lw["w_sp"], lw["bs_full"],
              lw["w_a_out"], lw["w_pool"], lw["b_pool"], lw["pool_scale"], lw["w_dw"], lw["b_dw"], lw["ln_c_g"],
              lw["ln_c_b"], lw["w_c_out"], lw["w_out"], lw["wr_hi"], lw["wr_lo"], lw["b_router"], lw["pmat"],
              lw["ltri"], cnt0]
    x_spec = pl.BlockSpec((bt, t, d), lambda q: (front(q) // steps + b0 // bt, front(q) % steps, 0))
    in_specs = [x_spec, batch_spec(front, 1, 6 * d), batch_spec(back, 1, 6 * d),
                batch_spec(front, POOL_HIST, d), batch_spec(front, CONV_HIST, d)]
    in_specs += [_const_spec(c.shape) for c in consts]
    if cfg.fuse_in:
        in_specs += [pl.BlockSpec((TOP_K, m, d // 2), lambda q: (0, front(q), 0)),
                     rows_spec(front, TOPK_LANES), batch_spec(front, 1, 6 * d)]
    out_shape = [jax.ShapeDtypeStruct((b, s, d), F32), jax.ShapeDtypeStruct((b, s, d // 2), jnp.int32),
                 jax.ShapeDtypeStruct((b * s, TOPK_LANES), jnp.int32), jax.ShapeDtypeStruct((b * s, TOPK_LANES), F32),
                 jax.ShapeDtypeStruct((b * s, TOPK_LANES), jnp.int32),
                 jax.ShapeDtypeStruct((SUBLANES, ROUTER_LANES), F32),
                 jax.ShapeDtypeStruct((b, POOL_STATE, d), F32), jax.ShapeDtypeStruct((b, CONV_STATE, d), F32)]
    out_specs = [tile_spec(back), tile_spec(back, d // 2),
                 rows_spec(back, TOPK_LANES), rows_spec(back, TOPK_LANES), rows_spec(back, TOPK_LANES),
                 pl.BlockSpec((SUBLANES, ROUTER_LANES), lambda q: (0, 0)),
                 batch_spec(front, POOL_STATE, d), batch_spec(front, CONV_STATE, d)]
    if cfg.emit_v:
        out_shape.append(jax.ShapeDtypeStruct((b, s, d), F32))
        out_specs.append(tile_spec(front))
    return pl.pallas_call(
        functools.partial(_mixer_kernel, cfg),
        grid=(n_tiles + 1,),
        in_specs=in_specs,
        out_specs=out_specs,
        out_shape=out_shape,
        scratch_shapes=[pltpu.VMEM((bt, POOL_HIST, d), F32),
                        pltpu.VMEM((bt, n_slab, CONV_HIST + t, LANES), F32),
                        pltpu.VMEM((1, ROUTER_LANES), F32),
                        pltpu.VMEM((m, d), BF16), pltpu.VMEM((m, d), F32)],
        compiler_params=pltpu.CompilerParams(dimension_semantics=("arbitrary",),
                                             vmem_limit_bytes=V7X_VMEM_LIMIT_BYTES),
        name=f"mixer_t{t}",
    )(x, mod, mod, prev_pool, prev_conv, *consts, *fuse)


def _ada_kernel(c_ref, w_ref, b_ref, o_ref):
    o_ref[0] = _dot(jax.nn.silu(c_ref[...]).astype(BF16), w_ref[0].astype(BF16)) + b_ref[0]


def _ada_call(c_all, w_ada, b_ada):
    n_layers, d, six_d = w_ada.shape
    rows = c_all.shape[0]
    bn = six_d // 6
    return pl.pallas_call(
        _ada_kernel,
        grid=(n_layers, six_d // bn),
        in_specs=[pl.BlockSpec((rows, d), lambda l, n: (0, 0)),
                  pl.BlockSpec((1, d, bn), lambda l, n: (l, 0, n)),
                  pl.BlockSpec((1, 1, bn), lambda l, n: (l, 0, n))],
        out_specs=pl.BlockSpec((1, rows, bn), lambda l, n: (l, 0, n)),
        out_shape=jax.ShapeDtypeStruct((n_layers, rows, six_d), F32),
        compiler_params=pltpu.CompilerParams(dimension_semantics=("arbitrary", "arbitrary")),
        name="adaln",
    )(c_all, w_ada, b_ada.reshape(n_layers, 1, six_d))


def _moe_kernel(be_ref, nused_ref, x_ref, wgu_ref, bgu_ref, wdn_ref, bdn_ref, y_ref, wgu_bf, wdn_bf):
    i = pl.program_id(0)
    nused = nused_ref[0]
    dff = wdn_bf.shape[0]

    @pl.when(i < nused)
    def _body():
        changed = jnp.logical_or(i == 0, be_ref[i] != be_ref[jnp.maximum(i - 1, 0)])

        @pl.when(changed)
        def _cast_weights():
            wgu_bf[...] = wgu_ref[0].astype(BF16)
            wdn_bf[...] = wdn_ref[0].astype(BF16)

        gu = _dot(_unpack_rows(x_ref[...]).astype(BF16), wgu_bf[...]) + bgu_ref[0]
        g = jnp.minimum(gu[:, :dff], SWIGLU_LIMIT)
        u = jnp.clip(gu[:, dff:], -SWIGLU_LIMIT, SWIGLU_LIMIT)
        act = (u + 1.0) * g * jax.nn.sigmoid(SWIGLU_ALPHA * g)
        y_ref[...] = _pack_rows(_dot(act.astype(BF16), wdn_bf[...]) + bdn_ref[0])

    @pl.when(i >= nused)
    def _unused_block():
        y_ref[...] = jnp.zeros_like(y_ref)


def _moe_call(xs, block_e, nused, layer, w_gu_all, b_gu_all, w_dn_all, b_dn_all):
    n_blocks = block_e.shape[0]
    n_layers, n_exp, d, two_f = w_gu_all.shape
    dff = two_f // 2
    tm = MOE_TM
    e0 = layer * n_exp
    w_gu = w_gu_all.reshape(n_layers * n_exp, d, two_f)
    b_gu = b_gu_all.reshape(n_layers * n_exp, 1, two_f)
    w_dn = w_dn_all.reshape(n_layers * n_exp, dff, d)
    b_dn = b_dn_all.reshape(n_layers * n_exp, 1, d)
    grid_spec = pltpu.PrefetchScalarGridSpec(
        num_scalar_prefetch=2,
        grid=(n_blocks,),
        in_specs=[
            pl.BlockSpec((tm, d // 2), lambda i, be, nu: (i, 0)),
            pl.BlockSpec((1, d, two_f), lambda i, be, nu: (e0 + be[i], 0, 0)),
            pl.BlockSpec((1, 1, two_f), lambda i, be, nu: (e0 + be[i], 0, 0)),
            pl.BlockSpec((1, dff, d), lambda i, be, nu: (e0 + be[i], 0, 0)),
            pl.BlockSpec((1, 1, d), lambda i, be, nu: (e0 + be[i], 0, 0)),
        ],
        out_specs=pl.BlockSpec((tm, d // 2), lambda i, be, nu: (i, 0)),
        scratch_shapes=[pltpu.VMEM((d, two_f), BF16), pltpu.VMEM((dff, d), BF16)],
    )
    return pl.pallas_call(
        _moe_kernel,
        grid_spec=grid_spec,
        out_shape=jax.ShapeDtypeStruct((n_blocks * tm, d // 2), jnp.int32),
        compiler_params=pltpu.CompilerParams(dimension_semantics=("arbitrary",),
                                             vmem_limit_bytes=V7X_VMEM_LIMIT_BYTES),
        name="moe_experts",
    )(block_e, nused, xs, w_gu, b_gu, w_dn, b_dn)


def _route(top_i, rank, counts):
    tm = MOE_TM
    n_all = top_i.shape[0]
    n_blocks = n_all * TOP_K // tm + N_EXPERTS
    padded = (counts + tm - 1) // tm * tm
    pad_ends = jnp.cumsum(padded)
    pad_starts = pad_ends - padded
    blk = jnp.arange(n_blocks, dtype=jnp.int32)[:, None] * tm
    block_e = jnp.minimum(jnp.sum((pad_ends[None, :] <= blk).astype(jnp.int32), axis=1), N_EXPERTS - 1)
    nused = (pad_ends[-1] // tm).astype(jnp.int32).reshape(1)
    onehot = top_i[:, :, None] == jnp.arange(N_EXPERTS, dtype=jnp.int32)[None, None, :]
    slot_of = jnp.sum(jnp.where(onehot, pad_starts[None, None, :], 0), axis=-1) + rank
    return slot_of.T.astype(jnp.int32), block_e.astype(jnp.int32), nused


SC_CORES = 2
SC_SUBCORES = 16
SC_WORKERS = SC_CORES * SC_SUBCORES
SC_CHUNK = 64


def _sc_worker_id():
    return lax.axis_index("s") * SC_CORES + lax.axis_index("c")


def _sc_plan(n_rows, split=False):
    per_w = n_rows // SC_WORKERS
    ch = min(SC_CHUNK, per_w // 2 if split else per_w)
    n_ch = per_w // ch
    assert per_w * SC_WORKERS == n_rows and n_ch * ch == per_w
    return per_w, ch, n_ch, (n_ch >= 2 and n_ch % 2 == 0)


def _sc_dispatch(tables, slots, n_slots):
    nt = len(tables)
    d, dtype = tables[0].shape[1], tables[0].dtype
    plans = [_sc_plan(h.shape[0]) for h in tables]

    buf_shapes = sorted({(2 if piped else 1, ch) for (_, ch, _, piped) in plans})

    def body(*refs):
        h_refs, idx_refs, out_hbm = refs[:nt], refs[nt:2 * nt], refs[2 * nt]
        scratch = refs[2 * nt + 1:]
        idx_vs, shared, (rsem, wsem) = scratch[:nt], scratch[nt:-2], scratch[-2:]
        bufs = [shared[buf_shapes.index((2 if piped else 1, ch))] for (_, ch, _, piped) in plans]
        wid = _sc_worker_id()
        for h_hbm, idx_hbm, idx_v, buf, (tpw, ch, n_ch, piped) in zip(h_refs, idx_refs, idx_vs, bufs, plans):
            pltpu.sync_copy(idx_hbm.at[wid], idx_v)

            def read(c, b, h_hbm=h_hbm, buf=buf, tpw=tpw, ch=ch):
                return pltpu.make_async_copy(h_hbm.at[pl.ds(wid * tpw + c * ch, ch)], buf.at[b], rsem.at[b])

            def write(c, k, b, buf=buf, idx_v=idx_v):
                return pltpu.make_async_copy(buf.at[b], out_hbm.at[idx_v.at[c * TOP_K + k]], wsem.at[b])

            if piped:
                read(0, 0).start()

                @pl.loop(0, n_ch, step=2)
                def _(c0, read=read, write=write, n_ch=n_ch):
                    for b in range(2):
                        c = c0 + b

                        @pl.when(c + 1 < n_ch)
                        def _():
                            @pl.when(c >= 1)
                            def _():
                                for k in range(TOP_K):
                                    write(c - 1, k, 1 - b).wait()
                            read(c + 1, 1 - b).start()

                        read(c, b).wait()
                        for k in range(TOP_K):
                            write(c, k, b).start()

                for c in (n_ch - 2, n_ch - 1):
                    for k in range(TOP_K):
                        write(c, k, c % 2).wait()
            else:
                for c in range(n_ch):
                    cp_in = read(c, 0)
                    cp_in.start()
                    cp_in.wait()
                    for k in range(TOP_K):
                        write(c, k, 0).start()
                    for k in range(TOP_K):
                        write(c, k, 0).wait()

    scratch_types = [pltpu.VMEM((n_ch * TOP_K, ch), jnp.int32) for (_, ch, n_ch, _) in plans]
    scratch_types += [pltpu.VMEM((nbuf, ch, d), dtype) for (nbuf, ch) in buf_shapes]
    scratch_types += [pltpu.SemaphoreType.DMA((2,)), pltpu.SemaphoreType.DMA((2,))]
    call = pl.kernel(
        body,
        out_type=jax.ShapeDtypeStruct((n_slots, d), dtype),
        mesh=plsc.VectorSubcoreMesh(core_axis_name="c", subcore_axis_name="s"),
        scratch_types=scratch_types,
        name="sc_dispatch",
    )
    idx = [s.reshape(TOP_K, SC_WORKERS, n_ch, ch).transpose(1, 2, 0, 3).reshape(SC_WORKERS, n_ch * TOP_K, ch)
           for s, (_, ch, n_ch, _) in zip(slots, plans)]
    return call(*tables, *idx)


def _sc_collect(ys, slots):
    nt = len(slots)
    d, dtype = ys.shape[1], ys.dtype
    plans = [_sc_plan(s.size, split=True) for s in slots]
    assert all(piped for (_, _, _, piped) in plans)

    def body(*refs):
        ys_hbm, idx_refs, out_refs = refs[0], refs[1:1 + nt], refs[1 + nt:1 + 2 * nt]
        scratch = refs[1 + 2 * nt:]
        idx_vs, bufs, (gsem, wsem) = scratch[:nt], scratch[nt:2 * nt], scratch[2 * nt:]
        wid = _sc_worker_id()
        for idx_hbm, out_hbm, idx_v, rows_v, (per_w, ch, n_ch, _) in zip(idx_refs, out_refs, idx_vs, bufs, plans):
            base = wid * per_w
            pltpu.sync_copy(idx_hbm.at[pl.ds(base, per_w)], idx_v)

            def gather(g, b, idx_v=idx_v, rows_v=rows_v, ch=ch):
                return pltpu.make_async_copy(ys_hbm.at[idx_v.at[pl.ds(g * ch, ch)]], rows_v.at[b], gsem.at[b])

            def put(g, b, out_hbm=out_hbm, rows_v=rows_v, base=base, ch=ch):
                return pltpu.make_async_copy(rows_v.at[b], out_hbm.at[pl.ds(base + g * ch, ch)], wsem.at[b])

            gather(0, 0).start()

            @pl.loop(0, n_ch, step=2)
            def _(g0, gather=gather, put=put, n_ch=n_ch):
                for b in range(2):
                    g = g0 + b

                    @pl.when(g + 1 < n_ch)
                    def _():
                        @pl.when(g >= 1)
                        def _():
                            put(g - 1, 1 - b).wait()
                        gather(g + 1, 1 - b).start()

                    gather(g, b).wait()
                    put(g, b).start()

            for g in (n_ch - 2, n_ch - 1):
                put(g, g % 2).wait()

    call = pl.kernel(
        body,
        out_type=tuple(jax.ShapeDtypeStruct((s.size, d), dtype) for s in slots),
        mesh=plsc.VectorSubcoreMesh(core_axis_name="c", subcore_axis_name="s"),
        scratch_types=[pltpu.VMEM((per_w,), jnp.int32) for (per_w, _, _, _) in plans]
        + [pltpu.VMEM((2, ch, d), dtype) for (_, ch, _, _) in plans]
        + [pltpu.SemaphoreType.DMA((2,)), pltpu.SemaphoreType.DMA((2,))],
        name="sc_collect",
    )
    out = call(ys, *[s.reshape(s.size) for s in slots])
    return out if isinstance(out, (tuple, list)) else (out,)


def _combine_kernel(x_ref, mod_ref, y_ref, tg_ref, fg_ref, *rest):
    o_ref = rest[-1]
    bt, t, d = x_ref.shape
    m = bt * t
    g2 = mod_ref[:, :, 5 * d:6 * d]
    g2 = g2.reshape(1, d) if bt == 1 else jnp.broadcast_to(g2, (bt, t, d)).reshape(m, d)
    out = x_ref[...].reshape(m, d) + g2 * _moe_mix(y_ref, tg_ref[...])
    o_ref[...] = _rms(out, fg_ref[...]).reshape(bt, t, d)


def _combine_call(x_new, b0, mod, y4, tg, final_g, bt, t, out_batch, out_b0, out_prev=None):
    _, s, d = x_new.shape
    b = mod.shape[0]
    m = bt * t
    steps = s // t
    in_specs = [pl.BlockSpec((bt, t, d), lambda bi, j: (bi + b0 // bt, j, 0)),
                pl.BlockSpec((bt, 1, 6 * d), lambda bi, j: (bi, 0, 0)),
                pl.BlockSpec((TOP_K, m, d // 2), lambda bi, j: (0, bi * steps + j, 0)),
                pl.BlockSpec((m, TOPK_LANES), lambda bi, j: (bi * steps + j, 0)),
                pl.BlockSpec((1, d), lambda bi, j: (0, 0))]
    operands = [x_new, mod, y4, tg, final_g]
    aliases = {}
    if out_prev is not None:
        in_specs.append(pl.BlockSpec(memory_space=pl.ANY))
        operands.append(out_prev)
        aliases = {len(operands) - 1: 0}
    return pl.pallas_call(
        _combine_kernel,
        grid=(b // bt, steps),
        in_specs=in_specs,
        out_specs=pl.BlockSpec((bt, t, d), lambda bi, j: (bi + out_b0 // bt, j, 0)),
        out_shape=jax.ShapeDtypeStruct((out_batch, s, d), F32),
        input_output_aliases=aliases,
        compiler_params=pltpu.CompilerParams(dimension_semantics=("arbitrary", "arbitrary")),
        name=f"combine_t{t}",
    )(*operands)


def _layer_weights(l, p, cfgs):
    d = p["w_out"].shape[-1]
    row = lambda v: v[l].reshape(1, d)
    wr = p["w_router"][l]
    wr_pad = jnp.zeros((d, ROUTER_LANES), F32).at[:, :N_EXPERTS].set(wr)
    wr_hi = wr_pad.astype(BF16)
    wr_lo = (wr_pad - wr_hi.astype(F32)).astype(BF16)
    b_router = jnp.full((1, ROUTER_LANES), -1e30, F32).at[0, :N_EXPERTS].set(p["b_router"][l])
    mask = jnp.tril(jnp.ones((GMLP_CHUNK, GMLP_CHUNK), dtype=bool))
    w_sp_full = jnp.where(mask[None], p["w_spatial"][l], 0)
    hd = d // GMLP_HEADS
    base = dict(norm_mix_g=row(p["norm_mix_g"]), norm_ffn_g=row(p["norm_ffn_g"]), w_in=p["w_in"][l].astype(BF16),
                ln_v_g=row(p["ln_v_g"]), ln_v_b=row(p["ln_v_b"]), w_a_out=p["w_a_out"][l].astype(BF16),
                w_pool=p["w_pool"][l].astype(BF16), b_pool=row(p["b_pool"]), pool_scale=row(p["pool_scale"]),
                w_dw=jnp.zeros((_round_up(CONV_K, SUBLANES), d), F32).at[:CONV_K].set(p["w_dw"][l]),
                b_dw=row(p["b_dw"]), ln_c_g=row(p["ln_c_g"]), ln_c_b=row(p["ln_c_b"]),
                w_c_out=p["w_c_out"][l].astype(BF16), w_out=p["w_out"][l].astype(BF16),
                wr_hi=wr_hi, wr_lo=wr_lo, b_router=b_router)
    out = []
    for cfg in cfgs:
        w_sp = jnp.zeros((GMLP_HEADS, cfg.ch, cfg.kg), F32).at[:, :, :cfg.ch].set(w_sp_full[:, :cfg.ch, :cfg.ch])
        bs_full = jnp.repeat(p["b_spatial"][l][:, :cfg.ch].T, hd, axis=1)
        m = cfg.bt * cfg.t
        ltri = (jnp.arange(m)[:, None] > jnp.arange(m)[None, :]).astype(BF16)
        out.append(dict(base, w_sp=w_sp.astype(BF16), bs_full=bs_full, pmat=_pool_band(cfg.t, cfg.kp), ltri=ltri))
    return out


def _pad_front(state, rows):
    pad = rows - state.shape[-2]
    return jnp.pad(state, ((0, 0),) * (state.ndim - 2) + ((pad, 0), (0, 0)))


def kernel(x_prompt, x_sample, c_prompt, c_sample, state_pool, state_conv, norm_mix_g, norm_ffn_g, w_ada, b_ada, w_in, ln_v_g, ln_v_b, w_spatial, b_spatial, w_a_out, w_pool, b_pool, pool_scale, w_dw, b_dw, ln_c_g, ln_c_b, w_c_out, w_out, w_router, b_router, w_gate_up, b_gate_up, w_down, b_down, final_norm_g):
    p = dict(norm_mix_g=norm_mix_g, norm_ffn_g=norm_ffn_g, w_in=w_in, ln_v_g=ln_v_g, ln_v_b=ln_v_b,
             w_spatial=w_spatial, b_spatial=b_spatial, w_a_out=w_a_out, w_pool=w_pool, b_pool=b_pool,
             pool_scale=pool_scale, w_dw=w_dw, b_dw=b_dw, ln_c_g=ln_c_g, ln_c_b=ln_c_b, w_c_out=w_c_out,
             w_out=w_out, w_router=w_router, b_router=b_router)
    n_layers = w_in.shape[0]
    bp, sp, d = x_prompt.shape
    bs, ss, _ = x_sample.shape
    tp = min(PROMPT_TILE, sp)
    bts = min(SAMPLE_BT, bs)
    cfg_p = MixerCfg(bt=1, t=tp, ch=GMLP_CHUNK, kg=GMLP_CHUNK, kp=_round_up(POOL_HIST + tp, LANES), d=d,
                     start_pos=0, emit_v=False, fuse_in=False, steps=sp // tp, n_tiles=bp * (sp // tp))
    cfg_s = MixerCfg(bt=bts, t=ss, ch=ss, kg=_round_up(ss, LANES), kp=_round_up(POOL_HIST + ss, LANES), d=d,
                     start_pos=PAST_LEN, emit_v=True, fuse_in=False, steps=1, n_tiles=bs // bts)
    n_p, n_s = bp * sp, bs * ss
    n_slots = (n_p + n_s) * TOP_K + N_EXPERTS * MOE_TM

    mod_all = _ada_call(jnp.concatenate([c_prompt, c_sample], axis=0), w_ada, b_ada)
    mod_p = mod_all[:, :bp].reshape(n_layers, bp, 1, 6 * d)
    mod_s = mod_all[:, bp:].reshape(n_layers, bs, 1, 6 * d)
    final_g = final_norm_g.reshape(1, d)

    hb = bp // 2
    halves = [(0, hb), (hb, bp - hb)]
    steps_p = sp // tp
    xs = x_sample
    parts = [dict(lo=0, n=bp, x=x_prompt)]
    zero_pool = jnp.zeros((bp, POOL_HIST, d), F32)
    zero_conv = jnp.zeros((bp, CONV_HIST, d), F32)
    zero_cnt = jnp.zeros((1, ROUTER_LANES), F32)
    fuse_p, fuse_s = {}, ()
    pools_p, convs_p, pools_s, convs_s, vs = [], [], [], [], []

    def holder(lo):
        return next(pp for pp in parts if pp["lo"] <= lo < pp["lo"] + pp["n"])

    for l in range(n_layers):
        fused = l > 0
        lw_p, lw_s = _layer_weights(l, p, (cfg_p, cfg_s))
        cnt = zero_cnt
        new_parts = []
        for lo, n in (halves if fused else [(0, bp)]):
            src = holder(lo)
            x_new, h2, ti, tg, rk, cnt, npool, nconv = _mixer_call(
                cfg_p._replace(fuse_in=fused, n_tiles=n * steps_p), src["x"], lo - src["lo"], mod_p[l, lo:lo + n],
                zero_pool[lo:lo + n], zero_conv[lo:lo + n], lw_p, cnt[0:1], fuse_p.get((lo, n), ()))
            new_parts.append(dict(lo=lo, n=n, x=x_new, h2=h2.reshape(n * sp, d // 2), ti=ti, tg=tg, rk=rk,
                                  npool=npool, nconv=nconv))
        parts = new_parts
        xs, h2s, tis, tgs, rks, cnt_all, npool_s, nconv_s, v_s = _mixer_call(
            cfg_s._replace(fuse_in=fused), xs, 0, mod_s[l], _pad_front(state_pool[l], POOL_HIST),
            _pad_front(state_conv[l], CONV_HIST), lw_s, cnt[0:1], fuse_s)
        top_i = jnp.concatenate([pp["ti"] for pp in parts] + [tis], axis=0)[:, :TOP_K]
        rank = jnp.concatenate([pp["rk"] for pp in parts] + [rks], axis=0)[:, :TOP_K]
        tg_p = jnp.concatenate([pp["tg"] for pp in parts], axis=0)
        slot_of, block_e, nused = _route(top_i, rank, cnt_all[0, :N_EXPERTS].astype(jnp.int32))
        tables = [pp["h2"] for pp in parts] + [h2s.reshape(n_s, d // 2)]
        bounds = [pp["lo"] * sp for pp in parts] + [n_p, n_p + n_s]
        slots = [slot_of[:, bounds[i]:bounds[i + 1]] for i in range(len(tables))]
        x_sorted = _sc_dispatch(tables, slots, n_slots)
        ys = _moe_call(x_sorted, block_e, nused, l, w_gate_up, b_gate_up, w_down, b_down)
        (lo_a, n_a), (lo_b, n_b) = halves
        (y4a,) = _sc_collect(ys, [slot_of[:, lo_a * sp:(lo_a + n_a) * sp]])
        y4b, y4s = _sc_collect(ys, [slot_of[:, lo_b * sp:(lo_b + n_b) * sp], slot_of[:, n_p:]])
        fuse_p = {(lo, n): (y4.reshape(TOP_K, n * sp, d // 2), tg_p[lo * sp:(lo + n) * sp], mod_p[l, lo:lo + n])
                  for (lo, n), y4 in zip(halves, (y4a, y4b))}
        fuse_s = (y4s.reshape(TOP_K, n_s, d // 2), tgs, mod_s[l])
        pools_p.append(jnp.concatenate([pp["npool"] for pp in parts], axis=0))
        convs_p.append(jnp.concatenate([pp["nconv"] for pp in parts], axis=0))
        pools_s.append(npool_s)
        convs_s.append(nconv_s)
        vs.append(v_s)
    yp = None
    for lo, n in halves:
        src = holder(lo)
        y4, tg, mod = fuse_p[(lo, n)]
        yp = _combine_call(src["x"], lo - src["lo"], mod, y4, tg, final_g, 1, tp, bp, lo, yp)
    ys_out = _combine_call(xs, 0, mod_s[-1], *fuse_s[:2], final_g, bts, ss, bs, 0)
    return (yp, ys_out, jnp.stack(pools_p), jnp.stack(convs_p), jnp.stack(pools_s), jnp.stack(convs_s),
            jnp.stack(vs))
```

```python
import functools
from typing import NamedTuple

import jax
import jax.numpy as jnp
from jax import lax
from jax.experimental import pallas as pl
from jax.experimental.pallas import tpu as pltpu
from jax.experimental.pallas import tpu_sc as plsc

GMLP_CHUNK = 128
GMLP_HEADS = 4
POOL_WINDOWS = (2, 4, 8, 16)
POOL_STATE = max(POOL_WINDOWS) - 1
CONV_K = 31
CONV_STATE = CONV_K - 1
N_BRANCH = 3
N_EXPERTS = 32
TOP_K = 4
SWIGLU_LIMIT = 7.0
SWIGLU_ALPHA = 1.702
EPS = 1e-6
PAST_LEN = 2048

LANES = 128
SUBLANES = 8
V7X_VMEM_LIMIT_BYTES = 56 * 2**20

POOL_HIST = 16
CONV_HIST = 32
ROUTER_LANES = LANES
TOPK_LANES = 8
MOE_TM = 512
PROMPT_TILE = 256
SAMPLE_BT = 8

F32 = jnp.float32
BF16 = jnp.bfloat16


def _round_up(a, m):
    return (a + m - 1) // m * m


class MixerCfg(NamedTuple):
    bt: int
    t: int
    ch: int
    kg: int
    kp: int
    d: int
    start_pos: int
    emit_v: bool
    fuse_in: bool
    steps: int
    n_tiles: int


def _rms(x, g):
    return x * lax.rsqrt(jnp.mean(x * x, axis=-1, keepdims=True) + EPS) * g


def _ln(x, g, b):
    mu = jnp.mean(x, axis=-1, keepdims=True)
    xc = x - mu
    var = jnp.mean(xc * xc, axis=-1, keepdims=True)
    return xc * lax.rsqrt(var + EPS) * g + b


def _dot(a, b):
    return jnp.dot(a, b, preferred_element_type=F32)


def _pack_rows(x):
    half = x.shape[1] // 2
    hi = lax.bitcast_convert_type(x[:, :half].astype(BF16).astype(F32), jnp.int32)
    lo = lax.bitcast_convert_type(x[:, half:].astype(BF16).astype(F32), jnp.int32)
    return hi | lax.shift_right_logical(lo, 16)


def _unpack_rows(w):
    hi = lax.bitcast_convert_type(w & jnp.int32(-65536), F32)
    lo = lax.bitcast_convert_type(lax.shift_left(w, 16), F32)
    return jnp.concatenate([hi, lo], axis=1)


def _moe_mix(y4_ref, tg):
    f = tg[:, 0:1] * _unpack_rows(y4_ref[0])
    for k in range(1, TOP_K):
        f = f + tg[:, k:k + 1] * _unpack_rows(y4_ref[k])
    return f


def _mixer_kernel(cfg, x_ref, mod_ref, modb_ref, pp_ref, pc_ref, nmg_ref, nfg_ref, win_ref, lnvg_ref, lnvb_ref,
                  wsp_ref, bsf_ref, wao_ref, wpool_ref, bpool_ref, pscale_ref, wdw_ref, bdw_ref,
                  lncg_ref, lncb_ref, wco_ref, wout_ref, wrh_ref, wrl_ref, br_ref, pmat_ref, ltri_ref, cnt0_ref,
                  *rest):
    rest = list(rest)
    if cfg.fuse_in:
        y4_ref, tgp_ref, modp_ref = rest[:3]
        rest = rest[3:]
    xo_ref, h2_ref, ti_ref, tg_ref, rk_ref, cnt_ref, npool_ref, nconv_ref = rest[:8]
    rest = rest[8:]
    if cfg.emit_v:
        v_ref, hp_scr, xc_scr, run_scr, mg_scr, xr_scr = rest
    else:
        hp_scr, xc_scr, run_scr, mg_scr, xr_scr = rest
    bt, t, d = cfg.bt, cfg.t, cfg.d
    m = bt * t
    n_slab = d // LANES
    gd = d // len(POOL_WINDOWS)
    hd = d // GMLP_HEADS
    q = pl.program_id(0)
    j = 0 if cfg.steps == 1 else lax.rem(jnp.minimum(q, cfg.n_tiles - 1), cfg.steps)

    def mod_rows(k, ref=mod_ref):
        r = ref[:, :, k * d:(k + 1) * d]
        if bt == 1:
            return r.reshape(1, d)
        return jnp.broadcast_to(r, (bt, t, d)).reshape(m, d)

    @pl.when(q == 0)
    def _init_carries():
        run_scr[...] = cnt0_ref[...]
        mg_scr[...] = jnp.zeros_like(mg_scr)
        xr_scr[...] = jnp.zeros_like(xr_scr)

    @pl.when(j == 0)
    def _load_state():
        hp_scr[...] = pp_ref[...]
        for bi in range(bt):
            for c in range(n_slab):
                xc_scr[bi, c, 0:CONV_HIST, :] = pc_ref[bi, :, c * LANES:(c + 1) * LANES]

    x_prev = xr_scr[...]
    out_prev = _dot(mg_scr[...], wout_ref[...])

    x = x_ref[...].reshape(m, d)
    if cfg.fuse_in:
        x = x + mod_rows(5, modp_ref) * _moe_mix(y4_ref, tgp_ref[...])
    sh1, sc1 = mod_rows(0), mod_rows(1)
    h = (_rms(x, nmg_ref[...]) * (1.0 + sc1) + sh1).astype(BF16)

    def zcols(lo, hi):
        return _dot(h, win_ref[:, lo:hi])

    off_u, off_v, off_b, off_c, off_g = 0, d, 2 * d, 3 * d, 5 * d
    zc_a = zcols(off_c, off_c + d)
    zc_b = zcols(off_c + d, off_c + 2 * d)

    x_new = x_prev + mod_rows(2, modb_ref) * out_prev
    xo_ref[...] = x_new.reshape(bt, t, d)
    h2 = _rms(x_new, nfg_ref[...]) * (1.0 + mod_rows(4, modb_ref)) + mod_rows(3, modb_ref)
    h2_ref[...] = _pack_rows(h2).reshape(bt, t, d // 2)
    h2_hi = h2.astype(BF16)
    h2_lo = (h2 - h2_hi.astype(F32)).astype(BF16)
    logits = _dot(h2_hi, wrh_ref[...]) + _dot(h2_lo, wrh_ref[...]) + _dot(h2_hi, wrl_ref[...]) + br_ref[...]

    x_c = zc_a * jax.nn.sigmoid(zc_b)
    for bi in range(bt):
        for c in range(n_slab):
            xc_scr[bi, c, CONV_HIST:CONV_HIST + t, :] = x_c[bi * t:(bi + 1) * t, c * LANES:(c + 1) * LANES]

    lane_r = lax.broadcasted_iota(jnp.int32, (m, ROUTER_LANES), 1).astype(F32)
    lane_k = lax.broadcasted_iota(jnp.int32, (m, TOPK_LANES), 1)
    work = logits
    top_vals, onehots = [], []
    ti = jnp.zeros((m, TOPK_LANES), jnp.int32)
    for r in range(TOP_K):
        mx = jnp.max(work, axis=-1, keepdims=True)
        idx = jnp.min(jnp.where(work == mx, lane_r, float(ROUTER_LANES)), axis=-1, keepdims=True)
        ti = jnp.where(lane_k == r, idx.astype(jnp.int32), ti)
        top_vals.append(mx)
        work = jnp.where(lane_r == idx, -jnp.inf, work)
        onehots.append((lane_r == idx).astype(F32))
    exps = [jnp.exp(tv - top_vals[0]) for tv in top_vals]
    denom = exps[0] + exps[1] + exps[2] + exps[3]
    tg = jnp.zeros((m, TOPK_LANES), F32)
    for r in range(TOP_K):
        tg = jnp.where(lane_k == r, exps[r] / denom, tg)
    ti_ref[...] = ti
    tg_ref[...] = tg
    proj_cols = dict(zb=off_b, zu=off_u, zv=off_v, ga=off_g, gb=off_g + d, gc=off_g + 2 * d)
    pending_proj = list(proj_cols)
    proj = {}

    def next_proj():
        if pending_proj:
            name = pending_proj.pop(0)
            proj[name] = zcols(proj_cols[name], proj_cols[name] + d)

    conv_parts = []
    for bi in range(bt):
        accs = []
        for c in range(n_slab):
            cs = slice(c * LANES, (c + 1) * LANES)
            acc = jnp.broadcast_to(bdw_ref[:, cs], (t, LANES))
            for k in range(CONV_K):
                acc = acc + wdw_ref[k:k + 1, cs] * xc_scr[bi, c, pl.ds(CONV_HIST - CONV_STATE + k, t), :]
            accs.append(acc)
            tail = xc_scr[bi, c, t:t + CONV_HIST, :]
            nconv_ref[bi, :, cs] = xc_scr[bi, c, pl.ds(t + CONV_HIST - CONV_STATE, CONV_STATE), :]
            xc_scr[bi, c, 0:CONV_HIST, :] = tail
            next_proj()
        conv_parts.append(jnp.concatenate(accs, axis=1))
    while pending_proj:
        next_proj()
    conv = conv_parts[0] if bt == 1 else jnp.concatenate(conv_parts, axis=0)

    base = run_scr[...]
    rk = jnp.zeros((m, TOPK_LANES), jnp.int32)
    new_base = base
    for r in range(TOP_K):
        before = _dot(ltri_ref[...], onehots[r].astype(BF16)) + new_base
        rank = jnp.sum(onehots[r] * before, axis=-1, keepdims=True)
        rk = jnp.where(lane_k == r, rank.astype(jnp.int32), rk)
        new_base = new_base + jnp.sum(onehots[r], axis=0, keepdims=True)
    new_base = jnp.where(q >= 1, new_base, base)
    run_scr[...] = new_base
    rk_ref[...] = rk
    cnt_ref[...] = jnp.broadcast_to(new_base, cnt_ref.shape)

    zb = proj["zb"]
    lane = lax.broadcasted_iota(jnp.int32, (1, d), 1)
    win_lane = jnp.left_shift(2, lane // gd).astype(F32)
    pos1 = (cfg.start_pos + 1 + j * t + lax.broadcasted_iota(jnp.int32, (t, 1), 0)).astype(F32)
    cnt = jnp.minimum(pos1, win_lane)
    pooled_parts = []
    for bi in range(bt):
        zb_b = zb[bi * t:(bi + 1) * t]
        pieces = [hp_scr[bi], zb_b]
        if cfg.kp > POOL_HIST + t:
            pieces = [jnp.zeros((cfg.kp - POOL_HIST - t, d), F32)] + pieces
        full = jnp.concatenate(pieces, axis=0)
        fullb = full.astype(BF16)
        sums = jnp.concatenate(
            [_dot(pmat_ref[g], fullb[:, g * gd:(g + 1) * gd]) for g in range(len(POOL_WINDOWS))], axis=1)
        pooled_parts.append(sums / cnt - zb_b)
        hp_scr[bi] = full[cfg.kp - POOL_HIST:]
        npool_ref[bi] = hp_scr[bi, pl.ds(POOL_HIST - POOL_STATE, POOL_STATE), :]
    pooled = (pooled_parts[0] if bt == 1 else jnp.concatenate(pooled_parts, axis=0)).astype(BF16)
    y_b = jnp.concatenate(
        [_dot(pooled[:, g * gd:(g + 1) * gd], wpool_ref[g]) for g in range(len(POOL_WINDOWS))], axis=1)
    y_b = (y_b + bpool_ref[...]) * pscale_ref[...]

    v = _ln(jax.nn.gelu(proj["zv"]), lnvg_ref[...], lnvb_ref[...])
    if cfg.emit_v:
        v_ref[...] = v.reshape(bt, t, d)
    vb = v.astype(BF16)
    mixed_rows = []
    for r0 in range(0, m, cfg.ch):
        vc = vb[r0:r0 + cfg.ch]
        if cfg.kg > cfg.ch:
            vc = jnp.concatenate([vc, jnp.zeros((cfg.kg - cfg.ch, d), BF16)], axis=0)
        mixed_rows.append(jnp.concatenate(
            [_dot(wsp_ref[hh], vc[:, hh * hd:(hh + 1) * hd]) for hh in range(GMLP_HEADS)], axis=1) + bsf_ref[...])
    mixed = mixed_rows[0] if len(mixed_rows) == 1 else jnp.concatenate(mixed_rows, axis=0)

    y_c = _dot(jax.nn.silu(_ln(conv, lncg_ref[...], lncb_ref[...])).astype(BF16), wco_ref[...])
    u = jax.nn.gelu(proj["zu"])
    y_a = _dot((u * mixed).astype(BF16), wao_ref[...])
    merged = jax.nn.sigmoid(proj["gb"]) * y_b
    merged = merged + jax.nn.sigmoid(proj["ga"]) * y_a
    merged = merged + jax.nn.sigmoid(proj["gc"]) * y_c

    mg_scr[...] = merged.astype(BF16)
    xr_scr[...] = x


def _const_spec(shape):
    nd = len(shape)
    return pl.BlockSpec(shape, lambda q, _nd=nd: (0,) * _nd, pipeline_mode=pl.Buffered(1))


def _pool_band(t, kp):
    col = jnp.arange(kp)[None, :]
    end = (kp - t) + jnp.arange(t)[:, None]
    return jnp.stack([((col <= end) & (col > end - w)) for w in POOL_WINDOWS]).astype(BF16)


def _mixer_call(cfg, x, b0, mod, prev_pool, prev_conv, lw, cnt0, fuse=()):
    _, s, d = x.shape
    b = mod.shape[0]
    bt, t = cfg.bt, cfg.t
    m = bt * t
    steps, n_tiles = cfg.steps, cfg.n_tiles
    assert steps == s // t and n_tiles == (b // bt) * steps and b0 % bt == 0
    assert steps == 1 or t >= max(CONV_STATE, POOL_STATE)
    n_slab = d // LANES

    def front(q):
        return jnp.minimum(q, n_tiles - 1)

    def back(q):
        return jnp.maximum(q - 1, 0)

    def tile_spec(which, width=d):
        return pl.BlockSpec((bt, t, width), lambda q: (which(q) // steps, which(q) % steps, 0))

    def batch_spec(which, rows, width):
        return pl.BlockSpec((bt, rows, width), lambda q: (which(q) // steps, 0, 0))

    def rows_spec(which, width):
        return pl.BlockSpec((m, width), lambda q: (which(q), 0))

    consts = [lw["norm_mix_g"], lw["norm_ffn_g"], lw["w_in"], lw["ln_v_g"], lw["ln_v_b"], lw["w_sp"], lw["bs_full"],
              lw["w_a_out"], lw["w_pool"], lw["b_pool"], lw["pool_scale"], lw["w_dw"], lw["b_dw"], lw["ln_c_g"],
              lw["ln_c_b"], lw["w_c_out"], lw["w_out"], lw["wr_hi"], lw["wr_lo"], lw["b_router"], lw["pmat"],
              lw["ltri"], cnt0]
    x_spec = pl.BlockSpec((bt, t, d), lambda q: (front(q) // steps + b0 // bt, front(q) % steps, 0))
    in_specs = [x_spec, batch_spec(front, 1, 6 * d), batch_spec(back, 1, 6 * d),
                batch_spec(front, POOL_HIST, d), batch_spec(front, CONV_HIST, d)]
    in_specs += [_const_spec(c.shape) for c in consts]
    if cfg.fuse_in:
        in_specs += [pl.BlockSpec((TOP_K, m, d // 2), lambda q: (0, front(q), 0)),
                     rows_spec(front, TOPK_LANES), batch_spec(front, 1, 6 * d)]
    out_shape = [jax.ShapeDtypeStruct((b, s, d), F32), jax.ShapeDtypeStruct((b, s, d // 2), jnp.int32),
                 jax.ShapeDtypeStruct((b * s, TOPK_LANES), jnp.int32), jax.ShapeDtypeStruct((b * s, TOPK_LANES), F32),
                 jax.ShapeDtypeStruct((b * s, TOPK_LANES), jnp.int32),
                 jax.ShapeDtypeStruct((SUBLANES, ROUTER_LANES), F32),
                 jax.ShapeDtypeStruct((b, POOL_STATE, d), F32), jax.ShapeDtypeStruct((b, CONV_STATE, d), F32)]
    out_specs = [tile_spec(back), tile_spec(back, d // 2),
                 rows_spec(back, TOPK_LANES), rows_spec(back, TOPK_LANES), rows_spec(back, TOPK_LANES),
                 pl.BlockSpec((SUBLANES, ROUTER_LANES), lambda q: (0, 0)),
                 batch_spec(front, POOL_STATE, d), batch_spec(front, CONV_STATE, d)]
    if cfg.emit_v:
        out_shape.append(jax.ShapeDtypeStruct((b, s, d), F32))
        out_specs.append(tile_spec(front))
    return pl.pallas_call(
        functools.partial(_mixer_kernel, cfg),
        grid=(n_tiles + 1,),
        in_specs=in_specs,
        out_specs=out_specs,
        out_shape=out_shape,
        scratch_shapes=[pltpu.VMEM((bt, POOL_HIST, d), F32),
                        pltpu.VMEM((bt, n_slab, CONV_HIST + t, LANES), F32),
                        pltpu.VMEM((1, ROUTER_LANES), F32),
                        pltpu.VMEM((m, d), BF16), pltpu.VMEM((m, d), F32)],
        compiler_params=pltpu.CompilerParams(dimension_semantics=("arbitrary",),
                                             vmem_limit_bytes=V7X_VMEM_LIMIT_BYTES),
        name=f"mixer_t{t}",
    )(x, mod, mod, prev_pool, prev_conv, *consts, *fuse)


def _ada_kernel(c_ref, w_ref, b_ref, o_ref):
    o_ref[0] = _dot(jax.nn.silu(c_ref[...]).astype(BF16), w_ref[0].astype(BF16)) + b_ref[0]


def _ada_call(c_all, w_ada, b_ada):
    n_layers, d, six_d = w_ada.shape
    rows = c_all.shape[0]
    bn = six_d // 6
    return pl.pallas_call(
        _ada_kernel,
        grid=(n_layers, six_d // bn),
        in_specs=[pl.BlockSpec((rows, d), lambda l, n: (0, 0)),
                  pl.BlockSpec((1, d, bn), lambda l, n: (l, 0, n)),
                  pl.BlockSpec((1, 1, bn), lambda l, n: (l, 0, n))],
        out_specs=pl.BlockSpec((1, rows, bn), lambda l, n: (l, 0, n)),
        out_shape=jax.ShapeDtypeStruct((n_layers, rows, six_d), F32),
        compiler_params=pltpu.CompilerParams(dimension_semantics=("arbitrary", "arbitrary")),
        name="adaln",
    )(c_all, w_ada, b_ada.reshape(n_layers, 1, six_d))


def _moe_kernel(be_ref, nused_ref, x_ref, wgu_ref, bgu_ref, wdn_ref, bdn_ref, y_ref, wgu_bf, wdn_bf):
    i = pl.program_id(0)
    nused = nused_ref[0]
    dff = wdn_bf.shape[0]

    @pl.when(i < nused)
    def _body():
        changed = jnp.logical_or(i == 0, be_ref[i] != be_ref[jnp.maximum(i - 1, 0)])

        @pl.when(changed)
        def _cast_weights():
            wgu_bf[...] = wgu_ref[0].astype(BF16)
            wdn_bf[...] = wdn_ref[0].astype(BF16)

        gu = _dot(_unpack_rows(x_ref[...]).astype(BF16), wgu_bf[...]) + bgu_ref[0]
        g = jnp.minimum(gu[:, :dff], SWIGLU_LIMIT)
        u = jnp.clip(gu[:, dff:], -SWIGLU_LIMIT, SWIGLU_LIMIT)
        act = (u + 1.0) * g * jax.nn.sigmoid(SWIGLU_ALPHA * g)
        y_ref[...] = _pack_rows(_dot(act.astype(BF16), wdn_bf[...]) + bdn_ref[0])

    @pl.when(i >= nused)
    def _unused_block():
        y_ref[...] = jnp.zeros_like(y_ref)


def _moe_call(xs, block_e, nused, layer, w_gu_all, b_gu_all, w_dn_all, b_dn_all):
    n_blocks = block_e.shape[0]
    n_layers, n_exp, d, two_f = w_gu_all.shape
    dff = two_f // 2
    tm = MOE_TM
    e0 = layer * n_exp
    w_gu = w_gu_all.reshape(n_layers * n_exp, d, two_f)
    b_gu = b_gu_all.reshape(n_layers * n_exp, 1, two_f)
    w_dn = w_dn_all.reshape(n_layers * n_exp, dff, d)
    b_dn = b_dn_all.reshape(n_layers * n_exp, 1, d)
    grid_spec = pltpu.PrefetchScalarGridSpec(
        num_scalar_prefetch=2,
        grid=(n_blocks,),
        in_specs=[
            pl.BlockSpec((tm, d // 2), lambda i, be, nu: (i, 0)),
            pl.BlockSpec((1, d, two_f), lambda i, be, nu: (e0 + be[i], 0, 0)),
            pl.BlockSpec((1, 1, two_f), lambda i, be, nu: (e0 + be[i], 0, 0)),
            pl.BlockSpec((1, dff, d), lambda i, be, nu: (e0 + be[i], 0, 0)),
            pl.BlockSpec((1, 1, d), lambda i, be, nu: (e0 + be[i], 0, 0)),
        ],
        out_specs=pl.BlockSpec((tm, d // 2), lambda i, be, nu: (i, 0)),
        scratch_shapes=[pltpu.VMEM((d, two_f), BF16), pltpu.VMEM((dff, d), BF16)],
    )
    return pl.pallas_call(
        _moe_kernel,
        grid_spec=grid_spec,
        out_shape=jax.ShapeDtypeStruct((n_blocks * tm, d // 2), jnp.int32),
        compiler_params=pltpu.CompilerParams(dimension_semantics=("arbitrary",),
                                             vmem_limit_bytes=V7X_VMEM_LIMIT_BYTES),
        name="moe_experts",
    )(block_e, nused, xs, w_gu, b_gu, w_dn, b_dn)


def _route(top_i, rank, counts):
    tm = MOE_TM
    n_all = top_i.shape[0]
    n_blocks = n_all * TOP_K // tm + N_EXPERTS
    padded = (counts + tm - 1) // tm * tm
    pad_ends = jnp.cumsum(padded)
    pad_starts = pad_ends - padded
    blk = jnp.arange(n_blocks, dtype=jnp.int32)[:, None] * tm
    block_e = jnp.minimum(jnp.sum((pad_ends[None, :] <= blk).astype(jnp.int32), axis=1), N_EXPERTS - 1)
    nused = (pad_ends[-1] // tm).astype(jnp.int32).reshape(1)
    onehot = top_i[:, :, None] == jnp.arange(N_EXPERTS, dtype=jnp.int32)[None, None, :]
    slot_of = jnp.sum(jnp.where(onehot, pad_starts[None, None, :], 0), axis=-1) + rank
    return slot_of.T.astype(jnp.int32), block_e.astype(jnp.int32), nused


SC_CORES = 2
SC_SUBCORES = 16
SC_WORKERS = SC_CORES * SC_SUBCORES
SC_CHUNK = 64


def _sc_worker_id():
    return lax.axis_index("s") * SC_CORES + lax.axis_index("c")


def _sc_plan(n_rows, split=False):
    per_w = n_rows // SC_WORKERS
    ch = min(SC_CHUNK, per_w // 2 if split else per_w)
    n_ch = per_w // ch
    assert per_w * SC_WORKERS == n_rows and n_ch * ch == per_w
    return per_w, ch, n_ch, (n_ch >= 2 and n_ch % 2 == 0)


def _sc_dispatch(tables, slots, n_slots):
    nt = len(tables)
    d, dtype = tables[0].shape[1], tables[0].dtype
    plans = [_sc_plan(h.shape[0]) for h in tables]

    buf_shapes = sorted({(2 if piped else 1, ch) for (_, ch, _, piped) in plans})

    def body(*refs):
        h_refs, idx_refs, out_hbm = refs[:nt], refs[nt:2 * nt], refs[2 * nt]
        scratch = refs[2 * nt + 1:]
        idx_vs, shared, (rsem, wsem) = scratch[:nt], scratch[nt:-2], scratch[-2:]
        bufs = [shared[buf_shapes.index((2 if piped else 1, ch))] for (_, ch, _, piped) in plans]
        wid = _sc_worker_id()
        for h_hbm, idx_hbm, idx_v, buf, (tpw, ch, n_ch, piped) in zip(h_refs, idx_refs, idx_vs, bufs, plans):
            pltpu.sync_copy(idx_hbm.at[wid], idx_v)

            def read(c, b, h_hbm=h_hbm, buf=buf, tpw=tpw, ch=ch):
                return pltpu.make_async_copy(h_hbm.at[pl.ds(wid * tpw + c * ch, ch)], buf.at[b], rsem.at[b])

            def write(c, k, b, buf=buf, idx_v=idx_v):
                return pltpu.make_async_copy(buf.at[b], out_hbm.at[idx_v.at[c * TOP_K + k]], wsem.at[b])

            if piped:
                read(0, 0).start()

                @pl.loop(0, n_ch, step=2)
                def _(c0, read=read, write=write, n_ch=n_ch):
                    for b in range(2):
                        c = c0 + b

                        @pl.when(c + 1 < n_ch)
                        def _():
                            @pl.when(c >= 1)
                            def _():
                                for k in range(TOP_K):
                                    write(c - 1, k, 1 - b).wait()
                            read(c + 1, 1 - b).start()

                        read(c, b).wait()
                        for k in range(TOP_K):
                            write(c, k, b).start()

                for c in (n_ch - 2, n_ch - 1):
                    for k in range(TOP_K):
                        write(c, k, c % 2).wait()
            else:
                for c in range(n_ch):
                    cp_in = read(c, 0)
                    cp_in.start()
                    cp_in.wait()
                    for k in range(TOP_K):
                        write(c, k, 0).start()
                    for k in range(TOP_K):
                        write(c, k, 0).wait()

    scratch_types = [pltpu.VMEM((n_ch * TOP_K, ch), jnp.int32) for (_, ch, n_ch, _) in plans]
    scratch_types += [pltpu.VMEM((nbuf, ch, d), dtype) for (nbuf, ch) in buf_shapes]
    scratch_types += [pltpu.SemaphoreType.DMA((2,)), pltpu.SemaphoreType.DMA((2,))]
    call = pl.kernel(
        body,
        out_type=jax.ShapeDtypeStruct((n_slots, d), dtype),
        mesh=plsc.VectorSubcoreMesh(core_axis_name="c", subcore_axis_name="s"),
        scratch_types=scratch_types,
        name="sc_dispatch",
    )
    idx = [s.reshape(TOP_K, SC_WORKERS, n_ch, ch).transpose(1, 2, 0, 3).reshape(SC_WORKERS, n_ch * TOP_K, ch)
           for s, (_, ch, n_ch, _) in zip(slots, plans)]
    return call(*tables, *idx)


def _sc_collect(ys, slots):
    nt = len(slots)
    d, dtype = ys.shape[1], ys.dtype
    plans = [_sc_plan(s.size, split=True) for s in slots]
    assert all(piped for (_, _, _, piped) in plans)

    def body(*refs):
        ys_hbm, idx_refs, out_refs = refs[0], refs[1:1 + nt], refs[1 + nt:1 + 2 * nt]
        scratch = refs[1 + 2 * nt:]
        idx_vs, bufs, (gsem, wsem) = scratch[:nt], scratch[nt:2 * nt], scratch[2 * nt:]
        wid = _sc_worker_id()
        for idx_hbm, out_hbm, idx_v, rows_v, (per_w, ch, n_ch, _) in zip(idx_refs, out_refs, idx_vs, bufs, plans):
            base = wid * per_w
            pltpu.sync_copy(idx_hbm.at[pl.ds(base, per_w)], idx_v)

            def gather(g, b, idx_v=idx_v, rows_v=rows_v, ch=ch):
                return pltpu.make_async_copy(ys_hbm.at[idx_v.at[pl.ds(g * ch, ch)]], rows_v.at[b], gsem.at[b])

            def put(g, b, out_hbm=out_hbm, rows_v=rows_v, base=base, ch=ch):
                return pltpu.make_async_copy(rows_v.at[b], out_hbm.at[pl.ds(base + g * ch, ch)], wsem.at[b])

            gather(0, 0).start()

            @pl.loop(0, n_ch, step=2)
            def _(g0, gather=gather, put=put, n_ch=n_ch):
                for b in range(2):
                    g = g0 + b

                    @pl.when(g + 1 < n_ch)
                    def _():
                        @pl.when(g >= 1)
                        def _():
                            put(g - 1, 1 - b).wait()
                        gather(g + 1, 1 - b).start()

                    gather(g, b).wait()
                    put(g, b).start()

            for g in (n_ch - 2, n_ch - 1):
                put(g, g % 2).wait()

    call = pl.kernel(
        body,
        out_type=tuple(jax.ShapeDtypeStruct((s.size, d), dtype) for s in slots),
        mesh=plsc.VectorSubcoreMesh(core_axis_name="c", subcore_axis_name="s"),
        scratch_types=[pltpu.VMEM((per_w,), jnp.int32) for (per_w, _, _, _) in plans]
        + [pltpu.VMEM((2, ch, d), dtype) for (_, ch, _, _) in plans]
        + [pltpu.SemaphoreType.DMA((2,)), pltpu.SemaphoreType.DMA((2,))],
        name="sc_collect",
    )
    out = call(ys, *[s.reshape(s.size) for s in slots])
    return out if isinstance(out, (tuple, list)) else (out,)


def _combine_kernel(x_ref, mod_ref, y_ref, tg_ref, fg_ref, *rest):
    o_ref = rest[-1]
    bt, t, d = x_ref.shape
    m = bt * t
    g2 = mod_ref[:, :, 5 * d:6 * d]
    g2 = g2.reshape(1, d) if bt == 1 else jnp.broadcast_to(g2, (bt, t, d)).reshape(m, d)
    out = x_ref[...].reshape(m, d) + g2 * _moe_mix(y_ref, tg_ref[...])
    o_ref[...] = _rms(out, fg_ref[...]).reshape(bt, t, d)


def _combine_call(x_new, b0, mod, y4, tg, final_g, bt, t, out_batch, out_b0, out_prev=None):
    _, s, d = x_new.shape
    b = mod.shape[0]
    m = bt * t
    steps = s // t
    in_specs = [pl.BlockSpec((bt, t, d), lambda bi, j: (bi + b0 // bt, j, 0)),
                pl.BlockSpec((bt, 1, 6 * d), lambda bi, j: (bi, 0, 0)),
                pl.BlockSpec((TOP_K, m, d // 2), lambda bi, j: (0, bi * steps + j, 0)),
                pl.BlockSpec((m, TOPK_LANES), lambda bi, j: (bi * steps + j, 0)),
                pl.BlockSpec((1, d), lambda bi, j: (0, 0))]
    operands = [x_new, mod, y4, tg, final_g]
    aliases = {}
    if out_prev is not None:
        in_specs.append(pl.BlockSpec(memory_space=pl.ANY))
        operands.append(out_prev)
        aliases = {len(operands) - 1: 0}
    return pl.pallas_call(
        _combine_kernel,
        grid=(b // bt, steps),
        in_specs=in_specs,
        out_specs=pl.BlockSpec((bt, t, d), lambda bi, j: (bi + out_b0 // bt, j, 0)),
        out_shape=jax.ShapeDtypeStruct((out_batch, s, d), F32),
        input_output_aliases=aliases,
        cost_estimate=pl.CostEstimate(flops=2 * (TOP_K + 3) * b * s * d, transcendentals=b * s,
                                      bytes_accessed=b * s * d * (4 + 4 + TOP_K * 2)),
        compiler_params=pltpu.CompilerParams(dimension_semantics=("arbitrary", "arbitrary")),
        name=f"combine_t{t}",
    )(*operands)


def _layer_weights(l, p, cfgs):
    d = p["w_out"].shape[-1]
    row = lambda v: v[l].reshape(1, d)
    wr = p["w_router"][l]
    wr_pad = jnp.zeros((d, ROUTER_LANES), F32).at[:, :N_EXPERTS].set(wr)
    wr_hi = wr_pad.astype(BF16)
    wr_lo = (wr_pad - wr_hi.astype(F32)).astype(BF16)
    b_router = jnp.full((1, ROUTER_LANES), -1e30, F32).at[0, :N_EXPERTS].set(p["b_router"][l])
    mask = jnp.tril(jnp.ones((GMLP_CHUNK, GMLP_CHUNK), dtype=bool))
    w_sp_full = jnp.where(mask[None], p["w_spatial"][l], 0)
    hd = d // GMLP_HEADS
    base = dict(norm_mix_g=row(p["norm_mix_g"]), norm_ffn_g=row(p["norm_ffn_g"]), w_in=p["w_in"][l].astype(BF16),
                ln_v_g=row(p["ln_v_g"]), ln_v_b=row(p["ln_v_b"]), w_a_out=p["w_a_out"][l].astype(BF16),
                w_pool=p["w_pool"][l].astype(BF16), b_pool=row(p["b_pool"]), pool_scale=row(p["pool_scale"]),
                w_dw=jnp.zeros((_round_up(CONV_K, SUBLANES), d), F32).at[:CONV_K].set(p["w_dw"][l]),
                b_dw=row(p["b_dw"]), ln_c_g=row(p["ln_c_g"]), ln_c_b=row(p["ln_c_b"]),
                w_c_out=p["w_c_out"][l].astype(BF16), w_out=p["w_out"][l].astype(BF16),
                wr_hi=wr_hi, wr_lo=wr_lo, b_router=b_router)
    out = []
    for cfg in cfgs:
        w_sp = jnp.zeros((GMLP_HEADS, cfg.ch, cfg.kg), F32).at[:, :, :cfg.ch].set(w_sp_full[:, :cfg.ch, :cfg.ch])
        bs_full = jnp.repeat(p["b_spatial"][l][:, :cfg.ch].T, hd, axis=1)
        m = cfg.bt * cfg.t
        ltri = (jnp.arange(m)[:, None] > jnp.arange(m)[None, :]).astype(BF16)
        out.append(dict(base, w_sp=w_sp.astype(BF16), bs_full=bs_full, pmat=_pool_band(cfg.t, cfg.kp), ltri=ltri))
    return out


def _pad_front(state, rows):
    pad = rows - state.shape[-2]
    return jnp.pad(state, ((0, 0),) * (state.ndim - 2) + ((pad, 0), (0, 0)))


def kernel(x_prompt, x_sample, c_prompt, c_sample, state_pool, state_conv, norm_mix_g, norm_ffn_g, w_ada, b_ada, w_in, ln_v_g, ln_v_b, w_spatial, b_spatial, w_a_out, w_pool, b_pool, pool_scale, w_dw, b_dw, ln_c_g, ln_c_b, w_c_out, w_out, w_router, b_router, w_gate_up, b_gate_up, w_down, b_down, final_norm_g):
    p = dict(norm_mix_g=norm_mix_g, norm_ffn_g=norm_ffn_g, w_in=w_in, ln_v_g=ln_v_g, ln_v_b=ln_v_b,
             w_spatial=w_spatial, b_spatial=b_spatial, w_a_out=w_a_out, w_pool=w_pool, b_pool=b_pool,
             pool_scale=pool_scale, w_dw=w_dw, b_dw=b_dw, ln_c_g=ln_c_g, ln_c_b=ln_c_b, w_c_out=w_c_out,
             w_out=w_out, w_router=w_router, b_router=b_router)
    n_layers = w_in.shape[0]
    bp, sp, d = x_prompt.shape
    bs, ss, _ = x_sample.shape
    tp = min(PROMPT_TILE, sp)
    bts = min(SAMPLE_BT, bs)
    cfg_p = MixerCfg(bt=1, t=tp, ch=GMLP_CHUNK, kg=GMLP_CHUNK, kp=_round_up(POOL_HIST + tp, LANES), d=d,
                     start_pos=0, emit_v=False, fuse_in=False, steps=sp // tp, n_tiles=bp * (sp // tp))
    cfg_s = MixerCfg(bt=bts, t=ss, ch=ss, kg=_round_up(ss, LANES), kp=_round_up(POOL_HIST + ss, LANES), d=d,
                     start_pos=PAST_LEN, emit_v=True, fuse_in=False, steps=1, n_tiles=bs // bts)
    n_p, n_s = bp * sp, bs * ss
    n_slots = (n_p + n_s) * TOP_K + N_EXPERTS * MOE_TM

    mod_all = _ada_call(jnp.concatenate([c_prompt, c_sample], axis=0), w_ada, b_ada)
    mod_p = mod_all[:, :bp].reshape(n_layers, bp, 1, 6 * d)
    mod_s = mod_all[:, bp:].reshape(n_layers, bs, 1, 6 * d)
    final_g = final_norm_g.reshape(1, d)

    hb = bp // 2
    halves = [(0, hb), (hb, bp - hb)]
    steps_p = sp // tp
    xs = x_sample
    parts = [dict(lo=0, n=bp, x=x_prompt)]
    zero_pool = jnp.zeros((bp, POOL_HIST, d), F32)
    zero_conv = jnp.zeros((bp, CONV_HIST, d), F32)
    zero_cnt = jnp.zeros((1, ROUTER_LANES), F32)
    fuse_p, fuse_s = {}, ()
    pools_p, convs_p, pools_s, convs_s, vs = [], [], [], [], []

    def holder(lo):
        return next(pp for pp in parts if pp["lo"] <= lo < pp["lo"] + pp["n"])

    for l in range(n_layers):
        fused = l > 0
        lw_p, lw_s = _layer_weights(l, p, (cfg_p, cfg_s))
        cnt = zero_cnt
        new_parts = []
        for lo, n in (halves if fused else [(0, bp)]):
            src = holder(lo)
            x_new, h2, ti, tg, rk, cnt, npool, nconv = _mixer_call(
                cfg_p._replace(fuse_in=fused, n_tiles=n * steps_p), src["x"], lo - src["lo"], mod_p[l, lo:lo + n],
                zero_pool[lo:lo + n], zero_conv[lo:lo + n], lw_p, cnt[0:1], fuse_p.get((lo, n), ()))
            new_parts.append(dict(lo=lo, n=n, x=x_new, h2=h2.reshape(n * sp, d // 2), ti=ti, tg=tg, rk=rk,
                                  npool=npool, nconv=nconv))
        parts = new_parts
        xs, h2s, tis, tgs, rks, cnt_all, npool_s, nconv_s, v_s = _mixer_call(
            cfg_s._replace(fuse_in=fused), xs, 0, mod_s[l], _pad_front(state_pool[l], POOL_HIST),
            _pad_front(state_conv[l], CONV_HIST), lw_s, cnt[0:1], fuse_s)
        top_i = jnp.concatenate([pp["ti"] for pp in parts] + [tis], axis=0)[:, :TOP_K]
        rank = jnp.concatenate([pp["rk"] for pp in parts] + [rks], axis=0)[:, :TOP_K]
        tg_p = jnp.concatenate([pp["tg"] for pp in parts], axis=0)
        slot_of, block_e, nused = _route(top_i, rank, cnt_all[0, :N_EXPERTS].astype(jnp.int32))
        tables = [pp["h2"] for pp in parts] + [h2s.reshape(n_s, d // 2)]
        bounds = [pp["lo"] * sp for pp in parts] + [n_p, n_p + n_s]
        slots = [slot_of[:, bounds[i]:bounds[i + 1]] for i in range(len(tables))]
        x_sorted = _sc_dispatch(tables, slots, n_slots)
        ys = _moe_call(x_sorted, block_e, nused, l, w_gate_up, b_gate_up, w_down, b_down)
        (lo_a, n_a), (lo_b, n_b) = halves
        (y4a,) = _sc_collect(ys, [slot_of[:, lo_a * sp:(lo_a + n_a) * sp]])
        y4b, y4s = _sc_collect(ys, [slot_of[:, lo_b * sp:(lo_b + n_b) * sp], slot_of[:, n_p:]])
        fuse_p = {(lo, n): (y4.reshape(TOP_K, n * sp, d // 2), tg_p[lo * sp:(lo + n) * sp], mod_p[l, lo:lo + n])
                  for (lo, n), y4 in zip(halves, (y4a, y4b))}
        fuse_s = (y4s.reshape(TOP_K, n_s, d // 2), tgs, mod_s[l])
        pools_p.append(jnp.concatenate([pp["npool"] for pp in parts], axis=0))
        convs_p.append(jnp.concatenate([pp["nconv"] for pp in parts], axis=0))
        pools_s.append(npool_s)
        convs_s.append(nconv_s)
        vs.append(v_s)
    yp = None
    for lo, n in halves:
        src = holder(lo)
        y4, tg, mod = fuse_p[(lo, n)]
        yp = _combine_call(src["x"], lo - src["lo"], mod, y4, tg, final_g, 1, tp, bp, lo, yp)
    ys_out = _combine_call(xs, 0, mod_s[-1], *fuse_s[:2], final_g, bts, ss, bs, 0)
    return (yp, ys_out, jnp.stack(pools_p), jnp.stack(convs_p), jnp.stack(pools_s), jnp.stack(convs_s),
            jnp.stack(vs))
```

```python
import functools
from typing import NamedTuple

import jax
import jax.numpy as jnp
from jax import lax
from jax.experimental import pallas as pl
from jax.experimental.pallas import tpu as pltpu
from jax.experimental.pallas import tpu_sc as plsc

GMLP_CHUNK = 128
GMLP_HEADS = 4
POOL_WINDOWS = (2, 4, 8, 16)
POOL_STATE = max(POOL_WINDOWS) - 1
CONV_K = 31
CONV_STATE = CONV_K - 1
N_BRANCH = 3
N_EXPERTS = 32
TOP_K = 4
SWIGLU_LIMIT = 7.0
SWIGLU_ALPHA = 1.702
EPS = 1e-6
PAST_LEN = 2048

LANES = 128
SUBLANES = 8
V7X_VMEM_LIMIT_BYTES = 56 * 2**20

POOL_HIST = 16
CONV_HIST = 32
ROUTER_LANES = LANES
TOPK_LANES = 8
MOE_TM = 512
PROMPT_TILE = 256
SAMPLE_BT = 8

F32 = jnp.float32
BF16 = jnp.bfloat16


def _round_up(a, m):
    return (a + m - 1) // m * m


class MixerCfg(NamedTuple):
    bt: int
    t: int
    ch: int
    kg: int
    kp: int
    d: int
    start_pos: int
    emit_v: bool
    fuse_in: bool
    steps: int
    n_tiles: int


def _rms(x, g):
    return x * lax.rsqrt(jnp.mean(x * x, axis=-1, keepdims=True) + EPS) * g


def _ln(x, g, b):
    mu = jnp.mean(x, axis=-1, keepdims=True)
    xc = x - mu
    var = jnp.mean(xc * xc, axis=-1, keepdims=True)
    return xc * lax.rsqrt(var + EPS) * g + b


def _dot(a, b):
    return jnp.dot(a, b, preferred_element_type=F32)


def _pack_rows(x):
    half = x.shape[1] // 2
    hi = lax.bitcast_convert_type(x[:, :half].astype(BF16).astype(F32), jnp.int32)
    lo = lax.bitcast_convert_type(x[:, half:].astype(BF16).astype(F32), jnp.int32)
    return hi | lax.shift_right_logical(lo, 16)


def _unpack_rows(w):
    hi = lax.bitcast_convert_type(w & jnp.int32(-65536), F32)
    lo = lax.bitcast_convert_type(lax.shift_left(w, 16), F32)
    return jnp.concatenate([hi, lo], axis=1)


def _moe_mix(y4_ref, tg):
    f = tg[:, 0:1] * _unpack_rows(y4_ref[0])
    for k in range(1, TOP_K):
        f = f + tg[:, k:k + 1] * _unpack_rows(y4_ref[k])
    return f


def _mixer_kernel(cfg, x_ref, mod_ref, modb_ref, pp_ref, pc_ref, nmg_ref, nfg_ref, win_ref, lnvg_ref, lnvb_ref,
                  wsp_ref, bsf_ref, wao_ref, wpool_ref, bpool_ref, pscale_ref, wdw_ref, bdw_ref,
                  lncg_ref, lncb_ref, wco_ref, wout_ref, wrh_ref, wrl_ref, br_ref, pmat_ref, ltri_ref, cnt0_ref,
                  *rest):
    rest = list(rest)
    if cfg.fuse_in:
        y4_ref, tgp_ref, modp_ref = rest[:3]
        rest = rest[3:]
    xo_ref, h2_ref, ti_ref, tg_ref, rk_ref, cnt_ref, npool_ref, nconv_ref = rest[:8]
    rest = rest[8:]
    if cfg.emit_v:
        v_ref, hp_scr, xc_scr, run_scr, mg_scr, xr_scr = rest
    else:
        hp_scr, xc_scr, run_scr, mg_scr, xr_scr = rest
    bt, t, d = cfg.bt, cfg.t, cfg.d
    m = bt * t
    n_slab = d // LANES
    gd = d // len(POOL_WINDOWS)
    hd = d // GMLP_HEADS
    q = pl.program_id(0)
    j = 0 if cfg.steps == 1 else lax.rem(jnp.minimum(q, cfg.n_tiles - 1), cfg.steps)

    def mod_rows(k, ref=mod_ref):
        r = ref[:, :, k * d:(k + 1) * d]
        if bt == 1:
            return r.reshape(1, d)
        return jnp.broadcast_to(r, (bt, t, d)).reshape(m, d)

    @pl.when(q == 0)
    def _init_carries():
        run_scr[...] = cnt0_ref[...]
        mg_scr[...] = jnp.zeros_like(mg_scr)
        xr_scr[...] = jnp.zeros_like(xr_scr)

    @pl.when(j == 0)
    def _load_state():
        hp_scr[...] = pp_ref[...]
        for bi in range(bt):
            for c in range(n_slab):
                xc_scr[bi, c, 0:CONV_HIST, :] = pc_ref[bi, :, c * LANES:(c + 1) * LANES]

    x_prev = xr_scr[...]
    out_prev = _dot(mg_scr[...], wout_ref[...])

    x = x_ref[...].reshape(m, d)
    if cfg.fuse_in:
        x = x + mod_rows(5, modp_ref) * _moe_mix(y4_ref, tgp_ref[...])
    sh1, sc1 = mod_rows(0), mod_rows(1)
    h = (_rms(x, nmg_ref[...]) * (1.0 + sc1) + sh1).astype(BF16)

    def zcols(lo, hi):
        return _dot(h, win_ref[:, lo:hi])

    off_u, off_v, off_b, off_c, off_g = 0, d, 2 * d, 3 * d, 5 * d
    zc_a = zcols(off_c, off_c + d)
    zc_b = zcols(off_c + d, off_c + 2 * d)

    x_new = x_prev + mod_rows(2, modb_ref) * out_prev
    xo_ref[...] = x_new.reshape(bt, t, d)
    h2 = _rms(x_new, nfg_ref[...]) * (1.0 + mod_rows(4, modb_ref)) + mod_rows(3, modb_ref)
    h2_ref[...] = _pack_rows(h2).reshape(bt, t, d // 2)
    h2_hi = h2.astype(BF16)
    h2_lo = (h2 - h2_hi.astype(F32)).astype(BF16)
    logits = _dot(h2_hi, wrh_ref[...]) + _dot(h2_lo, wrh_ref[...]) + _dot(h2_hi, wrl_ref[...]) + br_ref[...]

    x_c = zc_a * jax.nn.sigmoid(zc_b)
    for bi in range(bt):
        for c in range(n_slab):
            xc_scr[bi, c, CONV_HIST:CONV_HIST + t, :] = x_c[bi * t:(bi + 1) * t, c * LANES:(c + 1) * LANES]

    lane_r = lax.broadcasted_iota(jnp.int32, (m, ROUTER_LANES), 1).astype(F32)
    lane_k = lax.broadcasted_iota(jnp.int32, (m, TOPK_LANES), 1)
    topk = dict(work=logits, ti=jnp.zeros((m, TOPK_LANES), jnp.int32), vals=[], onehots=[])

    def next_topk_round():
        r = len(topk["vals"])
        if r < TOP_K:
            mx = jnp.max(topk["work"], axis=-1, keepdims=True)
            idx = jnp.min(jnp.where(topk["work"] == mx, lane_r, float(ROUTER_LANES)), axis=-1, keepdims=True)
            topk["ti"] = jnp.where(lane_k == r, idx.astype(jnp.int32), topk["ti"])
            topk["vals"].append(mx)
            topk["work"] = jnp.where(lane_r == idx, -jnp.inf, topk["work"])
            topk["onehots"].append((lane_r == idx).astype(F32))

    proj_cols = dict(ga=off_g, gb=off_g + d, gc=off_g + 2 * d, zb=off_b, zu=off_u, zv=off_v)
    pending_proj = list(proj_cols)
    proj = {}

    def next_proj():
        if pending_proj:
            name = pending_proj.pop(0)
            proj[name] = zcols(proj_cols[name], proj_cols[name] + d)

    conv_parts = []
    for bi in range(bt):
        accs = []
        for c in range(n_slab):
            cs = slice(c * LANES, (c + 1) * LANES)
            acc = jnp.broadcast_to(bdw_ref[:, cs], (t, LANES))
            for k in range(CONV_K):
                acc = acc + wdw_ref[k:k + 1, cs] * xc_scr[bi, c, pl.ds(CONV_HIST - CONV_STATE + k, t), :]
            accs.append(acc)
            next_proj()
            next_topk_round()
        conv_parts.append(jnp.concatenate(accs, axis=1))
    for bi in range(bt):
        for c in range(n_slab):
            cs = slice(c * LANES, (c + 1) * LANES)
            tail = xc_scr[bi, c, t:t + CONV_HIST, :]
            nconv_ref[bi, :, cs] = xc_scr[bi, c, pl.ds(t + CONV_HIST - CONV_STATE, CONV_STATE), :]
            xc_scr[bi, c, 0:CONV_HIST, :] = tail
    while pending_proj:
        next_proj()
    while len(topk["vals"]) < TOP_K:
        next_topk_round()
    conv = conv_parts[0] if bt == 1 else jnp.concatenate(conv_parts, axis=0)

    top_vals, onehots = topk["vals"], topk["onehots"]
    exps = [jnp.exp(tv - top_vals[0]) for tv in top_vals]
    denom = exps[0] + exps[1] + exps[2] + exps[3]
    tg = jnp.zeros((m, TOPK_LANES), F32)
    for r in range(TOP_K):
        tg = jnp.where(lane_k == r, exps[r] / denom, tg)
    ti_ref[...] = topk["ti"]
    tg_ref[...] = tg
    base = run_scr[...]
    rk = jnp.zeros((m, TOPK_LANES), jnp.int32)
    new_base = base
    for r in range(TOP_K):
        before = _dot(ltri_ref[...], onehots[r].astype(BF16)) + new_base
        rank = jnp.sum(onehots[r] * before, axis=-1, keepdims=True)
        rk = jnp.where(lane_k == r, rank.astype(jnp.int32), rk)
        new_base = new_base + jnp.sum(onehots[r], axis=0, keepdims=True)
    new_base = jnp.where(q >= 1, new_base, base)
    run_scr[...] = new_base
    rk_ref[...] = rk
    cnt_ref[...] = jnp.broadcast_to(new_base, cnt_ref.shape)

    zb = proj["zb"]
    lane = lax.broadcasted_iota(jnp.int32, (1, d), 1)
    win_lane = jnp.left_shift(2, lane // gd).astype(F32)
    pos1 = (cfg.start_pos + 1 + j * t + lax.broadcasted_iota(jnp.int32, (t, 1), 0)).astype(F32)
    cnt = jnp.minimum(pos1, win_lane)
    pooled_parts = []
    for bi in range(bt):
        zb_b = zb[bi * t:(bi + 1) * t]
        pieces = [hp_scr[bi], zb_b]
        if cfg.kp > POOL_HIST + t:
            pieces = [jnp.zeros((cfg.kp - POOL_HIST - t, d), F32)] + pieces
        full = jnp.concatenate(pieces, axis=0)
        fullb = full.astype(BF16)
        sums = jnp.concatenate(
            [_dot(pmat_ref[g], fullb[:, g * gd:(g + 1) * gd]) for g in range(len(POOL_WINDOWS))], axis=1)
        pooled_parts.append(sums / cnt - zb_b)
        hp_scr[bi] = full[cfg.kp - POOL_HIST:]
        npool_ref[bi] = hp_scr[bi, pl.ds(POOL_HIST - POOL_STATE, POOL_STATE), :]
    pooled = (pooled_parts[0] if bt == 1 else jnp.concatenate(pooled_parts, axis=0)).astype(BF16)
    y_b = jnp.concatenate(
        [_dot(pooled[:, g * gd:(g + 1) * gd], wpool_ref[g]) for g in range(len(POOL_WINDOWS))], axis=1)
    y_b = (y_b + bpool_ref[...]) * pscale_ref[...]

    v = _ln(jax.nn.gelu(proj["zv"]), lnvg_ref[...], lnvb_ref[...])
    if cfg.emit_v:
        v_ref[...] = v.reshape(bt, t, d)
    vb = v.astype(BF16)
    mixed_rows = []
    for r0 in range(0, m, cfg.ch):
        vc = vb[r0:r0 + cfg.ch]
        if cfg.kg > cfg.ch:
            vc = jnp.concatenate([vc, jnp.zeros((cfg.kg - cfg.ch, d), BF16)], axis=0)
        mixed_rows.append(jnp.concatenate(
            [_dot(wsp_ref[hh], vc[:, hh * hd:(hh + 1) * hd]) for hh in range(GMLP_HEADS)], axis=1) + bsf_ref[...])
    mixed = mixed_rows[0] if len(mixed_rows) == 1 else jnp.concatenate(mixed_rows, axis=0)

    y_c = _dot(jax.nn.silu(_ln(conv, lncg_ref[...], lncb_ref[...])).astype(BF16), wco_ref[...])
    u = jax.nn.gelu(proj["zu"])
    y_a = _dot((u * mixed).astype(BF16), wao_ref[...])
    merged = jax.nn.sigmoid(proj["gb"]) * y_b
    merged = merged + jax.nn.sigmoid(proj["ga"]) * y_a
    merged = merged + jax.nn.sigmoid(proj["gc"]) * y_c

    mg_scr[...] = merged.astype(BF16)
    xr_scr[...] = x


def _const_spec(shape):
    nd = len(shape)
    return pl.BlockSpec(shape, lambda q, _nd=nd: (0,) * _nd, pipeline_mode=pl.Buffered(1))


def _pool_band(t, kp):
    col = jnp.arange(kp)[None, :]
    end = (kp - t) + jnp.arange(t)[:, None]
    return jnp.stack([((col <= end) & (col > end - w)) for w in POOL_WINDOWS]).astype(BF16)


def _mixer_call(cfg, x, b0, mod, prev_pool, prev_conv, lw, cnt0, fuse=()):
    _, s, d = x.shape
    b = mod.shape[0]
    bt, t = cfg.bt, cfg.t
    m = bt * t
    steps, n_tiles = cfg.steps, cfg.n_tiles
    assert steps == s // t and n_tiles == (b // bt) * steps and b0 % bt == 0
    assert steps == 1 or t >= max(CONV_STATE, POOL_STATE)
    n_slab = d // LANES

    def front(q):
        return jnp.minimum(q, n_tiles - 1)

    def back(q):
        return jnp.maximum(q - 1, 0)

    def tile_spec(which, width=d):
        return pl.BlockSpec((bt, t, width), lambda q: (which(q) // steps, which(q) % steps, 0))

    def batch_spec(which, rows, width):
        return pl.BlockSpec((bt, rows, width), lambda q: (which(q) // steps, 0, 0))

    def rows_spec(which, width):
        return pl.BlockSpec((m, width), lambda q: (which(q), 0))

    consts = [lw["norm_mix_g"], lw["norm_ffn_g"], lw["w_in"], lw["ln_v_g"], lw["ln_v_b"], lw["w_sp"], lw["bs_full"],
              lw["w_a_out"], lw["w_pool"], lw["b_pool"], lw["pool_scale"], lw["w_dw"], lw["b_dw"], lw["ln_c_g"],
              lw["ln_c_b"], lw["w_c_out"], lw["w_out"], lw["wr_hi"], lw["wr_lo"], lw["b_router"], lw["pmat"],
              lw["ltri"], cnt0]
    x_spec = pl.BlockSpec((bt, t, d), lambda q: (front(q) // steps + b0 // bt, front(q) % steps, 0))
    in_specs = [x_spec, batch_spec(front, 1, 6 * d), batch_spec(back, 1, 6 * d),
                batch_spec(front, POOL_HIST, d), batch_spec(front, CONV_HIST, d)]
    in_specs += [_const_spec(c.shape) for c in consts]
    if cfg.fuse_in:
        in_specs += [pl.BlockSpec((TOP_K, m, d // 2), lambda q: (0, front(q), 0)),
                     rows_spec(front, TOPK_LANES), batch_spec(front, 1, 6 * d)]
    out_shape = [jax.ShapeDtypeStruct((b, s, d), F32), jax.ShapeDtypeStruct((b, s, d // 2), jnp.int32),
                 jax.ShapeDtypeStruct((b * s, TOPK_LANES), jnp.int32), jax.ShapeDtypeStruct((b * s, TOPK_LANES), F32),
                 jax.ShapeDtypeStruct((b * s, TOPK_LANES), jnp.int32),
                 jax.ShapeDtypeStruct((SUBLANES, ROUTER_LANES), F32),
                 jax.ShapeDtypeStruct((b, POOL_STATE, d), F32), jax.ShapeDtypeStruct((b, CONV_STATE, d), F32)]
    out_specs = [tile_spec(back), tile_spec(back, d // 2),
                 rows_spec(back, TOPK_LANES), rows_spec(back, TOPK_LANES), rows_spec(back, TOPK_LANES),
                 pl.BlockSpec((SUBLANES, ROUTER_LANES), lambda q: (0, 0)),
                 batch_spec(front, POOL_STATE, d), batch_spec(front, CONV_STATE, d)]
    if cfg.emit_v:
        out_shape.append(jax.ShapeDtypeStruct((b, s, d), F32))
        out_specs.append(tile_spec(front))
    return pl.pallas_call(
        functools.partial(_mixer_kernel, cfg),
        grid=(n_tiles + 1,),
        in_specs=in_specs,
        out_specs=out_specs,
        out_shape=out_shape,
        scratch_shapes=[pltpu.VMEM((bt, POOL_HIST, d), F32),
                        pltpu.VMEM((bt, n_slab, CONV_HIST + t, LANES), F32),
                        pltpu.VMEM((1, ROUTER_LANES), F32),
                        pltpu.VMEM((m, d), BF16), pltpu.VMEM((m, d), F32)],
        compiler_params=pltpu.CompilerParams(dimension_semantics=("arbitrary",),
                                             vmem_limit_bytes=V7X_VMEM_LIMIT_BYTES),
        name=f"mixer_t{t}",
    )(x, mod, mod, prev_pool, prev_conv, *consts, *fuse)


def _ada_kernel(c_ref, w_ref, b_ref, o_ref):
    o_ref[0] = _dot(jax.nn.silu(c_ref[...]).astype(BF16), w_ref[0].astype(BF16)) + b_ref[0]


def _ada_call(c_all, w_ada, b_ada):
    n_layers, d, six_d = w_ada.shape
    rows = c_all.shape[0]
    bn = six_d // 6
    return pl.pallas_call(
        _ada_kernel,
        grid=(n_layers, six_d // bn),
        in_specs=[pl.BlockSpec((rows, d), lambda l, n: (0, 0)),
                  pl.BlockSpec((1, d, bn), lambda l, n: (l, 0, n)),
                  pl.BlockSpec((1, 1, bn), lambda l, n: (l, 0, n))],
        out_specs=pl.BlockSpec((1, rows, bn), lambda l, n: (l, 0, n)),
        out_shape=jax.ShapeDtypeStruct((n_layers, rows, six_d), F32),
        compiler_params=pltpu.CompilerParams(dimension_semantics=("arbitrary", "arbitrary")),
        name="adaln",
    )(c_all, w_ada, b_ada.reshape(n_layers, 1, six_d))


def _moe_kernel(be_ref, nused_ref, x_ref, wgu_ref, bgu_ref, wdn_ref, bdn_ref, y_ref, wgu_bf, wdn_bf):
    i = pl.program_id(0)
    nused = nused_ref[0]
    dff = wdn_bf.shape[0]

    @pl.when(i < nused)
    def _body():
        changed = jnp.logical_or(i == 0, be_ref[i] != be_ref[jnp.maximum(i - 1, 0)])

        @pl.when(changed)
        def _cast_weights():
            wgu_bf[...] = wgu_ref[0].astype(BF16)
            wdn_bf[...] = wdn_ref[0].astype(BF16)

        gu = _dot(_unpack_rows(x_ref[...]).astype(BF16), wgu_bf[...]) + bgu_ref[0]
        g = jnp.minimum(gu[:, :dff], SWIGLU_LIMIT)
        u = jnp.clip(gu[:, dff:], -SWIGLU_LIMIT, SWIGLU_LIMIT)
        act = (u + 1.0) * g * jax.nn.sigmoid(SWIGLU_ALPHA * g)
        y_ref[...] = _pack_rows(_dot(act.astype(BF16), wdn_bf[...]) + bdn_ref[0])

    @pl.when(i >= nused)
    def _unused_block():
        y_ref[...] = jnp.zeros_like(y_ref)


def _moe_call(xs, block_e, nused, layer, w_gu_all, b_gu_all, w_dn_all, b_dn_all):
    n_blocks = block_e.shape[0]
    n_layers, n_exp, d, two_f = w_gu_all.shape
    dff = two_f // 2
    tm = MOE_TM
    e0 = layer * n_exp
    w_gu = w_gu_all.reshape(n_layers * n_exp, d, two_f)
    b_gu = b_gu_all.reshape(n_layers * n_exp, 1, two_f)
    w_dn = w_dn_all.reshape(n_layers * n_exp, dff, d)
    b_dn = b_dn_all.reshape(n_layers * n_exp, 1, d)
    grid_spec = pltpu.PrefetchScalarGridSpec(
        num_scalar_prefetch=2,
        grid=(n_blocks,),
        in_specs=[
            pl.BlockSpec((tm, d // 2), lambda i, be, nu: (i, 0)),
            pl.BlockSpec((1, d, two_f), lambda i, be, nu: (e0 + be[i], 0, 0)),
            pl.BlockSpec((1, 1, two_f), lambda i, be, nu: (e0 + be[i], 0, 0)),
            pl.BlockSpec((1, dff, d), lambda i, be, nu: (e0 + be[i], 0, 0)),
            pl.BlockSpec((1, 1, d), lambda i, be, nu: (e0 + be[i], 0, 0)),
        ],
        out_specs=pl.BlockSpec((tm, d // 2), lambda i, be, nu: (i, 0)),
        scratch_shapes=[pltpu.VMEM((d, two_f), BF16), pltpu.VMEM((dff, d), BF16)],
    )
    return pl.pallas_call(
        _moe_kernel,
        grid_spec=grid_spec,
        out_shape=jax.ShapeDtypeStruct((n_blocks * tm, d // 2), jnp.int32),
        compiler_params=pltpu.CompilerParams(dimension_semantics=("arbitrary",),
                                             vmem_limit_bytes=V7X_VMEM_LIMIT_BYTES),
        name="moe_experts",
    )(block_e, nused, xs, w_gu, b_gu, w_dn, b_dn)


def _route(top_i, rank, counts):
    tm = MOE_TM
    n_all = top_i.shape[0]
    n_blocks = n_all * TOP_K // tm + N_EXPERTS
    padded = (counts + tm - 1) // tm * tm
    pad_ends = jnp.cumsum(padded)
    pad_starts = pad_ends - padded
    blk = jnp.arange(n_blocks, dtype=jnp.int32)[:, None] * tm
    block_e = jnp.minimum(jnp.sum((pad_ends[None, :] <= blk).astype(jnp.int32), axis=1), N_EXPERTS - 1)
    nused = (pad_ends[-1] // tm).astype(jnp.int32).reshape(1)
    onehot = top_i[:, :, None] == jnp.arange(N_EXPERTS, dtype=jnp.int32)[None, None, :]
    slot_of = jnp.sum(jnp.where(onehot, pad_starts[None, None, :], 0), axis=-1) + rank
    return slot_of.T.astype(jnp.int32), block_e.astype(jnp.int32), nused


SC_CORES = 2
SC_SUBCORES = 16
SC_WORKERS = SC_CORES * SC_SUBCORES
SC_CHUNK = 64


def _sc_worker_id():
    return lax.axis_index("s") * SC_CORES + lax.axis_index("c")


def _sc_plan(n_rows, split=False):
    per_w = n_rows // SC_WORKERS
    ch = min(SC_CHUNK, per_w // 2 if split else per_w)
    n_ch = per_w // ch
    assert per_w * SC_WORKERS == n_rows and n_ch * ch == per_w
    return per_w, ch, n_ch, (n_ch >= 2 and n_ch % 2 == 0)


def _sc_dispatch(tables, slots, n_slots):
    nt = len(tables)
    d, dtype = tables[0].shape[1], tables[0].dtype
    plans = [_sc_plan(h.shape[0]) for h in tables]

    buf_shapes = sorted({(2 if piped else 1, ch) for (_, ch, _, piped) in plans})

    def body(*refs):
        h_refs, idx_refs, out_hbm = refs[:nt], refs[nt:2 * nt], refs[2 * nt]
        scratch = refs[2 * nt + 1:]
        idx_vs, shared, (rsem, wsem) = scratch[:nt], scratch[nt:-2], scratch[-2:]
        bufs = [shared[buf_shapes.index((2 if piped else 1, ch))] for (_, ch, _, piped) in plans]
        wid = _sc_worker_id()
        for h_hbm, idx_hbm, idx_v, buf, (tpw, ch, n_ch, piped) in zip(h_refs, idx_refs, idx_vs, bufs, plans):
            pltpu.sync_copy(idx_hbm.at[wid], idx_v)

            def read(c, b, h_hbm=h_hbm, buf=buf, tpw=tpw, ch=ch):
                return pltpu.make_async_copy(h_hbm.at[pl.ds(wid * tpw + c * ch, ch)], buf.at[b], rsem.at[b])

            def write(c, k, b, buf=buf, idx_v=idx_v):
                return pltpu.make_async_copy(buf.at[b], out_hbm.at[idx_v.at[c * TOP_K + k]], wsem.at[b])

            if piped:
                read(0, 0).start()

                @pl.loop(0, n_ch, step=2)
                def _(c0, read=read, write=write, n_ch=n_ch):
                    for b in range(2):
                        c = c0 + b

                        @pl.when(c + 1 < n_ch)
                        def _():
                            @pl.when(c >= 1)
                            def _():
                                for k in range(TOP_K):
                                    write(c - 1, k, 1 - b).wait()
                            read(c + 1, 1 - b).start()

                        read(c, b).wait()
                        for k in range(TOP_K):
                            write(c, k, b).start()

                for c in (n_ch - 2, n_ch - 1):
                    for k in range(TOP_K):
                        write(c, k, c % 2).wait()
            else:
                for c in range(n_ch):
                    cp_in = read(c, 0)
                    cp_in.start()
                    cp_in.wait()
                    for k in range(TOP_K):
                        write(c, k, 0).start()
                    for k in range(TOP_K):
                        write(c, k, 0).wait()

    scratch_types = [pltpu.VMEM((n_ch * TOP_K, ch), jnp.int32) for (_, ch, n_ch, _) in plans]
    scratch_types += [pltpu.VMEM((nbuf, ch, d), dtype) for (nbuf, ch) in buf_shapes]
    scratch_types += [pltpu.SemaphoreType.DMA((2,)), pltpu.SemaphoreType.DMA((2,))]
    call = pl.kernel(
        body,
        out_type=jax.ShapeDtypeStruct((n_slots, d), dtype),
        mesh=plsc.VectorSubcoreMesh(core_axis_name="c", subcore_axis_name="s"),
        scratch_types=scratch_types,
        name="sc_dispatch",
    )
    idx = [s.reshape(TOP_K, SC_WORKERS, n_ch, ch).transpose(1, 2, 0, 3).reshape(SC_WORKERS, n_ch * TOP_K, ch)
           for s, (_, ch, n_ch, _) in zip(slots, plans)]
    return call(*tables, *idx)


def _sc_collect(ys, slots):
    nt = len(slots)
    d, dtype = ys.shape[1], ys.dtype
    plans = [_sc_plan(s.size, split=True) for s in slots]
    assert all(piped for (_, _, _, piped) in plans)

    def body(*refs):
        ys_hbm, idx_refs, out_refs = refs[0], refs[1:1 + nt], refs[1 + nt:1 + 2 * nt]
        scratch = refs[1 + 2 * nt:]
        idx_vs, bufs, (gsem, wsem) = scratch[:nt], scratch[nt:2 * nt], scratch[2 * nt:]
        wid = _sc_worker_id()
        for idx_hbm, out_hbm, idx_v, rows_v, (per_w, ch, n_ch, _) in zip(idx_refs, out_refs, idx_vs, bufs, plans):
            base = wid * per_w
            pltpu.sync_copy(idx_hbm.at[pl.ds(base, per_w)], idx_v)

            def gather(g, b, idx_v=idx_v, rows_v=rows_v, ch=ch):
                return pltpu.make_async_copy(ys_hbm.at[idx_v.at[pl.ds(g * ch, ch)]], rows_v.at[b], gsem.at[b])

            def put(g, b, out_hbm=out_hbm, rows_v=rows_v, base=base, ch=ch):
                return pltpu.make_async_copy(rows_v.at[b], out_hbm.at[pl.ds(base + g * ch, ch)], wsem.at[b])

            gather(0, 0).start()

            @pl.loop(0, n_ch, step=2)
            def _(g0, gather=gather, put=put, n_ch=n_ch):
                for b in range(2):
                    g = g0 + b

                    @pl.when(g + 1 < n_ch)
                    def _():
                        @pl.when(g >= 1)
                        def _():
                            put(g - 1, 1 - b).wait()
                        gather(g + 1, 1 - b).start()

                    gather(g, b).wait()
                    put(g, b).start()

            for g in (n_ch - 2, n_ch - 1):
                put(g, g % 2).wait()

    call = pl.kernel(
        body,
        out_type=tuple(jax.ShapeDtypeStruct((s.size, d), dtype) for s in slots),
        mesh=plsc.VectorSubcoreMesh(core_axis_name="c", subcore_axis_name="s"),
        scratch_types=[pltpu.VMEM((per_w,), jnp.int32) for (per_w, _, _, _) in plans]
        + [pltpu.VMEM((2, ch, d), dtype) for (_, ch, _, _) in plans]
        + [pltpu.SemaphoreType.DMA((2,)), pltpu.SemaphoreType.DMA((2,))],
        name="sc_collect",
    )
    out = call(ys, *[s.reshape(s.size) for s in slots])
    return out if isinstance(out, (tuple, list)) else (out,)


def _combine_kernel(x_ref, mod_ref, y_ref, tg_ref, fg_ref, *rest):
    o_ref = rest[-1]
    bt, t, d = x_ref.shape
    m = bt * t
    g2 = mod_ref[:, :, 5 * d:6 * d]
    g2 = g2.reshape(1, d) if bt == 1 else jnp.broadcast_to(g2, (bt, t, d)).reshape(m, d)
    out = x_ref[...].reshape(m, d) + g2 * _moe_mix(y_ref, tg_ref[...])
    o_ref[...] = _rms(out, fg_ref[...]).reshape(bt, t, d)


def _combine_call(x_new, b0, mod, y4, tg, final_g, bt, t, out_batch, out_b0, out_prev=None):
    _, s, d = x_new.shape
    b = mod.shape[0]
    m = bt * t
    steps = s // t
    ob = out_b0 // bt
    in_specs = [pl.BlockSpec((bt, t, d), lambda bi, j: (bi + b0 // bt, j, 0)),
                pl.BlockSpec((bt, 1, 6 * d), lambda bi, j: (bi, 0, 0)),
                pl.BlockSpec((TOP_K, m, d // 2), lambda bi, j: (0, (bi + ob) * steps + j, 0)),
                pl.BlockSpec((m, TOPK_LANES), lambda bi, j: ((bi + ob) * steps + j, 0)),
                pl.BlockSpec((1, d), lambda bi, j: (0, 0))]
    operands = [x_new, mod, y4, tg, final_g]
    aliases = {}
    if out_prev is not None:
        in_specs.append(pl.BlockSpec(memory_space=pl.ANY))
        operands.append(out_prev)
        aliases = {len(operands) - 1: 0}
    return pl.pallas_call(
        _combine_kernel,
        grid=(b // bt, steps),
        in_specs=in_specs,
        out_specs=pl.BlockSpec((bt, t, d), lambda bi, j: (bi + out_b0 // bt, j, 0)),
        out_shape=jax.ShapeDtypeStruct((out_batch, s, d), F32),
        input_output_aliases=aliases,
        cost_estimate=pl.CostEstimate(flops=2 * (TOP_K + 3) * b * s * d, transcendentals=b * s,
                                      bytes_accessed=b * s * d * (4 + 4 + TOP_K * 2)),
        compiler_params=pltpu.CompilerParams(dimension_semantics=("arbitrary", "arbitrary")),
        name=f"combine_t{t}",
    )(*operands)


def _layer_weights(l, p, cfgs):
    d = p["w_out"].shape[-1]
    row = lambda v: v[l].reshape(1, d)
    wr = p["w_router"][l]
    wr_pad = jnp.zeros((d, ROUTER_LANES), F32).at[:, :N_EXPERTS].set(wr)
    wr_hi = wr_pad.astype(BF16)
    wr_lo = (wr_pad - wr_hi.astype(F32)).astype(BF16)
    b_router = jnp.full((1, ROUTER_LANES), -1e30, F32).at[0, :N_EXPERTS].set(p["b_router"][l])
    mask = jnp.tril(jnp.ones((GMLP_CHUNK, GMLP_CHUNK), dtype=bool))
    w_sp_full = jnp.where(mask[None], p["w_spatial"][l], 0)
    hd = d // GMLP_HEADS
    base = dict(norm_mix_g=row(p["norm_mix_g"]), norm_ffn_g=row(p["norm_ffn_g"]), w_in=p["w_in"][l].astype(BF16),
                ln_v_g=row(p["ln_v_g"]), ln_v_b=row(p["ln_v_b"]), w_a_out=p["w_a_out"][l].astype(BF16),
                w_pool=p["w_pool"][l].astype(BF16), b_pool=row(p["b_pool"]), pool_scale=row(p["pool_scale"]),
                w_dw=jnp.zeros((_round_up(CONV_K, SUBLANES), d), F32).at[:CONV_K].set(p["w_dw"][l]),
                b_dw=row(p["b_dw"]), ln_c_g=row(p["ln_c_g"]), ln_c_b=row(p["ln_c_b"]),
                w_c_out=p["w_c_out"][l].astype(BF16), w_out=p["w_out"][l].astype(BF16),
                wr_hi=wr_hi, wr_lo=wr_lo, b_router=b_router)
    out = []
    for cfg in cfgs:
        w_sp = jnp.zeros((GMLP_HEADS, cfg.ch, cfg.kg), F32).at[:, :, :cfg.ch].set(w_sp_full[:, :cfg.ch, :cfg.ch])
        bs_full = jnp.repeat(p["b_spatial"][l][:, :cfg.ch].T, hd, axis=1)
        m = cfg.bt * cfg.t
        ltri = (jnp.arange(m)[:, None] > jnp.arange(m)[None, :]).astype(BF16)
        out.append(dict(base, w_sp=w_sp.astype(BF16), bs_full=bs_full, pmat=_pool_band(cfg.t, cfg.kp), ltri=ltri))
    return out


def _pad_front(state, rows):
    pad = rows - state.shape[-2]
    return jnp.pad(state, ((0, 0),) * (state.ndim - 2) + ((pad, 0), (0, 0)))


def kernel(x_prompt, x_sample, c_prompt, c_sample, state_pool, state_conv, norm_mix_g, norm_ffn_g, w_ada, b_ada, w_in, ln_v_g, ln_v_b, w_spatial, b_spatial, w_a_out, w_pool, b_pool, pool_scale, w_dw, b_dw, ln_c_g, ln_c_b, w_c_out, w_out, w_router, b_router, w_gate_up, b_gate_up, w_down, b_down, final_norm_g):
    p = dict(norm_mix_g=norm_mix_g, norm_ffn_g=norm_ffn_g, w_in=w_in, ln_v_g=ln_v_g, ln_v_b=ln_v_b,
             w_spatial=w_spatial, b_spatial=b_spatial, w_a_out=w_a_out, w_pool=w_pool, b_pool=b_pool,
             pool_scale=pool_scale, w_dw=w_dw, b_dw=b_dw, ln_c_g=ln_c_g, ln_c_b=ln_c_b, w_c_out=w_c_out,
             w_out=w_out, w_router=w_router, b_router=b_router)
    n_layers = w_in.shape[0]
    bp, sp, d = x_prompt.shape
    bs, ss, _ = x_sample.shape
    tp = min(PROMPT_TILE, sp)
    bts = min(SAMPLE_BT, bs)
    cfg_p = MixerCfg(bt=1, t=tp, ch=GMLP_CHUNK, kg=GMLP_CHUNK, kp=_round_up(POOL_HIST + tp, LANES), d=d,
                     start_pos=0, emit_v=False, fuse_in=False, steps=sp // tp, n_tiles=bp * (sp // tp))
    cfg_s = MixerCfg(bt=bts, t=ss, ch=ss, kg=_round_up(ss, LANES), kp=_round_up(POOL_HIST + ss, LANES), d=d,
                     start_pos=PAST_LEN, emit_v=True, fuse_in=False, steps=1, n_tiles=bs // bts)
    n_p, n_s = bp * sp, bs * ss
    n_slots = (n_p + n_s) * TOP_K + N_EXPERTS * MOE_TM

    mod_all = _ada_call(jnp.concatenate([c_prompt, c_sample], axis=0), w_ada, b_ada)
    mod_p = mod_all[:, :bp].reshape(n_layers, bp, 1, 6 * d)
    mod_s = mod_all[:, bp:].reshape(n_layers, bs, 1, 6 * d)
    final_g = final_norm_g.reshape(1, d)

    hb = bp // 2
    halves = [(0, hb), (hb, bp - hb)]
    steps_p = sp // tp
    xs = x_sample
    parts = [dict(lo=0, n=bp, x=x_prompt)]
    zero_pool = jnp.zeros((bp, POOL_HIST, d), F32)
    zero_conv = jnp.zeros((bp, CONV_HIST, d), F32)
    zero_cnt = jnp.zeros((1, ROUTER_LANES), F32)
    fuse_p, fuse_s = {}, ()
    pools_p, convs_p, pools_s, convs_s, vs = [], [], [], [], []

    def holder(lo):
        return next(pp for pp in parts if pp["lo"] <= lo < pp["lo"] + pp["n"])

    for l in range(n_layers):
        fused = l > 0
        lw_p, lw_s = _layer_weights(l, p, (cfg_p, cfg_s))
        cnt = zero_cnt
        new_parts = []
        for lo, n in (halves if fused else [(0, bp)]):
            src = holder(lo)
            x_new, h2, ti, tg, rk, cnt, npool, nconv = _mixer_call(
                cfg_p._replace(fuse_in=fused, n_tiles=n * steps_p), src["x"], lo - src["lo"], mod_p[l, lo:lo + n],
                zero_pool[lo:lo + n], zero_conv[lo:lo + n], lw_p, cnt[0:1], fuse_p.get((lo, n), ()))
            new_parts.append(dict(lo=lo, n=n, x=x_new, h2=h2.reshape(n * sp, d // 2), ti=ti, tg=tg, rk=rk,
                                  npool=npool, nconv=nconv))
        parts = new_parts
        xs, h2s, tis, tgs, rks, cnt_all, npool_s, nconv_s, v_s = _mixer_call(
            cfg_s._replace(fuse_in=fused), xs, 0, mod_s[l], _pad_front(state_pool[l], POOL_HIST),
            _pad_front(state_conv[l], CONV_HIST), lw_s, cnt[0:1], fuse_s)
        top_i = jnp.concatenate([pp["ti"] for pp in parts] + [tis], axis=0)[:, :TOP_K]
        rank = jnp.concatenate([pp["rk"] for pp in parts] + [rks], axis=0)[:, :TOP_K]
        tg_p = jnp.concatenate([pp["tg"] for pp in parts], axis=0)
        slot_of, block_e, nused = _route(top_i, rank, cnt_all[0, :N_EXPERTS].astype(jnp.int32))
        tables = [pp["h2"] for pp in parts] + [h2s.reshape(n_s, d // 2)]
        bounds = [pp["lo"] * sp for pp in parts] + [n_p, n_p + n_s]
        slots = [slot_of[:, bounds[i]:bounds[i + 1]] for i in range(len(tables))]
        x_sorted = _sc_dispatch(tables, slots, n_slots)
        ys = _moe_call(x_sorted, block_e, nused, l, w_gate_up, b_gate_up, w_down, b_down)
        consumers = halves if l < n_layers - 1 else [(0, bp)]
        y4_parts = []
        for lo, n in consumers[:-1]:
            y4_parts += _sc_collect(ys, [slot_of[:, lo * sp:(lo + n) * sp]])
        lo, n = consumers[-1]
        y4_last, y4s = _sc_collect(ys, [slot_of[:, lo * sp:(lo + n) * sp], slot_of[:, n_p:]])
        fuse_p = {(lo, n): (y4.reshape(TOP_K, n * sp, d // 2), tg_p[lo * sp:(lo + n) * sp], mod_p[l, lo:lo + n])
                  for (lo, n), y4 in zip(consumers, y4_parts + [y4_last])}
        fuse_s = (y4s.reshape(TOP_K, n_s, d // 2), tgs, mod_s[l])
        pools_p.append(jnp.concatenate([pp["npool"] for pp in parts], axis=0))
        convs_p.append(jnp.concatenate([pp["nconv"] for pp in parts], axis=0))
        pools_s.append(npool_s)
        convs_s.append(nconv_s)
        vs.append(v_s)
    yp = None
    (y4, tg, _), = fuse_p.values()
    for pp in parts:
        yp = _combine_call(pp["x"], 0, mod_p[-1, pp["lo"]:pp["lo"] + pp["n"]], y4, tg, final_g, 1, tp, bp, pp["lo"],
                           yp)
    ys_out = _combine_call(xs, 0, mod_s[-1], *fuse_s[:2], final_g, bts, ss, bs, 0)
    return (yp, ys_out, jnp.stack(pools_p), jnp.stack(convs_p), jnp.stack(pools_s), jnp.stack(convs_s),
            jnp.stack(vs))
```

```python
import functools
from typing import NamedTuple

import jax
import jax.numpy as jnp
from jax import lax
from jax.experimental import pallas as pl
from jax.experimental.pallas import tpu as pltpu
from jax.experimental.pallas import tpu_sc as plsc

GMLP_CHUNK = 128
GMLP_HEADS = 4
POOL_WINDOWS = (2, 4, 8, 16)
POOL_STATE = max(POOL_WINDOWS) - 1
CONV_K = 31
CONV_STATE = CONV_K - 1
N_BRANCH = 3
N_EXPERTS = 32
TOP_K = 4
SWIGLU_LIMIT = 7.0
SWIGLU_ALPHA = 1.702
EPS = 1e-6
PAST_LEN = 2048

LANES = 128
SUBLANES = 8
V7X_VMEM_LIMIT_BYTES = 56 * 2**20

POOL_HIST = 16
CONV_HIST = 32
ROUTER_LANES = LANES
TOPK_LANES = 8
MOE_TM = 512
PROMPT_TILE = 256
SAMPLE_BT = 8

F32 = jnp.float32
BF16 = jnp.bfloat16


def _round_up(a, m):
    return (a + m - 1) // m * m


class MixerCfg(NamedTuple):
    bt: int
    t: int
    ch: int
    kg: int
    kp: int
    d: int
    start_pos: int
    emit_v: bool
    fuse_in: bool
    steps: int
    n_tiles: int


def _rms(x, g):
    return x * lax.rsqrt(jnp.mean(x * x, axis=-1, keepdims=True) + EPS) * g


def _ln(x, g, b):
    mu = jnp.mean(x, axis=-1, keepdims=True)
    xc = x - mu
    var = jnp.mean(xc * xc, axis=-1, keepdims=True)
    return xc * lax.rsqrt(var + EPS) * g + b


def _dot(a, b):
    return jnp.dot(a, b, preferred_element_type=F32)


def _pack_rows(x):
    half = x.shape[1] // 2
    hi = lax.bitcast_convert_type(x[:, :half].astype(BF16).astype(F32), jnp.int32)
    lo = lax.bitcast_convert_type(x[:, half:].astype(BF16).astype(F32), jnp.int32)
    return hi | lax.shift_right_logical(lo, 16)


def _unpack_rows(w):
    hi = lax.bitcast_convert_type(w & jnp.int32(-65536), F32)
    lo = lax.bitcast_convert_type(lax.shift_left(w, 16), F32)
    return jnp.concatenate([hi, lo], axis=1)


def _moe_mix(y4_ref, tg):
    f = tg[:, 0:1] * _unpack_rows(y4_ref[0])
    for k in range(1, TOP_K):
        f = f + tg[:, k:k + 1] * _unpack_rows(y4_ref[k])
    return f


def _mixer_kernel(cfg, x_ref, mod_ref, modb_ref, pp_ref, pc_ref, nmg_ref, nfg_ref, win_ref, lnvg_ref, lnvb_ref,
                  wsp_ref, bsf_ref, wao_ref, wpool_ref, bpool_ref, pscale_ref, wdw_ref, bdw_ref,
                  lncg_ref, lncb_ref, wco_ref, wout_ref, wrh_ref, wrl_ref, br_ref, pmat_ref, ltri_ref, cnt0_ref,
                  *rest):
    rest = list(rest)
    if cfg.fuse_in:
        y4_ref, tgp_ref, modp_ref = rest[:3]
        rest = rest[3:]
    xo_ref, h2_ref, ti_ref, tg_ref, rk_ref, cnt_ref, npool_ref, nconv_ref = rest[:8]
    rest = rest[8:]
    if cfg.emit_v:
        v_ref, hp_scr, xc_scr, run_scr, mg_scr, xr_scr = rest
    else:
        hp_scr, xc_scr, run_scr, mg_scr, xr_scr = rest
    bt, t, d = cfg.bt, cfg.t, cfg.d
    m = bt * t
    n_slab = d // LANES
    gd = d // len(POOL_WINDOWS)
    hd = d // GMLP_HEADS
    q = pl.program_id(0)
    j = 0 if cfg.steps == 1 else lax.rem(jnp.minimum(q, cfg.n_tiles - 1), cfg.steps)

    def mod_rows(k, ref=mod_ref):
        r = ref[:, :, k * d:(k + 1) * d]
        if bt == 1:
            return r.reshape(1, d)
        return jnp.broadcast_to(r, (bt, t, d)).reshape(m, d)

    @pl.when(q == 0)
    def _init_carries():
        run_scr[...] = cnt0_ref[...]
        mg_scr[...] = jnp.zeros_like(mg_scr)
        xr_scr[...] = jnp.zeros_like(xr_scr)

    @pl.when(j == 0)
    def _load_state():
        hp_scr[...] = pp_ref[...]
        for bi in range(bt):
            for c in range(n_slab):
                xc_scr[bi, c, 0:CONV_HIST, :] = pc_ref[bi, :, c * LANES:(c + 1) * LANES]

    x_prev = xr_scr[...]
    out_prev = _dot(mg_scr[...], wout_ref[...])

    x = x_ref[...].reshape(m, d)
    if cfg.fuse_in:
        x = x + mod_rows(5, modp_ref) * _moe_mix(y4_ref, tgp_ref[...])
    sh1, sc1 = mod_rows(0), mod_rows(1)
    h = (_rms(x, nmg_ref[...]) * (1.0 + sc1) + sh1).astype(BF16)

    def zcols(lo, hi):
        return _dot(h, win_ref[:, lo:hi])

    off_u, off_v, off_b, off_c, off_g = 0, d, 2 * d, 3 * d, 5 * d
    zc_a = zcols(off_c, off_c + d)
    zc_b = zcols(off_c + d, off_c + 2 * d)

    x_new = x_prev + mod_rows(2, modb_ref) * out_prev
    xo_ref[...] = x_new.reshape(bt, t, d)
    h2 = _rms(x_new, nfg_ref[...]) * (1.0 + mod_rows(4, modb_ref)) + mod_rows(3, modb_ref)
    h2_ref[...] = _pack_rows(h2).reshape(bt, t, d // 2)
    h2_hi = h2.astype(BF16)
    h2_lo = (h2 - h2_hi.astype(F32)).astype(BF16)
    logits = _dot(h2_hi, wrh_ref[...]) + _dot(h2_lo, wrh_ref[...]) + _dot(h2_hi, wrl_ref[...]) + br_ref[...]

    x_c = zc_a * jax.nn.sigmoid(zc_b)
    for bi in range(bt):
        for c in range(n_slab):
            xc_scr[bi, c, CONV_HIST:CONV_HIST + t, :] = x_c[bi * t:(bi + 1) * t, c * LANES:(c + 1) * LANES]

    lane_r = lax.broadcasted_iota(jnp.int32, (m, ROUTER_LANES), 1).astype(F32)
    lane_k = lax.broadcasted_iota(jnp.int32, (m, TOPK_LANES), 1)
    topk = dict(work=logits, ti=jnp.zeros((m, TOPK_LANES), jnp.int32), vals=[], onehots=[])

    def next_topk_round():
        r = len(topk["vals"])
        if r < TOP_K:
            mx = jnp.max(topk["work"], axis=-1, keepdims=True)
            idx = jnp.min(jnp.where(topk["work"] == mx, lane_r, float(ROUTER_LANES)), axis=-1, keepdims=True)
            topk["ti"] = jnp.where(lane_k == r, idx.astype(jnp.int32), topk["ti"])
            topk["vals"].append(mx)
            topk["work"] = jnp.where(lane_r == idx, -jnp.inf, topk["work"])
            topk["onehots"].append((lane_r == idx).astype(F32))

    proj_cols = dict(ga=off_g, gb=off_g + d, gc=off_g + 2 * d, zu=off_u, zb=off_b, zv=off_v)
    pending_proj = list(proj_cols)
    proj = {}

    def next_proj():
        if pending_proj:
            name = pending_proj.pop(0)
            proj[name] = zcols(proj_cols[name], proj_cols[name] + d)

    conv_parts = []
    for bi in range(bt):
        accs = []
        for c in range(n_slab):
            cs = slice(c * LANES, (c + 1) * LANES)
            acc = jnp.broadcast_to(bdw_ref[:, cs], (t, LANES))
            for k in range(CONV_K):
                acc = acc + wdw_ref[k:k + 1, cs] * xc_scr[bi, c, pl.ds(CONV_HIST - CONV_STATE + k, t), :]
            accs.append(acc)
            next_proj()
            next_topk_round()
        conv_parts.append(jnp.concatenate(accs, axis=1))
    for bi in range(bt):
        for c in range(n_slab):
            cs = slice(c * LANES, (c + 1) * LANES)
            tail = xc_scr[bi, c, t:t + CONV_HIST, :]
            nconv_ref[bi, :, cs] = xc_scr[bi, c, pl.ds(t + CONV_HIST - CONV_STATE, CONV_STATE), :]
            xc_scr[bi, c, 0:CONV_HIST, :] = tail
    while pending_proj:
        next_proj()
    while len(topk["vals"]) < TOP_K:
        next_topk_round()
    conv = conv_parts[0] if bt == 1 else jnp.concatenate(conv_parts, axis=0)

    top_vals, onehots = topk["vals"], topk["onehots"]
    exps = [jnp.exp(tv - top_vals[0]) for tv in top_vals]
    denom = exps[0] + exps[1] + exps[2] + exps[3]
    tg = jnp.zeros((m, TOPK_LANES), F32)
    for r in range(TOP_K):
        tg = jnp.where(lane_k == r, exps[r] / denom, tg)
    ti_ref[...] = topk["ti"]
    tg_ref[...] = tg
    base = run_scr[...]
    rk = jnp.zeros((m, TOPK_LANES), jnp.int32)
    new_base = base
    for r in range(TOP_K):
        before = _dot(ltri_ref[...], onehots[r].astype(BF16)) + new_base
        rank = jnp.sum(onehots[r] * before, axis=-1, keepdims=True)
        rk = jnp.where(lane_k == r, rank.astype(jnp.int32), rk)
        new_base = new_base + jnp.sum(onehots[r], axis=0, keepdims=True)
    new_base = jnp.where(q >= 1, new_base, base)
    run_scr[...] = new_base
    rk_ref[...] = rk
    cnt_ref[...] = jnp.broadcast_to(new_base, cnt_ref.shape)

    zb = proj["zb"]
    lane = lax.broadcasted_iota(jnp.int32, (1, d), 1)
    win_lane = jnp.left_shift(2, lane // gd).astype(F32)
    pos1 = (cfg.start_pos + 1 + j * t + lax.broadcasted_iota(jnp.int32, (t, 1), 0)).astype(F32)
    cnt = jnp.minimum(pos1, win_lane)
    pooled_parts = []
    for bi in range(bt):
        zb_b = zb[bi * t:(bi + 1) * t]
        pieces = [hp_scr[bi], zb_b]
        if cfg.kp > POOL_HIST + t:
            pieces = [jnp.zeros((cfg.kp - POOL_HIST - t, d), F32)] + pieces
        full = jnp.concatenate(pieces, axis=0)
        fullb = full.astype(BF16)
        sums = jnp.concatenate(
            [_dot(pmat_ref[g], fullb[:, g * gd:(g + 1) * gd]) for g in range(len(POOL_WINDOWS))], axis=1)
        pooled_parts.append(sums / cnt - zb_b)
        hp_scr[bi] = full[cfg.kp - POOL_HIST:]
        npool_ref[bi] = hp_scr[bi, pl.ds(POOL_HIST - POOL_STATE, POOL_STATE), :]
    pooled = (pooled_parts[0] if bt == 1 else jnp.concatenate(pooled_parts, axis=0)).astype(BF16)
    y_b = jnp.concatenate(
        [_dot(pooled[:, g * gd:(g + 1) * gd], wpool_ref[g]) for g in range(len(POOL_WINDOWS))], axis=1)
    y_b = (y_b + bpool_ref[...]) * pscale_ref[...]

    v = _ln(jax.nn.gelu(proj["zv"]), lnvg_ref[...], lnvb_ref[...])
    if cfg.emit_v:
        v_ref[...] = v.reshape(bt, t, d)
    vb = v.astype(BF16)
    mixed_rows = []
    for r0 in range(0, m, cfg.ch):
        vc = vb[r0:r0 + cfg.ch]
        if cfg.kg > cfg.ch:
            vc = jnp.concatenate([vc, jnp.zeros((cfg.kg - cfg.ch, d), BF16)], axis=0)
        mixed_rows.append(jnp.concatenate(
            [_dot(wsp_ref[hh], vc[:, hh * hd:(hh + 1) * hd]) for hh in range(GMLP_HEADS)], axis=1) + bsf_ref[...])
    mixed = mixed_rows[0] if len(mixed_rows) == 1 else jnp.concatenate(mixed_rows, axis=0)

    y_c = _dot(jax.nn.silu(_ln(conv, lncg_ref[...], lncb_ref[...])).astype(BF16), wco_ref[...])
    u = jax.nn.gelu(proj["zu"])
    y_a = _dot((u * mixed).astype(BF16), wao_ref[...])
    merged = jax.nn.sigmoid(proj["gb"]) * y_b
    merged = merged + jax.nn.sigmoid(proj["ga"]) * y_a
    merged = merged + jax.nn.sigmoid(proj["gc"]) * y_c

    mg_scr[...] = merged.astype(BF16)
    xr_scr[...] = x


def _const_spec(shape):
    nd = len(shape)
    return pl.BlockSpec(shape, lambda q, _nd=nd: (0,) * _nd, pipeline_mode=pl.Buffered(1))


def _pool_band(t, kp):
    col = jnp.arange(kp)[None, :]
    end = (kp - t) + jnp.arange(t)[:, None]
    return jnp.stack([((col <= end) & (col > end - w)) for w in POOL_WINDOWS]).astype(BF16)


def _mixer_call(cfg, x, b0, mod, prev_pool, prev_conv, lw, cnt0, fuse=()):
    _, s, d = x.shape
    b = mod.shape[0]
    bt, t = cfg.bt, cfg.t
    m = bt * t
    steps, n_tiles = cfg.steps, cfg.n_tiles
    assert steps == s // t and n_tiles == (b // bt) * steps and b0 % bt == 0
    assert steps == 1 or t >= max(CONV_STATE, POOL_STATE)
    n_slab = d // LANES

    def front(q):
        return jnp.minimum(q, n_tiles - 1)

    def back(q):
        return jnp.maximum(q - 1, 0)

    def tile_spec(which, width=d):
        return pl.BlockSpec((bt, t, width), lambda q: (which(q) // steps, which(q) % steps, 0))

    def batch_spec(which, rows, width):
        return pl.BlockSpec((bt, rows, width), lambda q: (which(q) // steps, 0, 0))

    def rows_spec(which, width):
        return pl.BlockSpec((m, width), lambda q: (which(q), 0))

    consts = [lw["norm_mix_g"], lw["norm_ffn_g"], lw["w_in"], lw["ln_v_g"], lw["ln_v_b"], lw["w_sp"], lw["bs_full"],
              lw["w_a_out"], lw["w_pool"], lw["b_pool"], lw["pool_scale"], lw["w_dw"], lw["b_dw"], lw["ln_c_g"],
              lw["ln_c_b"], lw["w_c_out"], lw["w_out"], lw["wr_hi"], lw["wr_lo"], lw["b_router"], lw["pmat"],
              lw["ltri"], cnt0]
    x_spec = pl.BlockSpec((bt, t, d), lambda q: (front(q) // steps + b0 // bt, front(q) % steps, 0))
    in_specs = [x_spec, batch_spec(front, 1, 6 * d), batch_spec(back, 1, 6 * d),
                batch_spec(front, POOL_HIST, d), batch_spec(front, CONV_HIST, d)]
    in_specs += [_const_spec(c.shape) for c in consts]
    if cfg.fuse_in:
        in_specs += [pl.BlockSpec((TOP_K, m, d // 2), lambda q: (0, front(q), 0)),
                     rows_spec(front, TOPK_LANES), batch_spec(front, 1, 6 * d)]
    out_shape = [jax.ShapeDtypeStruct((b, s, d), F32), jax.ShapeDtypeStruct((b, s, d // 2), jnp.int32),
                 jax.ShapeDtypeStruct((b * s, TOPK_LANES), jnp.int32), jax.ShapeDtypeStruct((b * s, TOPK_LANES), F32),
                 jax.ShapeDtypeStruct((b * s, TOPK_LANES), jnp.int32),
                 jax.ShapeDtypeStruct((SUBLANES, ROUTER_LANES), F32),
                 jax.ShapeDtypeStruct((b, POOL_STATE, d), F32), jax.ShapeDtypeStruct((b, CONV_STATE, d), F32)]
    out_specs = [tile_spec(back), tile_spec(back, d // 2),
                 rows_spec(back, TOPK_LANES), rows_spec(back, TOPK_LANES), rows_spec(back, TOPK_LANES),
                 pl.BlockSpec((SUBLANES, ROUTER_LANES), lambda q: (0, 0)),
                 batch_spec(front, POOL_STATE, d), batch_spec(front, CONV_STATE, d)]
    if cfg.emit_v:
        out_shape.append(jax.ShapeDtypeStruct((b, s, d), F32))
        out_specs.append(tile_spec(front))
    return pl.pallas_call(
        functools.partial(_mixer_kernel, cfg),
        grid=(n_tiles + 1,),
        in_specs=in_specs,
        out_specs=out_specs,
        out_shape=out_shape,
        scratch_shapes=[pltpu.VMEM((bt, POOL_HIST, d), F32),
                        pltpu.VMEM((bt, n_slab, CONV_HIST + t, LANES), F32),
                        pltpu.VMEM((1, ROUTER_LANES), F32),
                        pltpu.VMEM((m, d), BF16), pltpu.VMEM((m, d), F32)],
        compiler_params=pltpu.CompilerParams(dimension_semantics=("arbitrary",),
                                             vmem_limit_bytes=V7X_VMEM_LIMIT_BYTES),
        name=f"mixer_t{t}",
    )(x, mod, mod, prev_pool, prev_conv, *consts, *fuse)


def _ada_kernel(c_ref, w_ref, b_ref, o_ref):
    o_ref[0] = _dot(jax.nn.silu(c_ref[...]).astype(BF16), w_ref[0].astype(BF16)) + b_ref[0]


def _ada_call(c_all, w_ada, b_ada):
    n_layers, d, six_d = w_ada.shape
    rows = c_all.shape[0]
    bn = six_d // 6
    return pl.pallas_call(
        _ada_kernel,
        grid=(n_layers, six_d // bn),
        in_specs=[pl.BlockSpec((rows, d), lambda l, n: (0, 0)),
                  pl.BlockSpec((1, d, bn), lambda l, n: (l, 0, n)),
                  pl.BlockSpec((1, 1, bn), lambda l, n: (l, 0, n))],
        out_specs=pl.BlockSpec((1, rows, bn), lambda l, n: (l, 0, n)),
        out_shape=jax.ShapeDtypeStruct((n_layers, rows, six_d), F32),
        compiler_params=pltpu.CompilerParams(dimension_semantics=("arbitrary", "arbitrary")),
        name="adaln",
    )(c_all, w_ada, b_ada.reshape(n_layers, 1, six_d))


def _moe_kernel(be_ref, nused_ref, x_ref, wgu_ref, bgu_ref, wdn_ref, bdn_ref, y_ref, wgu_bf, wdn_bf):
    i = pl.program_id(0)
    nused = nused_ref[0]
    dff = wdn_bf.shape[0]

    @pl.when(i < nused)
    def _body():
        changed = jnp.logical_or(i == 0, be_ref[i] != be_ref[jnp.maximum(i - 1, 0)])

        @pl.when(changed)
        def _cast_weights():
            wgu_bf[...] = wgu_ref[0].astype(BF16)
            wdn_bf[...] = wdn_ref[0].astype(BF16)

        gu = _dot(_unpack_rows(x_ref[...]).astype(BF16), wgu_bf[...]) + bgu_ref[0]
        g = jnp.minimum(gu[:, :dff], SWIGLU_LIMIT)
        u = jnp.clip(gu[:, dff:], -SWIGLU_LIMIT, SWIGLU_LIMIT)
        act = (u + 1.0) * g * jax.nn.sigmoid(SWIGLU_ALPHA * g)
        y_ref[...] = _pack_rows(_dot(act.astype(BF16), wdn_bf[...]) + bdn_ref[0])

    @pl.when(i >= nused)
    def _unused_block():
        y_ref[...] = jnp.zeros_like(y_ref)


def _moe_call(xs, block_e, nused, layer, w_gu_all, b_gu_all, w_dn_all, b_dn_all):
    n_blocks = block_e.shape[0]
    n_layers, n_exp, d, two_f = w_gu_all.shape
    dff = two_f // 2
    tm = MOE_TM
    e0 = layer * n_exp
    w_gu = w_gu_all.reshape(n_layers * n_exp, d, two_f)
    b_gu = b_gu_all.reshape(n_layers * n_exp, 1, two_f)
    w_dn = w_dn_all.reshape(n_layers * n_exp, dff, d)
    b_dn = b_dn_all.reshape(n_layers * n_exp, 1, d)
    grid_spec = pltpu.PrefetchScalarGridSpec(
        num_scalar_prefetch=2,
        grid=(n_blocks,),
        in_specs=[
            pl.BlockSpec((tm, d // 2), lambda i, be, nu: (i, 0)),
            pl.BlockSpec((1, d, two_f), lambda i, be, nu: (e0 + be[i], 0, 0)),
            pl.BlockSpec((1, 1, two_f), lambda i, be, nu: (e0 + be[i], 0, 0)),
            pl.BlockSpec((1, dff, d), lambda i, be, nu: (e0 + be[i], 0, 0)),
            pl.BlockSpec((1, 1, d), lambda i, be, nu: (e0 + be[i], 0, 0)),
        ],
        out_specs=pl.BlockSpec((tm, d // 2), lambda i, be, nu: (i, 0)),
        scratch_shapes=[pltpu.VMEM((d, two_f), BF16), pltpu.VMEM((dff, d), BF16)],
    )
    return pl.pallas_call(
        _moe_kernel,
        grid_spec=grid_spec,
        out_shape=jax.ShapeDtypeStruct((n_blocks * tm, d // 2), jnp.int32),
        compiler_params=pltpu.CompilerParams(dimension_semantics=("arbitrary",),
                                             vmem_limit_bytes=V7X_VMEM_LIMIT_BYTES),
        name="moe_experts",
    )(block_e, nused, xs, w_gu, b_gu, w_dn, b_dn)


def _route(top_i, rank, counts):
    tm = MOE_TM
    n_all = top_i.shape[0]
    n_blocks = n_all * TOP_K // tm + N_EXPERTS
    padded = (counts + tm - 1) // tm * tm
    pad_ends = jnp.cumsum(padded)
    pad_starts = pad_ends - padded
    blk = jnp.arange(n_blocks, dtype=jnp.int32)[:, None] * tm
    block_e = jnp.minimum(jnp.sum((pad_ends[None, :] <= blk).astype(jnp.int32), axis=1), N_EXPERTS - 1)
    nused = (pad_ends[-1] // tm).astype(jnp.int32).reshape(1)
    onehot = top_i[:, :, None] == jnp.arange(N_EXPERTS, dtype=jnp.int32)[None, None, :]
    slot_of = jnp.sum(jnp.where(onehot, pad_starts[None, None, :], 0), axis=-1) + rank
    return slot_of.T.astype(jnp.int32), block_e.astype(jnp.int32), nused


SC_CORES = 2
SC_SUBCORES = 16
SC_WORKERS = SC_CORES * SC_SUBCORES
SC_CHUNK = 64


def _sc_worker_id():
    return lax.axis_index("s") * SC_CORES + lax.axis_index("c")


def _sc_plan(n_rows, split=False):
    per_w = n_rows // SC_WORKERS
    ch = min(SC_CHUNK, per_w // 2 if split else per_w)
    n_ch = per_w // ch
    assert per_w * SC_WORKERS == n_rows and n_ch * ch == per_w
    return per_w, ch, n_ch, (n_ch >= 2 and n_ch % 2 == 0)


def _sc_dispatch(tables, slots, n_slots):
    nt = len(tables)
    d, dtype = tables[0].shape[1], tables[0].dtype
    plans = [_sc_plan(h.shape[0]) for h in tables]

    buf_shapes = sorted({(2 if piped else 1, ch) for (_, ch, _, piped) in plans})

    def body(*refs):
        h_refs, idx_refs, out_hbm = refs[:nt], refs[nt:2 * nt], refs[2 * nt]
        scratch = refs[2 * nt + 1:]
        idx_vs, shared, (rsem, wsem) = scratch[:nt], scratch[nt:-2], scratch[-2:]
        bufs = [shared[buf_shapes.index((2 if piped else 1, ch))] for (_, ch, _, piped) in plans]
        wid = _sc_worker_id()
        for h_hbm, idx_hbm, idx_v, buf, (tpw, ch, n_ch, piped) in zip(h_refs, idx_refs, idx_vs, bufs, plans):
            pltpu.sync_copy(idx_hbm.at[wid], idx_v)

            def read(c, b, h_hbm=h_hbm, buf=buf, tpw=tpw, ch=ch):
                return pltpu.make_async_copy(h_hbm.at[pl.ds(wid * tpw + c * ch, ch)], buf.at[b], rsem.at[b])

            def write(c, k, b, buf=buf, idx_v=idx_v):
                return pltpu.make_async_copy(buf.at[b], out_hbm.at[idx_v.at[c * TOP_K + k]], wsem.at[b])

            if piped:
                read(0, 0).start()

                @pl.loop(0, n_ch, step=2)
                def _(c0, read=read, write=write, n_ch=n_ch):
                    for b in range(2):
                        c = c0 + b

                        @pl.when(c + 1 < n_ch)
                        def _():
                            @pl.when(c >= 1)
                            def _():
                                for k in range(TOP_K):
                                    write(c - 1, k, 1 - b).wait()
                            read(c + 1, 1 - b).start()

                        read(c, b).wait()
                        for k in range(TOP_K):
                            write(c, k, b).start()

                for c in (n_ch - 2, n_ch - 1):
                    for k in range(TOP_K):
                        write(c, k, c % 2).wait()
            else:
                for c in range(n_ch):
                    cp_in = read(c, 0)
                    cp_in.start()
                    cp_in.wait()
                    for k in range(TOP_K):
                        write(c, k, 0).start()
                    for k in range(TOP_K):
                        write(c, k, 0).wait()

    scratch_types = [pltpu.VMEM((n_ch * TOP_K, ch), jnp.int32) for (_, ch, n_ch, _) in plans]
    scratch_types += [pltpu.VMEM((nbuf, ch, d), dtype) for (nbuf, ch) in buf_shapes]
    scratch_types += [pltpu.SemaphoreType.DMA((2,)), pltpu.SemaphoreType.DMA((2,))]
    call = pl.kernel(
        body,
        out_type=jax.ShapeDtypeStruct((n_slots, d), dtype),
        mesh=plsc.VectorSubcoreMesh(core_axis_name="c", subcore_axis_name="s"),
        scratch_types=scratch_types,
        name="sc_dispatch",
    )
    idx = [s.reshape(TOP_K, SC_WORKERS, n_ch, ch).transpose(1, 2, 0, 3).reshape(SC_WORKERS, n_ch * TOP_K, ch)
           for s, (_, ch, n_ch, _) in zip(slots, plans)]
    return call(*tables, *idx)


def _sc_collect(ys, slots):
    nt = len(slots)
    d, dtype = ys.shape[1], ys.dtype
    plans = [_sc_plan(s.size, split=True) for s in slots]
    assert all(piped for (_, _, _, piped) in plans)

    def body(*refs):
        ys_hbm, idx_refs, out_refs = refs[0], refs[1:1 + nt], refs[1 + nt:1 + 2 * nt]
        scratch = refs[1 + 2 * nt:]
        idx_vs, bufs, (gsem, wsem) = scratch[:nt], scratch[nt:2 * nt], scratch[2 * nt:]
        wid = _sc_worker_id()
        for idx_hbm, out_hbm, idx_v, rows_v, (per_w, ch, n_ch, _) in zip(idx_refs, out_refs, idx_vs, bufs, plans):
            base = wid * per_w
            pltpu.sync_copy(idx_hbm.at[pl.ds(base, per_w)], idx_v)

            def gather(g, b, idx_v=idx_v, rows_v=rows_v, ch=ch):
                return pltpu.make_async_copy(ys_hbm.at[idx_v.at[pl.ds(g * ch, ch)]], rows_v.at[b], gsem.at[b])

            def put(g, b, out_hbm=out_hbm, rows_v=rows_v, base=base, ch=ch):
                return pltpu.make_async_copy(rows_v.at[b], out_hbm.at[pl.ds(base + g * ch, ch)], wsem.at[b])

            gather(0, 0).start()

            @pl.loop(0, n_ch, step=2)
            def _(g0, gather=gather, put=put, n_ch=n_ch):
                for b in range(2):
                    g = g0 + b

                    @pl.when(g + 1 < n_ch)
                    def _():
                        @pl.when(g >= 1)
                        def _():
                            put(g - 1, 1 - b).wait()
                        gather(g + 1, 1 - b).start()

                    gather(g, b).wait()
                    put(g, b).start()

            for g in (n_ch - 2, n_ch - 1):
                put(g, g % 2).wait()

    call = pl.kernel(
        body,
        out_type=tuple(jax.ShapeDtypeStruct((s.size, d), dtype) for s in slots),
        mesh=plsc.VectorSubcoreMesh(core_axis_name="c", subcore_axis_name="s"),
        scratch_types=[pltpu.VMEM((per_w,), jnp.int32) for (per_w, _, _, _) in plans]
        + [pltpu.VMEM((2, ch, d), dtype) for (_, ch, _, _) in plans]
        + [pltpu.SemaphoreType.DMA((2,)), pltpu.SemaphoreType.DMA((2,))],
        name="sc_collect",
    )
    out = call(ys, *[s.reshape(s.size) for s in slots])
    return out if isinstance(out, (tuple, list)) else (out,)


def _combine_kernel(x_ref, mod_ref, y_ref, tg_ref, fg_ref, *rest):
    o_ref = rest[-1]
    bt, t, d = x_ref.shape
    m = bt * t
    g2 = mod_ref[:, :, 5 * d:6 * d]
    g2 = g2.reshape(1, d) if bt == 1 else jnp.broadcast_to(g2, (bt, t, d)).reshape(m, d)
    out = x_ref[...].reshape(m, d) + g2 * _moe_mix(y_ref, tg_ref[...])
    o_ref[...] = _rms(out, fg_ref[...]).reshape(bt, t, d)


def _combine_call(x_new, b0, mod, y4, tg, final_g, bt, t, out_batch, out_b0, out_prev=None):
    _, s, d = x_new.shape
    b = mod.shape[0]
    m = bt * t
    steps = s // t
    ob = out_b0 // bt
    in_specs = [pl.BlockSpec((bt, t, d), lambda bi, j: (bi + b0 // bt, j, 0)),
                pl.BlockSpec((bt, 1, 6 * d), lambda bi, j: (bi, 0, 0)),
                pl.BlockSpec((TOP_K, m, d // 2), lambda bi, j: (0, (bi + ob) * steps + j, 0)),
                pl.BlockSpec((m, TOPK_LANES), lambda bi, j: ((bi + ob) * steps + j, 0)),
                pl.BlockSpec((1, d), lambda bi, j: (0, 0))]
    operands = [x_new, mod, y4, tg, final_g]
    aliases = {}
    if out_prev is not None:
        in_specs.append(pl.BlockSpec(memory_space=pl.ANY))
        operands.append(out_prev)
        aliases = {len(operands) - 1: 0}
    return pl.pallas_call(
        _combine_kernel,
        grid=(b // bt, steps),
        in_specs=in_specs,
        out_specs=pl.BlockSpec((bt, t, d), lambda bi, j: (bi + out_b0 // bt, j, 0)),
        out_shape=jax.ShapeDtypeStruct((out_batch, s, d), F32),
        input_output_aliases=aliases,
        cost_estimate=pl.CostEstimate(flops=2 * (TOP_K + 3) * b * s * d, transcendentals=b * s,
                                      bytes_accessed=b * s * d * (4 + 4 + TOP_K * 2)),
        compiler_params=pltpu.CompilerParams(dimension_semantics=("arbitrary", "arbitrary")),
        name=f"combine_t{t}",
    )(*operands)


def _layer_weights(l, p, cfgs):
    d = p["w_out"].shape[-1]
    row = lambda v: v[l].reshape(1, d)
    wr = p["w_router"][l]
    wr_pad = jnp.zeros((d, ROUTER_LANES), F32).at[:, :N_EXPERTS].set(wr)
    wr_hi = wr_pad.astype(BF16)
    wr_lo = (wr_pad - wr_hi.astype(F32)).astype(BF16)
    b_router = jnp.full((1, ROUTER_LANES), -1e30, F32).at[0, :N_EXPERTS].set(p["b_router"][l])
    mask = jnp.tril(jnp.ones((GMLP_CHUNK, GMLP_CHUNK), dtype=bool))
    w_sp_full = jnp.where(mask[None], p["w_spatial"][l], 0)
    hd = d // GMLP_HEADS
    base = dict(norm_mix_g=row(p["norm_mix_g"]), norm_ffn_g=row(p["norm_ffn_g"]), w_in=p["w_in"][l].astype(BF16),
                ln_v_g=row(p["ln_v_g"]), ln_v_b=row(p["ln_v_b"]), w_a_out=p["w_a_out"][l].astype(BF16),
                w_pool=p["w_pool"][l].astype(BF16), b_pool=row(p["b_pool"]), pool_scale=row(p["pool_scale"]),
                w_dw=jnp.zeros((_round_up(CONV_K, SUBLANES), d), F32).at[:CONV_K].set(p["w_dw"][l]),
                b_dw=row(p["b_dw"]), ln_c_g=row(p["ln_c_g"]), ln_c_b=row(p["ln_c_b"]),
                w_c_out=p["w_c_out"][l].astype(BF16), w_out=p["w_out"][l].astype(BF16),
                wr_hi=wr_hi, wr_lo=wr_lo, b_router=b_router)
    out = []
    for cfg in cfgs:
        w_sp = jnp.zeros((GMLP_HEADS, cfg.ch, cfg.kg), F32).at[:, :, :cfg.ch].set(w_sp_full[:, :cfg.ch, :cfg.ch])
        bs_full = jnp.repeat(p["b_spatial"][l][:, :cfg.ch].T, hd, axis=1)
        m = cfg.bt * cfg.t
        ltri = (jnp.arange(m)[:, None] > jnp.arange(m)[None, :]).astype(BF16)
        out.append(dict(base, w_sp=w_sp.astype(BF16), bs_full=bs_full, pmat=_pool_band(cfg.t, cfg.kp), ltri=ltri))
    return out


def _pad_front(state, rows):
    pad = rows - state.shape[-2]
    return jnp.pad(state, ((0, 0),) * (state.ndim - 2) + ((pad, 0), (0, 0)))


def kernel(x_prompt, x_sample, c_prompt, c_sample, state_pool, state_conv, norm_mix_g, norm_ffn_g, w_ada, b_ada, w_in, ln_v_g, ln_v_b, w_spatial, b_spatial, w_a_out, w_pool, b_pool, pool_scale, w_dw, b_dw, ln_c_g, ln_c_b, w_c_out, w_out, w_router, b_router, w_gate_up, b_gate_up, w_down, b_down, final_norm_g):
    p = dict(norm_mix_g=norm_mix_g, norm_ffn_g=norm_ffn_g, w_in=w_in, ln_v_g=ln_v_g, ln_v_b=ln_v_b,
             w_spatial=w_spatial, b_spatial=b_spatial, w_a_out=w_a_out, w_pool=w_pool, b_pool=b_pool,
             pool_scale=pool_scale, w_dw=w_dw, b_dw=b_dw, ln_c_g=ln_c_g, ln_c_b=ln_c_b, w_c_out=w_c_out,
             w_out=w_out, w_router=w_router, b_router=b_router)
    n_layers = w_in.shape[0]
    bp, sp, d = x_prompt.shape
    bs, ss, _ = x_sample.shape
    tp = min(PROMPT_TILE, sp)
    bts = min(SAMPLE_BT, bs)
    cfg_p = MixerCfg(bt=1, t=tp, ch=GMLP_CHUNK, kg=GMLP_CHUNK, kp=_round_up(POOL_HIST + tp, LANES), d=d,
                     start_pos=0, emit_v=False, fuse_in=False, steps=sp // tp, n_tiles=bp * (sp // tp))
    cfg_s = MixerCfg(bt=bts, t=ss, ch=ss, kg=_round_up(ss, LANES), kp=_round_up(POOL_HIST + ss, LANES), d=d,
                     start_pos=PAST_LEN, emit_v=True, fuse_in=False, steps=1, n_tiles=bs // bts)
    n_p, n_s = bp * sp, bs * ss
    n_slots = (n_p + n_s) * TOP_K + N_EXPERTS * MOE_TM

    mod_all = _ada_call(jnp.concatenate([c_prompt, c_sample], axis=0), w_ada, b_ada)
    mod_p = mod_all[:, :bp].reshape(n_layers, bp, 1, 6 * d)
    mod_s = mod_all[:, bp:].reshape(n_layers, bs, 1, 6 * d)
    final_g = final_norm_g.reshape(1, d)

    hb = bp // 2
    halves = [(0, hb), (hb, bp - hb)]
    steps_p = sp // tp
    xs = x_sample
    parts = [dict(lo=0, n=bp, x=x_prompt)]
    zero_pool = jnp.zeros((bp, POOL_HIST, d), F32)
    zero_conv = jnp.zeros((bp, CONV_HIST, d), F32)
    zero_cnt = jnp.zeros((1, ROUTER_LANES), F32)
    fuse_p, fuse_s = {}, ()
    pools_p, convs_p, pools_s, convs_s, vs = [], [], [], [], []

    def holder(lo):
        return next(pp for pp in parts if pp["lo"] <= lo < pp["lo"] + pp["n"])

    for l in range(n_layers):
        fused = l > 0
        lw_p, lw_s = _layer_weights(l, p, (cfg_p, cfg_s))
        cnt = zero_cnt
        new_parts = []
        for lo, n in (halves if fused else [(0, bp)]):
            src = holder(lo)
            x_new, h2, ti, tg, rk, cnt, npool, nconv = _mixer_call(
                cfg_p._replace(fuse_in=fused, n_tiles=n * steps_p), src["x"], lo - src["lo"], mod_p[l, lo:lo + n],
                zero_pool[lo:lo + n], zero_conv[lo:lo + n], lw_p, cnt[0:1], fuse_p.get((lo, n), ()))
            new_parts.append(dict(lo=lo, n=n, x=x_new, h2=h2.reshape(n * sp, d // 2), ti=ti, tg=tg, rk=rk,
                                  npool=npool, nconv=nconv))
        parts = new_parts
        xs, h2s, tis, tgs, rks, cnt_all, npool_s, nconv_s, v_s = _mixer_call(
            cfg_s._replace(fuse_in=fused), xs, 0, mod_s[l], _pad_front(state_pool[l], POOL_HIST),
            _pad_front(state_conv[l], CONV_HIST), lw_s, cnt[0:1], fuse_s)
        top_i = jnp.concatenate([pp["ti"] for pp in parts] + [tis], axis=0)[:, :TOP_K]
        rank = jnp.concatenate([pp["rk"] for pp in parts] + [rks], axis=0)[:, :TOP_K]
        tg_p = jnp.concatenate([pp["tg"] for pp in parts], axis=0)
        slot_of, block_e, nused = _route(top_i, rank, cnt_all[0, :N_EXPERTS].astype(jnp.int32))
        tables = [pp["h2"] for pp in parts] + [h2s.reshape(n_s, d // 2)]
        bounds = [pp["lo"] * sp for pp in parts] + [n_p, n_p + n_s]
        slots = [slot_of[:, bounds[i]:bounds[i + 1]] for i in range(len(tables))]
        x_sorted = _sc_dispatch(tables, slots, n_slots)
        ys = _moe_call(x_sorted, block_e, nused, l, w_gate_up, b_gate_up, w_down, b_down)
        consumers = halves if l < n_layers - 1 else [(0, bp)]
        y4_parts = []
        for lo, n in consumers[:-1]:
            y4_parts += _sc_collect(ys, [slot_of[:, lo * sp:(lo + n) * sp]])
        lo, n = consumers[-1]
        y4_last, y4s = _sc_collect(ys, [slot_of[:, lo * sp:(lo + n) * sp], slot_of[:, n_p:]])
        fuse_p = {(lo, n): (y4.reshape(TOP_K, n * sp, d // 2), tg_p[lo * sp:(lo + n) * sp], mod_p[l, lo:lo + n])
                  for (lo, n), y4 in zip(consumers, y4_parts + [y4_last])}
        fuse_s = (y4s.reshape(TOP_K, n_s, d // 2), tgs, mod_s[l])
        pools_p.append(jnp.concatenate([pp["npool"] for pp in parts], axis=0))
        convs_p.append(jnp.concatenate([pp["nconv"] for pp in parts], axis=0))
        pools_s.append(npool_s)
        convs_s.append(nconv_s)
        vs.append(v_s)
    yp = None
    (y4, tg, _), = fuse_p.values()
    for pp in parts:
        yp = _combine_call(pp["x"], 0, mod_p[-1, pp["lo"]:pp["lo"] + pp["n"]], y4, tg, final_g, 1, tp, bp, pp["lo"],
                           yp)
    ys_out = _combine_call(xs, 0, mod_s[-1], *fuse_s[:2], final_g, bts, ss, bs, 0)
    return (yp, ys_out, jnp.stack(pools_p), jnp.stack(convs_p), jnp.stack(pools_s), jnp.stack(convs_s),
            jnp.stack(vs))
```

```python
import functools
from typing import NamedTuple

import jax
import jax.numpy as jnp
from jax import lax
from jax.experimental import pallas as pl
from jax.experimental.pallas import tpu as pltpu
from jax.experimental.pallas import tpu_sc as plsc

GMLP_CHUNK = 128
GMLP_HEADS = 4
POOL_WINDOWS = (2, 4, 8, 16)
POOL_STATE = max(POOL_WINDOWS) - 1
CONV_K = 31
CONV_STATE = CONV_K - 1
N_BRANCH = 3
N_EXPERTS = 32
TOP_K = 4
SWIGLU_LIMIT = 7.0
SWIGLU_ALPHA = 1.702
EPS = 1e-6
PAST_LEN = 2048

LANES = 128
SUBLANES = 8
V7X_VMEM_LIMIT_BYTES = 56 * 2**20

POOL_HIST = 16
CONV_HIST = 32
ROUTER_LANES = LANES
TOPK_LANES = 8
MOE_TM = 512
PROMPT_TILE = 256
SAMPLE_BT = 8

F32 = jnp.float32
BF16 = jnp.bfloat16


def _round_up(a, m):
    return (a + m - 1) // m * m


class MixerCfg(NamedTuple):
    bt: int
    t: int
    ch: int
    kg: int
    kp: int
    d: int
    start_pos: int
    emit_v: bool
    fuse_in: bool
    steps: int
    n_tiles: int


def _rms(x, g):
    return x * lax.rsqrt(jnp.mean(x * x, axis=-1, keepdims=True) + EPS) * g


def _ln(x, g, b):
    mu = jnp.mean(x, axis=-1, keepdims=True)
    xc = x - mu
    var = jnp.mean(xc * xc, axis=-1, keepdims=True)
    return xc * lax.rsqrt(var + EPS) * g + b


def _dot(a, b):
    return jnp.dot(a, b, preferred_element_type=F32)


def _pack_rows(x):
    half = x.shape[1] // 2
    hi = lax.bitcast_convert_type(x[:, :half].astype(BF16).astype(F32), jnp.int32)
    lo = lax.bitcast_convert_type(x[:, half:].astype(BF16).astype(F32), jnp.int32)
    return hi | lax.shift_right_logical(lo, 16)


def _unpack_rows(w):
    hi = lax.bitcast_convert_type(w & jnp.int32(-65536), F32)
    lo = lax.bitcast_convert_type(lax.shift_left(w, 16), F32)
    return jnp.concatenate([hi, lo], axis=1)


def _moe_mix(y4_ref, tg):
    f = tg[:, 0:1] * _unpack_rows(y4_ref[0])
    for k in range(1, TOP_K):
        f = f + tg[:, k:k + 1] * _unpack_rows(y4_ref[k])
    return f


def _mixer_kernel(cfg, x_ref, mod_ref, modb_ref, pp_ref, pc_ref, nmg_ref, nfg_ref, win_ref, lnvg_ref, lnvb_ref,
                  wsp_ref, bsf_ref, wao_ref, wpool_ref, bpool_ref, pscale_ref, wdw_ref, bdw_ref,
                  lncg_ref, lncb_ref, wco_ref, wout_ref, wrh_ref, wrl_ref, br_ref, pmat_ref, ltri_ref, cnt0_ref,
                  *rest):
    rest = list(rest)
    if cfg.fuse_in:
        y4_ref, tgp_ref, modp_ref = rest[:3]
        rest = rest[3:]
    xo_ref, h2_ref, ti_ref, tg_ref, rk_ref, cnt_ref, npool_ref, nconv_ref = rest[:8]
    rest = rest[8:]
    if cfg.emit_v:
        v_ref, hp_scr, xc_scr, run_scr, mg_scr, xr_scr = rest
    else:
        hp_scr, xc_scr, run_scr, mg_scr, xr_scr = rest
    bt, t, d = cfg.bt, cfg.t, cfg.d
    m = bt * t
    n_slab = d // LANES
    gd = d // len(POOL_WINDOWS)
    hd = d // GMLP_HEADS
    q = pl.program_id(0)
    j = 0 if cfg.steps == 1 else lax.rem(jnp.minimum(q, cfg.n_tiles - 1), cfg.steps)

    def mod_rows(k, ref=mod_ref):
        r = ref[:, :, k * d:(k + 1) * d]
        if bt == 1:
            return r.reshape(1, d)
        return jnp.broadcast_to(r, (bt, t, d)).reshape(m, d)

    @pl.when(q == 0)
    def _init_carries():
        run_scr[...] = cnt0_ref[...]
        mg_scr[...] = jnp.zeros_like(mg_scr)
        xr_scr[...] = jnp.zeros_like(xr_scr)

    @pl.when(j == 0)
    def _load_state():
        hp_scr[...] = pp_ref[...]
        for bi in range(bt):
            for c in range(n_slab):
                xc_scr[bi, c, 0:CONV_HIST, :] = pc_ref[bi, :, c * LANES:(c + 1) * LANES]

    x_prev = xr_scr[...]
    out_prev = _dot(mg_scr[...], wout_ref[...])

    x = x_ref[...].reshape(m, d)
    if cfg.fuse_in:
        x = x + mod_rows(5, modp_ref) * _moe_mix(y4_ref, tgp_ref[...])
    sh1, sc1 = mod_rows(0), mod_rows(1)
    h = (_rms(x, nmg_ref[...]) * (1.0 + sc1) + sh1).astype(BF16)

    def zcols(lo, hi):
        return _dot(h, win_ref[:, lo:hi])

    off_u, off_v, off_b, off_c, off_g = 0, d, 2 * d, 3 * d, 5 * d
    zc_a = zcols(off_c, off_c + d)
    zc_b = zcols(off_c + d, off_c + 2 * d)

    x_new = x_prev + mod_rows(2, modb_ref) * out_prev
    xo_ref[...] = x_new.reshape(bt, t, d)
    h2 = _rms(x_new, nfg_ref[...]) * (1.0 + mod_rows(4, modb_ref)) + mod_rows(3, modb_ref)
    h2_ref[...] = _pack_rows(h2).reshape(bt, t, d // 2)
    h2_hi = h2.astype(BF16)
    h2_lo = (h2 - h2_hi.astype(F32)).astype(BF16)
    logits = _dot(h2_hi, wrh_ref[...]) + _dot(h2_lo, wrh_ref[...]) + _dot(h2_hi, wrl_ref[...]) + br_ref[...]

    x_c = zc_a * jax.nn.sigmoid(zc_b)
    for bi in range(bt):
        for c in range(n_slab):
            xc_scr[bi, c, CONV_HIST:CONV_HIST + t, :] = x_c[bi * t:(bi + 1) * t, c * LANES:(c + 1) * LANES]

    lane_r = lax.broadcasted_iota(jnp.int32, (m, ROUTER_LANES), 1).astype(F32)
    lane_k = lax.broadcasted_iota(jnp.int32, (m, TOPK_LANES), 1)
    topk = dict(work=logits, ti=jnp.zeros((m, TOPK_LANES), jnp.int32), vals=[], onehots=[])

    def next_topk_round():
        r = len(topk["vals"])
        if r < TOP_K:
            mx = jnp.max(topk["work"], axis=-1, keepdims=True)
            idx = jnp.min(jnp.where(topk["work"] == mx, lane_r, float(ROUTER_LANES)), axis=-1, keepdims=True)
            topk["ti"] = jnp.where(lane_k == r, idx.astype(jnp.int32), topk["ti"])
            topk["vals"].append(mx)
            topk["work"] = jnp.where(lane_r == idx, -jnp.inf, topk["work"])
            topk["onehots"].append((lane_r == idx).astype(F32))

    proj_cols = dict(ga=off_g, gb=off_g + d, gc=off_g + 2 * d, zu=off_u, zv=off_v, zb=off_b)
    pending_proj = list(proj_cols)
    proj = {}

    def next_proj():
        if pending_proj:
            name = pending_proj.pop(0)
            proj[name] = zcols(proj_cols[name], proj_cols[name] + d)

    conv_parts = []
    for bi in range(bt):
        accs = []
        for c in range(n_slab):
            cs = slice(c * LANES, (c + 1) * LANES)
            acc = jnp.broadcast_to(bdw_ref[:, cs], (t, LANES))
            for k in range(CONV_K):
                acc = acc + wdw_ref[k:k + 1, cs] * xc_scr[bi, c, pl.ds(CONV_HIST - CONV_STATE + k, t), :]
            accs.append(acc)
            next_proj()
            next_topk_round()
        conv_parts.append(jnp.concatenate(accs, axis=1))
    for bi in range(bt):
        for c in range(n_slab):
            cs = slice(c * LANES, (c + 1) * LANES)
            tail = xc_scr[bi, c, t:t + CONV_HIST, :]
            nconv_ref[bi, :, cs] = xc_scr[bi, c, pl.ds(t + CONV_HIST - CONV_STATE, CONV_STATE), :]
            xc_scr[bi, c, 0:CONV_HIST, :] = tail
    while pending_proj:
        next_proj()
    while len(topk["vals"]) < TOP_K:
        next_topk_round()
    conv = conv_parts[0] if bt == 1 else jnp.concatenate(conv_parts, axis=0)

    top_vals, onehots = topk["vals"], topk["onehots"]
    exps = [jnp.exp(tv - top_vals[0]) for tv in top_vals]
    denom = exps[0] + exps[1] + exps[2] + exps[3]
    tg = jnp.zeros((m, TOPK_LANES), F32)
    for r in range(TOP_K):
        tg = jnp.where(lane_k == r, exps[r] / denom, tg)
    ti_ref[...] = topk["ti"]
    tg_ref[...] = tg
    base = run_scr[...]
    rk = jnp.zeros((m, TOPK_LANES), jnp.int32)
    new_base = base
    for r in range(TOP_K):
        before = _dot(ltri_ref[...], onehots[r].astype(BF16)) + new_base
        rank = jnp.sum(onehots[r] * before, axis=-1, keepdims=True)
        rk = jnp.where(lane_k == r, rank.astype(jnp.int32), rk)
        new_base = new_base + jnp.sum(onehots[r], axis=0, keepdims=True)
    new_base = jnp.where(q >= 1, new_base, base)
    run_scr[...] = new_base
    rk_ref[...] = rk
    cnt_ref[...] = jnp.broadcast_to(new_base, cnt_ref.shape)

    zb = proj["zb"]
    lane = lax.broadcasted_iota(jnp.int32, (1, d), 1)
    win_lane = jnp.left_shift(2, lane // gd).astype(F32)
    pos1 = (cfg.start_pos + 1 + j * t + lax.broadcasted_iota(jnp.int32, (t, 1), 0)).astype(F32)
    cnt = jnp.minimum(pos1, win_lane)
    pooled_parts = []
    for bi in range(bt):
        zb_b = zb[bi * t:(bi + 1) * t]
        pieces = [hp_scr[bi], zb_b]
        if cfg.kp > POOL_HIST + t:
            pieces = [jnp.zeros((cfg.kp - POOL_HIST - t, d), F32)] + pieces
        full = jnp.concatenate(pieces, axis=0)
        fullb = full.astype(BF16)
        sums = jnp.concatenate(
            [_dot(pmat_ref[g], fullb[:, g * gd:(g + 1) * gd]) for g in range(len(POOL_WINDOWS))], axis=1)
        pooled_parts.append(sums / cnt - zb_b)
        hp_scr[bi] = full[cfg.kp - POOL_HIST:]
        npool_ref[bi] = hp_scr[bi, pl.ds(POOL_HIST - POOL_STATE, POOL_STATE), :]
    pooled = (pooled_parts[0] if bt == 1 else jnp.concatenate(pooled_parts, axis=0)).astype(BF16)
    y_b = jnp.concatenate(
        [_dot(pooled[:, g * gd:(g + 1) * gd], wpool_ref[g]) for g in range(len(POOL_WINDOWS))], axis=1)
    y_b = (y_b + bpool_ref[...]) * pscale_ref[...]

    v = _ln(jax.nn.gelu(proj["zv"]), lnvg_ref[...], lnvb_ref[...])
    if cfg.emit_v:
        v_ref[...] = v.reshape(bt, t, d)
    vb = v.astype(BF16)
    mixed_rows = []
    for r0 in range(0, m, cfg.ch):
        vc = vb[r0:r0 + cfg.ch]
        if cfg.kg > cfg.ch:
            vc = jnp.concatenate([vc, jnp.zeros((cfg.kg - cfg.ch, d), BF16)], axis=0)
        mixed_rows.append(jnp.concatenate(
            [_dot(wsp_ref[hh], vc[:, hh * hd:(hh + 1) * hd]) for hh in range(GMLP_HEADS)], axis=1) + bsf_ref[...])
    mixed = mixed_rows[0] if len(mixed_rows) == 1 else jnp.concatenate(mixed_rows, axis=0)

    y_c = _dot(jax.nn.silu(_ln(conv, lncg_ref[...], lncb_ref[...])).astype(BF16), wco_ref[...])
    u = jax.nn.gelu(proj["zu"])
    y_a = _dot((u * mixed).astype(BF16), wao_ref[...])
    merged = jax.nn.sigmoid(proj["gb"]) * y_b
    merged = merged + jax.nn.sigmoid(proj["ga"]) * y_a
    merged = merged + jax.nn.sigmoid(proj["gc"]) * y_c

    mg_scr[...] = merged.astype(BF16)
    xr_scr[...] = x


def _const_spec(shape):
    nd = len(shape)
    return pl.BlockSpec(shape, lambda q, _nd=nd: (0,) * _nd, pipeline_mode=pl.Buffered(1))


def _pool_band(t, kp):
    col = jnp.arange(kp)[None, :]
    end = (kp - t) + jnp.arange(t)[:, None]
    return jnp.stack([((col <= end) & (col > end - w)) for w in POOL_WINDOWS]).astype(BF16)


def _mixer_call(cfg, x, b0, mod, prev_pool, prev_conv, lw, cnt0, fuse=()):
    _, s, d = x.shape
    b = mod.shape[0]
    bt, t = cfg.bt, cfg.t
    m = bt * t
    steps, n_tiles = cfg.steps, cfg.n_tiles
    assert steps == s // t and n_tiles == (b // bt) * steps and b0 % bt == 0
    assert steps == 1 or t >= max(CONV_STATE, POOL_STATE)
    n_slab = d // LANES

    def front(q):
        return jnp.minimum(q, n_tiles - 1)

    def back(q):
        return jnp.maximum(q - 1, 0)

    def tile_spec(which, width=d):
        return pl.BlockSpec((bt, t, width), lambda q: (which(q) // steps, which(q) % steps, 0))

    def batch_spec(which, rows, width):
        return pl.BlockSpec((bt, rows, width), lambda q: (which(q) // steps, 0, 0))

    def rows_spec(which, width):
        return pl.BlockSpec((m, width), lambda q: (which(q), 0))

    consts = [lw["norm_mix_g"], lw["norm_ffn_g"], lw["w_in"], lw["ln_v_g"], lw["ln_v_b"], lw["w_sp"], lw["bs_full"],
              lw["w_a_out"], lw["w_pool"], lw["b_pool"], lw["pool_scale"], lw["w_dw"], lw["b_dw"], lw["ln_c_g"],
              lw["ln_c_b"], lw["w_c_out"], lw["w_out"], lw["wr_hi"], lw["wr_lo"], lw["b_router"], lw["pmat"],
              lw["ltri"], cnt0]
    x_spec = pl.BlockSpec((bt, t, d), lambda q: (front(q) // steps + b0 // bt, front(q) % steps, 0))
    in_specs = [x_spec, batch_spec(front, 1, 6 * d), batch_spec(back, 1, 6 * d),
                batch_spec(front, POOL_HIST, d), batch_spec(front, CONV_HIST, d)]
    in_specs += [_const_spec(c.shape) for c in consts]
    if cfg.fuse_in:
        in_specs += [pl.BlockSpec((TOP_K, m, d // 2), lambda q: (0, front(q), 0)),
                     rows_spec(front, TOPK_LANES), batch_spec(front, 1, 6 * d)]
    out_shape = [jax.ShapeDtypeStruct((b, s, d), F32), jax.ShapeDtypeStruct((b, s, d // 2), jnp.int32),
                 jax.ShapeDtypeStruct((b * s, TOPK_LANES), jnp.int32), jax.ShapeDtypeStruct((b * s, TOPK_LANES), F32),
                 jax.ShapeDtypeStruct((b * s, TOPK_LANES), jnp.int32),
                 jax.ShapeDtypeStruct((SUBLANES, ROUTER_LANES), F32),
                 jax.ShapeDtypeStruct((b, POOL_STATE, d), F32), jax.ShapeDtypeStruct((b, CONV_STATE, d), F32)]
    out_specs = [tile_spec(back), tile_spec(back, d // 2),
                 rows_spec(back, TOPK_LANES), rows_spec(back, TOPK_LANES), rows_spec(back, TOPK_LANES),
                 pl.BlockSpec((SUBLANES, ROUTER_LANES), lambda q: (0, 0)),
                 batch_spec(front, POOL_STATE, d), batch_spec(front, CONV_STATE, d)]
    if cfg.emit_v:
        out_shape.append(jax.ShapeDtypeStruct((b, s, d), F32))
        out_specs.append(tile_spec(front))
    return pl.pallas_call(
        functools.partial(_mixer_kernel, cfg),
        grid=(n_tiles + 1,),
        in_specs=in_specs,
        out_specs=out_specs,
        out_shape=out_shape,
        scratch_shapes=[pltpu.VMEM((bt, POOL_HIST, d), F32),
                        pltpu.VMEM((bt, n_slab, CONV_HIST + t, LANES), F32),
                        pltpu.VMEM((1, ROUTER_LANES), F32),
                        pltpu.VMEM((m, d), BF16), pltpu.VMEM((m, d), F32)],
        compiler_params=pltpu.CompilerParams(dimension_semantics=("arbitrary",),
                                             vmem_limit_bytes=V7X_VMEM_LIMIT_BYTES),
        name=f"mixer_t{t}",
    )(x, mod, mod, prev_pool, prev_conv, *consts, *fuse)


def _ada_kernel(c_ref, w_ref, b_ref, o_ref):
    o_ref[0] = _dot(jax.nn.silu(c_ref[...]).astype(BF16), w_ref[0].astype(BF16)) + b_ref[0]


def _ada_call(c_all, w_ada, b_ada):
    n_layers, d, six_d = w_ada.shape
    rows = c_all.shape[0]
    bn = six_d // 6
    return pl.pallas_call(
        _ada_kernel,
        grid=(n_layers, six_d // bn),
        in_specs=[pl.BlockSpec((rows, d), lambda l, n: (0, 0)),
                  pl.BlockSpec((1, d, bn), lambda l, n: (l, 0, n)),
                  pl.BlockSpec((1, 1, bn), lambda l, n: (l, 0, n))],
        out_specs=pl.BlockSpec((1, rows, bn), lambda l, n: (l, 0, n)),
        out_shape=jax.ShapeDtypeStruct((n_layers, rows, six_d), F32),
        compiler_params=pltpu.CompilerParams(dimension_semantics=("arbitrary", "arbitrary")),
        name="adaln",
    )(c_all, w_ada, b_ada.reshape(n_layers, 1, six_d))


def _moe_kernel(be_ref, nused_ref, x_ref, wgu_ref, bgu_ref, wdn_ref, bdn_ref, y_ref, wgu_bf, wdn_bf):
    i = pl.program_id(0)
    nused = nused_ref[0]
    dff = wdn_bf.shape[0]

    @pl.when(i < nused)
    def _body():
        changed = jnp.logical_or(i == 0, be_ref[i] != be_ref[jnp.maximum(i - 1, 0)])

        @pl.when(changed)
        def _cast_weights():
            wgu_bf[...] = wgu_ref[0].astype(BF16)
            wdn_bf[...] = wdn_ref[0].astype(BF16)

        gu = _dot(_unpack_rows(x_ref[...]).astype(BF16), wgu_bf[...]) + bgu_ref[0]
        g = jnp.minimum(gu[:, :dff], SWIGLU_LIMIT)
        u = jnp.clip(gu[:, dff:], -SWIGLU_LIMIT, SWIGLU_LIMIT)
        act = (u + 1.0) * g * jax.nn.sigmoid(SWIGLU_ALPHA * g)
        y_ref[...] = _pack_rows(_dot(act.astype(BF16), wdn_bf[...]) + bdn_ref[0])

    @pl.when(i >= nused)
    def _unused_block():
        y_ref[...] = jnp.zeros_like(y_ref)


def _moe_call(xs, block_e, nused, layer, w_gu_all, b_gu_all, w_dn_all, b_dn_all):
    n_blocks = block_e.shape[0]
    n_layers, n_exp, d, two_f = w_gu_all.shape
    dff = two_f // 2
    tm = MOE_TM
    e0 = layer * n_exp
    w_gu = w_gu_all.reshape(n_layers * n_exp, d, two_f)
    b_gu = b_gu_all.reshape(n_layers * n_exp, 1, two_f)
    w_dn = w_dn_all.reshape(n_layers * n_exp, dff, d)
    b_dn = b_dn_all.reshape(n_layers * n_exp, 1, d)
    grid_spec = pltpu.PrefetchScalarGridSpec(
        num_scalar_prefetch=2,
        grid=(n_blocks,),
        in_specs=[
            pl.BlockSpec((tm, d // 2), lambda i, be, nu: (i, 0)),
            pl.BlockSpec((1, d, two_f), lambda i, be, nu: (e0 + be[i], 0, 0)),
            pl.BlockSpec((1, 1, two_f), lambda i, be, nu: (e0 + be[i], 0, 0)),
            pl.BlockSpec((1, dff, d), lambda i, be, nu: (e0 + be[i], 0, 0)),
            pl.BlockSpec((1, 1, d), lambda i, be, nu: (e0 + be[i], 0, 0)),
        ],
        out_specs=pl.BlockSpec((tm, d // 2), lambda i, be, nu: (i, 0)),
        scratch_shapes=[pltpu.VMEM((d, two_f), BF16), pltpu.VMEM((dff, d), BF16)],
    )
    return pl.pallas_call(
        _moe_kernel,
        grid_spec=grid_spec,
        out_shape=jax.ShapeDtypeStruct((n_blocks * tm, d // 2), jnp.int32),
        compiler_params=pltpu.CompilerParams(dimension_semantics=("arbitrary",),
                                             vmem_limit_bytes=V7X_VMEM_LIMIT_BYTES),
        name="moe_experts",
    )(block_e, nused, xs, w_gu, b_gu, w_dn, b_dn)


def _route(top_i, rank, counts):
    tm = MOE_TM
    n_all = top_i.shape[0]
    n_blocks = n_all * TOP_K // tm + N_EXPERTS
    padded = (counts + tm - 1) // tm * tm
    pad_ends = jnp.cumsum(padded)
    pad_starts = pad_ends - padded
    blk = jnp.arange(n_blocks, dtype=jnp.int32)[:, None] * tm
    block_e = jnp.minimum(jnp.sum((pad_ends[None, :] <= blk).astype(jnp.int32), axis=1), N_EXPERTS - 1)
    nused = (pad_ends[-1] // tm).astype(jnp.int32).reshape(1)
    onehot = top_i[:, :, None] == jnp.arange(N_EXPERTS, dtype=jnp.int32)[None, None, :]
    slot_of = jnp.sum(jnp.where(onehot, pad_starts[None, None, :], 0), axis=-1) + rank
    return slot_of.T.astype(jnp.int32), block_e.astype(jnp.int32), nused


SC_CORES = 2
SC_SUBCORES = 16
SC_WORKERS = SC_CORES * SC_SUBCORES
SC_CHUNK = 64


def _sc_worker_id():
    return lax.axis_index("s") * SC_CORES + lax.axis_index("c")


def _sc_plan(n_rows, split=False):
    per_w = n_rows // SC_WORKERS
    ch = min(SC_CHUNK, per_w // 2 if split else per_w)
    n_ch = per_w // ch
    assert per_w * SC_WORKERS == n_rows and n_ch * ch == per_w
    return per_w, ch, n_ch, (n_ch >= 2 and n_ch % 2 == 0)


def _sc_dispatch(tables, slots, n_slots):
    nt = len(tables)
    d, dtype = tables[0].shape[1], tables[0].dtype
    plans = [_sc_plan(h.shape[0]) for h in tables]

    buf_shapes = sorted({(2 if piped else 1, ch) for (_, ch, _, piped) in plans})

    def body(*refs):
        h_refs, idx_refs, out_hbm = refs[:nt], refs[nt:2 * nt], refs[2 * nt]
        scratch = refs[2 * nt + 1:]
        idx_vs, shared, (rsem, wsem) = scratch[:nt], scratch[nt:-2], scratch[-2:]
        bufs = [shared[buf_shapes.index((2 if piped else 1, ch))] for (_, ch, _, piped) in plans]
        wid = _sc_worker_id()
        for h_hbm, idx_hbm, idx_v, buf, (tpw, ch, n_ch, piped) in zip(h_refs, idx_refs, idx_vs, bufs, plans):
            pltpu.sync_copy(idx_hbm.at[wid], idx_v)

            def read(c, b, h_hbm=h_hbm, buf=buf, tpw=tpw, ch=ch):
                return pltpu.make_async_copy(h_hbm.at[pl.ds(wid * tpw + c * ch, ch)], buf.at[b], rsem.at[b])

            def write(c, k, b, buf=buf, idx_v=idx_v):
                return pltpu.make_async_copy(buf.at[b], out_hbm.at[idx_v.at[c * TOP_K + k]], wsem.at[b])

            if piped:
                read(0, 0).start()

                @pl.loop(0, n_ch, step=2)
                def _(c0, read=read, write=write, n_ch=n_ch):
                    for b in range(2):
                        c = c0 + b

                        @pl.when(c + 1 < n_ch)
                        def _():
                            @pl.when(c >= 1)
                            def _():
                                for k in range(TOP_K):
                                    write(c - 1, k, 1 - b).wait()
                            read(c + 1, 1 - b).start()

                        read(c, b).wait()
                        for k in range(TOP_K):
                            write(c, k, b).start()

                for c in (n_ch - 2, n_ch - 1):
                    for k in range(TOP_K):
                        write(c, k, c % 2).wait()
            else:
                for c in range(n_ch):
                    cp_in = read(c, 0)
                    cp_in.start()
                    cp_in.wait()
                    for k in range(TOP_K):
                        write(c, k, 0).start()
                    for k in range(TOP_K):
                        write(c, k, 0).wait()

    scratch_types = [pltpu.VMEM((n_ch * TOP_K, ch), jnp.int32) for (_, ch, n_ch, _) in plans]
    scratch_types += [pltpu.VMEM((nbuf, ch, d), dtype) for (nbuf, ch) in buf_shapes]
    scratch_types += [pltpu.SemaphoreType.DMA((2,)), pltpu.SemaphoreType.DMA((2,))]
    call = pl.kernel(
        body,
        out_type=jax.ShapeDtypeStruct((n_slots, d), dtype),
        mesh=plsc.VectorSubcoreMesh(core_axis_name="c", subcore_axis_name="s"),
        scratch_types=scratch_types,
        name="sc_dispatch",
    )
    idx = [s.reshape(TOP_K, SC_WORKERS, n_ch, ch).transpose(1, 2, 0, 3).reshape(SC_WORKERS, n_ch * TOP_K, ch)
           for s, (_, ch, n_ch, _) in zip(slots, plans)]
    return call(*tables, *idx)


def _sc_collect(ys, slots):
    nt = len(slots)
    d, dtype = ys.shape[1], ys.dtype
    plans = [_sc_plan(s.size, split=True) for s in slots]
    assert all(piped for (_, _, _, piped) in plans)

    def body(*refs):
        ys_hbm, idx_refs, out_refs = refs[0], refs[1:1 + nt], refs[1 + nt:1 + 2 * nt]
        scratch = refs[1 + 2 * nt:]
        idx_vs, bufs, (gsem, wsem) = scratch[:nt], scratch[nt:2 * nt], scratch[2 * nt:]
        wid = _sc_worker_id()
        for idx_hbm, out_hbm, idx_v, rows_v, (per_w, ch, n_ch, _) in zip(idx_refs, out_refs, idx_vs, bufs, plans):
            base = wid * per_w
            pltpu.sync_copy(idx_hbm.at[pl.ds(base, per_w)], idx_v)

            def gather(g, b, idx_v=idx_v, rows_v=rows_v, ch=ch):
                return pltpu.make_async_copy(ys_hbm.at[idx_v.at[pl.ds(g * ch, ch)]], rows_v.at[b], gsem.at[b])

            def put(g, b, out_hbm=out_hbm, rows_v=rows_v, base=base, ch=ch):
                return pltpu.make_async_copy(rows_v.at[b], out_hbm.at[pl.ds(base + g * ch, ch)], wsem.at[b])

            gather(0, 0).start()

            @pl.loop(0, n_ch, step=2)
            def _(g0, gather=gather, put=put, n_ch=n_ch):
                for b in range(2):
                    g = g0 + b

                    @pl.when(g + 1 < n_ch)
                    def _():
                        @pl.when(g >= 1)
                        def _():
                            put(g - 1, 1 - b).wait()
                        gather(g + 1, 1 - b).start()

                    gather(g, b).wait()
                    put(g, b).start()

            for g in (n_ch - 2, n_ch - 1):
                put(g, g % 2).wait()

    call = pl.kernel(
        body,
        out_type=tuple(jax.ShapeDtypeStruct((s.size, d), dtype) for s in slots),
        mesh=plsc.VectorSubcoreMesh(core_axis_name="c", subcore_axis_name="s"),
        scratch_types=[pltpu.VMEM((per_w,), jnp.int32) for (per_w, _, _, _) in plans]
        + [pltpu.VMEM((2, ch, d), dtype) for (_, ch, _, _) in plans]
        + [pltpu.SemaphoreType.DMA((2,)), pltpu.SemaphoreType.DMA((2,))],
        name="sc_collect",
    )
    out = call(ys, *[s.reshape(s.size) for s in slots])
    return out if isinstance(out, (tuple, list)) else (out,)


def _combine_kernel(x_ref, mod_ref, y_ref, tg_ref, fg_ref, *rest):
    o_ref = rest[-1]
    bt, t, d = x_ref.shape
    m = bt * t
    g2 = mod_ref[:, :, 5 * d:6 * d]
    g2 = g2.reshape(1, d) if bt == 1 else jnp.broadcast_to(g2, (bt, t, d)).reshape(m, d)
    out = x_ref[...].reshape(m, d) + g2 * _moe_mix(y_ref, tg_ref[...])
    o_ref[...] = _rms(out, fg_ref[...]).reshape(bt, t, d)


def _combine_call(x_new, b0, mod, y4, tg, final_g, bt, t, out_batch, out_b0, out_prev=None):
    _, s, d = x_new.shape
    b = mod.shape[0]
    m = bt * t
    steps = s // t
    ob = out_b0 // bt
    in_specs = [pl.BlockSpec((bt, t, d), lambda bi, j: (bi + b0 // bt, j, 0)),
                pl.BlockSpec((bt, 1, 6 * d), lambda bi, j: (bi, 0, 0)),
                pl.BlockSpec((TOP_K, m, d // 2), lambda bi, j: (0, (bi + ob) * steps + j, 0)),
                pl.BlockSpec((m, TOPK_LANES), lambda bi, j: ((bi + ob) * steps + j, 0)),
                pl.BlockSpec((1, d), lambda bi, j: (0, 0))]
    operands = [x_new, mod, y4, tg, final_g]
    aliases = {}
    if out_prev is not None:
        in_specs.append(pl.BlockSpec(memory_space=pl.ANY))
        operands.append(out_prev)
        aliases = {len(operands) - 1: 0}
    return pl.pallas_call(
        _combine_kernel,
        grid=(b // bt, steps),
        in_specs=in_specs,
        out_specs=pl.BlockSpec((bt, t, d), lambda bi, j: (bi + out_b0 // bt, j, 0)),
        out_shape=jax.ShapeDtypeStruct((out_batch, s, d), F32),
        input_output_aliases=aliases,
        cost_estimate=pl.CostEstimate(flops=2 * (TOP_K + 3) * b * s * d, transcendentals=b * s,
                                      bytes_accessed=b * s * d * (4 + 4 + TOP_K * 2)),
        compiler_params=pltpu.CompilerParams(dimension_semantics=("arbitrary", "arbitrary")),
        name=f"combine_t{t}",
    )(*operands)


def _layer_weights(l, p, cfgs):
    d = p["w_out"].shape[-1]
    row = lambda v: v[l].reshape(1, d)
    wr = p["w_router"][l]
    wr_pad = jnp.zeros((d, ROUTER_LANES), F32).at[:, :N_EXPERTS].set(wr)
    wr_hi = wr_pad.astype(BF16)
    wr_lo = (wr_pad - wr_hi.astype(F32)).astype(BF16)
    b_router = jnp.full((1, ROUTER_LANES), -1e30, F32).at[0, :N_EXPERTS].set(p["b_router"][l])
    mask = jnp.tril(jnp.ones((GMLP_CHUNK, GMLP_CHUNK), dtype=bool))
    w_sp_full = jnp.where(mask[None], p["w_spatial"][l], 0)
    hd = d // GMLP_HEADS
    base = dict(norm_mix_g=row(p["norm_mix_g"]), norm_ffn_g=row(p["norm_ffn_g"]), w_in=p["w_in"][l].astype(BF16),
                ln_v_g=row(p["ln_v_g"]), ln_v_b=row(p["ln_v_b"]), w_a_out=p["w_a_out"][l].astype(BF16),
                w_pool=p["w_pool"][l].astype(BF16), b_pool=row(p["b_pool"]), pool_scale=row(p["pool_scale"]),
                w_dw=jnp.zeros((_round_up(CONV_K, SUBLANES), d), F32).at[:CONV_K].set(p["w_dw"][l]),
                b_dw=row(p["b_dw"]), ln_c_g=row(p["ln_c_g"]), ln_c_b=row(p["ln_c_b"]),
                w_c_out=p["w_c_out"][l].astype(BF16), w_out=p["w_out"][l].astype(BF16),
                wr_hi=wr_hi, wr_lo=wr_lo, b_router=b_router)
    out = []
    for cfg in cfgs:
        w_sp = jnp.zeros((GMLP_HEADS, cfg.ch, cfg.kg), F32).at[:, :, :cfg.ch].set(w_sp_full[:, :cfg.ch, :cfg.ch])
        bs_full = jnp.repeat(p["b_spatial"][l][:, :cfg.ch].T, hd, axis=1)
        m = cfg.bt * cfg.t
        ltri = (jnp.arange(m)[:, None] > jnp.arange(m)[None, :]).astype(BF16)
        out.append(dict(base, w_sp=w_sp.astype(BF16), bs_full=bs_full, pmat=_pool_band(cfg.t, cfg.kp), ltri=ltri))
    return out


def _pad_front(state, rows):
    pad = rows - state.shape[-2]
    return jnp.pad(state, ((0, 0),) * (state.ndim - 2) + ((pad, 0), (0, 0)))


def kernel(x_prompt, x_sample, c_prompt, c_sample, state_pool, state_conv, norm_mix_g, norm_ffn_g, w_ada, b_ada, w_in, ln_v_g, ln_v_b, w_spatial, b_spatial, w_a_out, w_pool, b_pool, pool_scale, w_dw, b_dw, ln_c_g, ln_c_b, w_c_out, w_out, w_router, b_router, w_gate_up, b_gate_up, w_down, b_down, final_norm_g):
    p = dict(norm_mix_g=norm_mix_g, norm_ffn_g=norm_ffn_g, w_in=w_in, ln_v_g=ln_v_g, ln_v_b=ln_v_b,
             w_spatial=w_spatial, b_spatial=b_spatial, w_a_out=w_a_out, w_pool=w_pool, b_pool=b_pool,
             pool_scale=pool_scale, w_dw=w_dw, b_dw=b_dw, ln_c_g=ln_c_g, ln_c_b=ln_c_b, w_c_out=w_c_out,
             w_out=w_out, w_router=w_router, b_router=b_router)
    n_layers = w_in.shape[0]
    bp, sp, d = x_prompt.shape
    bs, ss, _ = x_sample.shape
    tp = min(PROMPT_TILE, sp)
    bts = min(SAMPLE_BT, bs)
    cfg_p = MixerCfg(bt=1, t=tp, ch=GMLP_CHUNK, kg=GMLP_CHUNK, kp=_round_up(POOL_HIST + tp, LANES), d=d,
                     start_pos=0, emit_v=False, fuse_in=False, steps=sp // tp, n_tiles=bp * (sp // tp))
    cfg_s = MixerCfg(bt=bts, t=ss, ch=ss, kg=_round_up(ss, LANES), kp=_round_up(POOL_HIST + ss, LANES), d=d,
                     start_pos=PAST_LEN, emit_v=True, fuse_in=False, steps=1, n_tiles=bs // bts)
    n_p, n_s = bp * sp, bs * ss
    n_slots = (n_p + n_s) * TOP_K + N_EXPERTS * MOE_TM

    mod_all = _ada_call(jnp.concatenate([c_prompt, c_sample], axis=0), w_ada, b_ada)
    mod_p = mod_all[:, :bp].reshape(n_layers, bp, 1, 6 * d)
    mod_s = mod_all[:, bp:].reshape(n_layers, bs, 1, 6 * d)
    final_g = final_norm_g.reshape(1, d)

    hb = bp // 2
    halves = [(0, hb), (hb, bp - hb)]
    steps_p = sp // tp
    xs = x_sample
    parts = [dict(lo=0, n=bp, x=x_prompt)]
    zero_pool = jnp.zeros((bp, POOL_HIST, d), F32)
    zero_conv = jnp.zeros((bp, CONV_HIST, d), F32)
    zero_cnt = jnp.zeros((1, ROUTER_LANES), F32)
    fuse_p, fuse_s = {}, ()
    pools_p, convs_p, pools_s, convs_s, vs = [], [], [], [], []

    def holder(lo):
        return next(pp for pp in parts if pp["lo"] <= lo < pp["lo"] + pp["n"])

    for l in range(n_layers):
        fused = l > 0
        if fused:
            p, _ = lax.optimization_barrier((p, cnt_all))
        lw_p, lw_s = _layer_weights(l, p, (cfg_p, cfg_s))
        cnt = zero_cnt
        new_parts = []
        for lo, n in (halves if fused else [(0, bp)]):
            src = holder(lo)
            x_new, h2, ti, tg, rk, cnt, npool, nconv = _mixer_call(
                cfg_p._replace(fuse_in=fused, n_tiles=n * steps_p), src["x"], lo - src["lo"], mod_p[l, lo:lo + n],
                zero_pool[lo:lo + n], zero_conv[lo:lo + n], lw_p, cnt[0:1], fuse_p.get((lo, n), ()))
            new_parts.append(dict(lo=lo, n=n, x=x_new, h2=h2.reshape(n * sp, d // 2), ti=ti, tg=tg, rk=rk,
                                  npool=npool, nconv=nconv))
        parts = new_parts
        xs, h2s, tis, tgs, rks, cnt_all, npool_s, nconv_s, v_s = _mixer_call(
            cfg_s._replace(fuse_in=fused), xs, 0, mod_s[l], _pad_front(state_pool[l], POOL_HIST),
            _pad_front(state_conv[l], CONV_HIST), lw_s, cnt[0:1], fuse_s)
        top_i = jnp.concatenate([pp["ti"] for pp in parts] + [tis], axis=0)[:, :TOP_K]
        rank = jnp.concatenate([pp["rk"] for pp in parts] + [rks], axis=0)[:, :TOP_K]
        tg_p = jnp.concatenate([pp["tg"] for pp in parts], axis=0)
        slot_of, block_e, nused = _route(top_i, rank, cnt_all[0, :N_EXPERTS].astype(jnp.int32))
        tables = [pp["h2"] for pp in parts] + [h2s.reshape(n_s, d // 2)]
        bounds = [pp["lo"] * sp for pp in parts] + [n_p, n_p + n_s]
        slots = [slot_of[:, bounds[i]:bounds[i + 1]] for i in range(len(tables))]
        x_sorted = _sc_dispatch(tables, slots, n_slots)
        ys = _moe_call(x_sorted, block_e, nused, l, w_gate_up, b_gate_up, w_down, b_down)
        consumers = halves if l < n_layers - 1 else [(0, bp)]
        y4_parts = []
        for lo, n in consumers[:-1]:
            y4_parts += _sc_collect(ys, [slot_of[:, lo * sp:(lo + n) * sp]])
        lo, n = consumers[-1]
        y4_last, y4s = _sc_collect(ys, [slot_of[:, lo * sp:(lo + n) * sp], slot_of[:, n_p:]])
        fuse_p = {(lo, n): (y4.reshape(TOP_K, n * sp, d // 2), tg_p[lo * sp:(lo + n) * sp], mod_p[l, lo:lo + n])
                  for (lo, n), y4 in zip(consumers, y4_parts + [y4_last])}
        fuse_s = (y4s.reshape(TOP_K, n_s, d // 2), tgs, mod_s[l])
        pools_p.append(jnp.concatenate([pp["npool"] for pp in parts], axis=0))
        convs_p.append(jnp.concatenate([pp["nconv"] for pp in parts], axis=0))
        pools_s.append(npool_s)
        convs_s.append(nconv_s)
        vs.append(v_s)
    yp = None
    (y4, tg, _), = fuse_p.values()
    for pp in parts:
        yp = _combine_call(pp["x"], 0, mod_p[-1, pp["lo"]:pp["lo"] + pp["n"]], y4, tg, final_g, 1, tp, bp, pp["lo"],
                           yp)
    ys_out = _combine_call(xs, 0, mod_s[-1], *fuse_s[:2], final_g, bts, ss, bs, 0)
    return (yp, ys_out, jnp.stack(pools_p), jnp.stack(convs_p), jnp.stack(pools_s), jnp.stack(convs_s),
            jnp.stack(vs))
```

```python
import functools
from typing import NamedTuple

import jax
import jax.numpy as jnp
from jax import lax
from jax.experimental import pallas as pl
from jax.experimental.pallas import tpu as pltpu
from jax.experimental.pallas import tpu_sc as plsc

GMLP_CHUNK = 128
GMLP_HEADS = 4
POOL_WINDOWS = (2, 4, 8, 16)
POOL_STATE = max(POOL_WINDOWS) - 1
CONV_K = 31
CONV_STATE = CONV_K - 1
N_BRANCH = 3
N_EXPERTS = 32
TOP_K = 4
SWIGLU_LIMIT = 7.0
SWIGLU_ALPHA = 1.702
EPS = 1e-6
PAST_LEN = 2048

LANES = 128
SUBLANES = 8
V7X_VMEM_LIMIT_BYTES = 56 * 2**20

POOL_HIST = 16
CONV_HIST = 32
ROUTER_LANES = LANES
TOPK_LANES = 8
MOE_TM = 512
PROMPT_TILE = 256
SAMPLE_BT = 8

F32 = jnp.float32
BF16 = jnp.bfloat16


def _round_up(a, m):
    return (a + m - 1) // m * m


class MixerCfg(NamedTuple):
    bt: int
    t: int
    ch: int
    kg: int
    kp: int
    d: int
    start_pos: int
    emit_v: bool
    fuse_in: bool
    steps: int
    n_tiles: int


def _rms(x, g):
    return x * lax.rsqrt(jnp.mean(x * x, axis=-1, keepdims=True) + EPS) * g


def _ln(x, g, b):
    mu = jnp.mean(x, axis=-1, keepdims=True)
    xc = x - mu
    var = jnp.mean(xc * xc, axis=-1, keepdims=True)
    return xc * lax.rsqrt(var + EPS) * g + b


def _dot(a, b):
    return jnp.dot(a, b, preferred_element_type=F32)


def _pack_rows(x):
    half = x.shape[1] // 2
    hi = lax.bitcast_convert_type(x[:, :half].astype(BF16).astype(F32), jnp.int32)
    lo = lax.bitcast_convert_type(x[:, half:].astype(BF16).astype(F32), jnp.int32)
    return hi | lax.shift_right_logical(lo, 16)


def _unpack_rows(w):
    hi = lax.bitcast_convert_type(w & jnp.int32(-65536), F32)
    lo = lax.bitcast_convert_type(lax.shift_left(w, 16), F32)
    return jnp.concatenate([hi, lo], axis=1)


def _moe_mix(y4_ref, tg):
    f = tg[:, 0:1] * _unpack_rows(y4_ref[0])
    for k in range(1, TOP_K):
        f = f + tg[:, k:k + 1] * _unpack_rows(y4_ref[k])
    return f


def _mixer_kernel(cfg, x_ref, mod_ref, modb_ref, pp_ref, pc_ref, nmg_ref, nfg_ref, win_ref, lnvg_ref, lnvb_ref,
                  wsp_ref, bsf_ref, wao_ref, wpool_ref, bpool_ref, pscale_ref, wdw_ref, bdw_ref,
                  lncg_ref, lncb_ref, wco_ref, wout_ref, wrh_ref, wrl_ref, br_ref, pmat_ref, ltri_ref, cnt0_ref,
                  *rest):
    rest = list(rest)
    if cfg.fuse_in:
        y4_ref, tgp_ref, modp_ref = rest[:3]
        rest = rest[3:]
    xo_ref, h2_ref, ti_ref, tg_ref, rk_ref, cnt_ref, npool_ref, nconv_ref = rest[:8]
    rest = rest[8:]
    if cfg.emit_v:
        v_ref, hp_scr, xc_scr, run_scr, mg_scr, xr_scr = rest
    else:
        hp_scr, xc_scr, run_scr, mg_scr, xr_scr = rest
    bt, t, d = cfg.bt, cfg.t, cfg.d
    m = bt * t
    n_slab = d // LANES
    gd = d // len(POOL_WINDOWS)
    hd = d // GMLP_HEADS
    q = pl.program_id(0)
    j = 0 if cfg.steps == 1 else lax.rem(jnp.minimum(q, cfg.n_tiles - 1), cfg.steps)

    def mod_rows(k, ref=mod_ref):
        r = ref[:, :, k * d:(k + 1) * d]
        if bt == 1:
            return r.reshape(1, d)
        return jnp.broadcast_to(r, (bt, t, d)).reshape(m, d)

    @pl.when(q == 0)
    def _init_carries():
        run_scr[...] = cnt0_ref[...]
        mg_scr[...] = jnp.zeros_like(mg_scr)
        xr_scr[...] = jnp.zeros_like(xr_scr)

    @pl.when(j == 0)
    def _load_state():
        hp_scr[...] = pp_ref[...]
        for bi in range(bt):
            for c in range(n_slab):
                xc_scr[bi, c, 0:CONV_HIST, :] = pc_ref[bi, :, c * LANES:(c + 1) * LANES]

    x_prev = xr_scr[...]
    out_prev = _dot(mg_scr[...], wout_ref[...])

    x = x_ref[...].reshape(m, d)
    if cfg.fuse_in:
        x = x + mod_rows(5, modp_ref) * _moe_mix(y4_ref, tgp_ref[...])
    sh1, sc1 = mod_rows(0), mod_rows(1)
    h = (_rms(x, nmg_ref[...]) * (1.0 + sc1) + sh1).astype(BF16)

    def zcols(lo, hi):
        return _dot(h, win_ref[:, lo:hi])

    off_u, off_v, off_b, off_c, off_g = 0, d, 2 * d, 3 * d, 5 * d
    zc_a = zcols(off_c, off_c + d)
    zc_b = zcols(off_c + d, off_c + 2 * d)

    x_new = x_prev + mod_rows(2, modb_ref) * out_prev
    xo_ref[...] = x_new.reshape(bt, t, d)
    h2 = _rms(x_new, nfg_ref[...]) * (1.0 + mod_rows(4, modb_ref)) + mod_rows(3, modb_ref)
    h2_ref[...] = _pack_rows(h2).reshape(bt, t, d // 2)
    h2_hi = h2.astype(BF16)
    h2_lo = (h2 - h2_hi.astype(F32)).astype(BF16)
    logits = _dot(h2_hi, wrh_ref[...]) + _dot(h2_lo, wrh_ref[...]) + _dot(h2_hi, wrl_ref[...]) + br_ref[...]

    x_c = zc_a * jax.nn.sigmoid(zc_b)
    for bi in range(bt):
        for c in range(n_slab):
            xc_scr[bi, c, CONV_HIST:CONV_HIST + t, :] = x_c[bi * t:(bi + 1) * t, c * LANES:(c + 1) * LANES]

    lane_r = lax.broadcasted_iota(jnp.int32, (m, ROUTER_LANES), 1).astype(F32)
    lane_k = lax.broadcasted_iota(jnp.int32, (m, TOPK_LANES), 1)
    topk = dict(work=logits, ti=jnp.zeros((m, TOPK_LANES), jnp.int32), vals=[], onehots=[])

    def next_topk_round():
        r = len(topk["vals"])
        if r < TOP_K:
            mx = jnp.max(topk["work"], axis=-1, keepdims=True)
            idx = jnp.min(jnp.where(topk["work"] == mx, lane_r, float(ROUTER_LANES)), axis=-1, keepdims=True)
            topk["ti"] = jnp.where(lane_k == r, idx.astype(jnp.int32), topk["ti"])
            topk["vals"].append(mx)
            topk["work"] = jnp.where(lane_r == idx, -jnp.inf, topk["work"])
            topk["onehots"].append((lane_r == idx).astype(F32))

    proj_cols = dict(ga=off_g, gb=off_g + d, gc=off_g + 2 * d, zu=off_u, zv=off_v, zb=off_b)
    pending_proj = list(proj_cols)
    proj = {}

    def next_proj():
        if pending_proj:
            name = pending_proj.pop(0)
            proj[name] = zcols(proj_cols[name], proj_cols[name] + d)

    conv_parts = []
    for bi in range(bt):
        accs = []
        for c in range(n_slab):
            cs = slice(c * LANES, (c + 1) * LANES)
            acc = jnp.broadcast_to(bdw_ref[:, cs], (t, LANES))
            for k in range(CONV_K):
                acc = acc + wdw_ref[k:k + 1, cs] * xc_scr[bi, c, pl.ds(CONV_HIST - CONV_STATE + k, t), :]
            accs.append(acc)
            next_proj()
            next_topk_round()
        conv_parts.append(jnp.concatenate(accs, axis=1))
    for bi in range(bt):
        for c in range(n_slab):
            cs = slice(c * LANES, (c + 1) * LANES)
            tail = xc_scr[bi, c, t:t + CONV_HIST, :]
            nconv_ref[bi, :, cs] = xc_scr[bi, c, pl.ds(t + CONV_HIST - CONV_STATE, CONV_STATE), :]
            xc_scr[bi, c, 0:CONV_HIST, :] = tail
    while pending_proj:
        next_proj()
    while len(topk["vals"]) < TOP_K:
        next_topk_round()
    conv = conv_parts[0] if bt == 1 else jnp.concatenate(conv_parts, axis=0)

    top_vals, onehots = topk["vals"], topk["onehots"]
    exps = [jnp.exp(tv - top_vals[0]) for tv in top_vals]
    denom = exps[0] + exps[1] + exps[2] + exps[3]
    tg = jnp.zeros((m, TOPK_LANES), F32)
    for r in range(TOP_K):
        tg = jnp.where(lane_k == r, exps[r] / denom, tg)
    ti_ref[...] = topk["ti"]
    tg_ref[...] = tg
    base = run_scr[...]
    rk = jnp.zeros((m, TOPK_LANES), jnp.int32)
    new_base = base
    for r in range(TOP_K):
        before = _dot(ltri_ref[...], onehots[r].astype(BF16)) + new_base
        rank = jnp.sum(onehots[r] * before, axis=-1, keepdims=True)
        rk = jnp.where(lane_k == r, rank.astype(jnp.int32), rk)
        new_base = new_base + jnp.sum(onehots[r], axis=0, keepdims=True)
    new_base = jnp.where(q >= 1, new_base, base)
    run_scr[...] = new_base
    rk_ref[...] = rk
    cnt_ref[...] = jnp.broadcast_to(new_base, cnt_ref.shape)

    zb = proj["zb"]
    lane = lax.broadcasted_iota(jnp.int32, (1, d), 1)
    win_lane = jnp.left_shift(2, lane // gd).astype(F32)
    pos1 = (cfg.start_pos + 1 + j * t + lax.broadcasted_iota(jnp.int32, (t, 1), 0)).astype(F32)
    cnt = jnp.minimum(pos1, win_lane)
    pooled_parts = []
    for bi in range(bt):
        zb_b = zb[bi * t:(bi + 1) * t]
        pieces = [hp_scr[bi], zb_b]
        if cfg.kp > POOL_HIST + t:
            pieces = [jnp.zeros((cfg.kp - POOL_HIST - t, d), F32)] + pieces
        full = jnp.concatenate(pieces, axis=0)
        fullb = full.astype(BF16)
        sums = jnp.concatenate(
            [_dot(pmat_ref[g], fullb[:, g * gd:(g + 1) * gd]) for g in range(len(POOL_WINDOWS))], axis=1)
        pooled_parts.append(sums / cnt - zb_b)
        hp_scr[bi] = full[cfg.kp - POOL_HIST:]
        npool_ref[bi] = hp_scr[bi, pl.ds(POOL_HIST - POOL_STATE, POOL_STATE), :]
    pooled = (pooled_parts[0] if bt == 1 else jnp.concatenate(pooled_parts, axis=0)).astype(BF16)
    y_b = jnp.concatenate(
        [_dot(pooled[:, g * gd:(g + 1) * gd], wpool_ref[g]) for g in range(len(POOL_WINDOWS))], axis=1)
    y_b = (y_b + bpool_ref[...]) * pscale_ref[...]

    v = _ln(jax.nn.gelu(proj["zv"]), lnvg_ref[...], lnvb_ref[...])
    if cfg.emit_v:
        v_ref[...] = v.reshape(bt, t, d)
    vb = v.astype(BF16)
    mixed_rows = []
    for r0 in range(0, m, cfg.ch):
        vc = vb[r0:r0 + cfg.ch]
        if cfg.kg > cfg.ch:
            vc = jnp.concatenate([vc, jnp.zeros((cfg.kg - cfg.ch, d), BF16)], axis=0)
        mixed_rows.append(jnp.concatenate(
            [_dot(wsp_ref[hh], vc[:, hh * hd:(hh + 1) * hd]) for hh in range(GMLP_HEADS)], axis=1) + bsf_ref[...])
    mixed = mixed_rows[0] if len(mixed_rows) == 1 else jnp.concatenate(mixed_rows, axis=0)

    y_c = _dot(jax.nn.silu(_ln(conv, lncg_ref[...], lncb_ref[...])).astype(BF16), wco_ref[...])
    u = jax.nn.gelu(proj["zu"])
    y_a = _dot((u * mixed).astype(BF16), wao_ref[...])
    merged = jax.nn.sigmoid(proj["gb"]) * y_b
    merged = merged + jax.nn.sigmoid(proj["ga"]) * y_a
    merged = merged + jax.nn.sigmoid(proj["gc"]) * y_c

    mg_scr[...] = merged.astype(BF16)
    xr_scr[...] = x


def _const_spec(shape):
    nd = len(shape)
    return pl.BlockSpec(shape, lambda q, _nd=nd: (0,) * _nd, pipeline_mode=pl.Buffered(1))


def _pool_band(t, kp):
    col = jnp.arange(kp)[None, :]
    end = (kp - t) + jnp.arange(t)[:, None]
    return jnp.stack([((col <= end) & (col > end - w)) for w in POOL_WINDOWS]).astype(BF16)


def _mixer_call(cfg, x, b0, mod, prev_pool, prev_conv, lw, cnt0, fuse=()):
    _, s, d = x.shape
    b = mod.shape[0]
    bt, t = cfg.bt, cfg.t
    m = bt * t
    steps, n_tiles = cfg.steps, cfg.n_tiles
    assert steps == s // t and n_tiles == (b // bt) * steps and b0 % bt == 0
    assert steps == 1 or t >= max(CONV_STATE, POOL_STATE)
    n_slab = d // LANES

    def front(q):
        return jnp.minimum(q, n_tiles - 1)

    def back(q):
        return jnp.maximum(q - 1, 0)

    def tile_spec(which, width=d):
        return pl.BlockSpec((bt, t, width), lambda q: (which(q) // steps, which(q) % steps, 0))

    def batch_spec(which, rows, width):
        return pl.BlockSpec((bt, rows, width), lambda q: (which(q) // steps, 0, 0))

    def rows_spec(which, width):
        return pl.BlockSpec((m, width), lambda q: (which(q), 0))

    consts = [lw["norm_mix_g"], lw["norm_ffn_g"], lw["w_in"], lw["ln_v_g"], lw["ln_v_b"], lw["w_sp"], lw["bs_full"],
              lw["w_a_out"], lw["w_pool"], lw["b_pool"], lw["pool_scale"], lw["w_dw"], lw["b_dw"], lw["ln_c_g"],
              lw["ln_c_b"], lw["w_c_out"], lw["w_out"], lw["wr_hi"], lw["wr_lo"], lw["b_router"], lw["pmat"],
              lw["ltri"], cnt0]
    x_spec = pl.BlockSpec((bt, t, d), lambda q: (front(q) // steps + b0 // bt, front(q) % steps, 0))
    in_specs = [x_spec, batch_spec(front, 1, 6 * d), batch_spec(back, 1, 6 * d),
                batch_spec(front, POOL_HIST, d), batch_spec(front, CONV_HIST, d)]
    in_specs += [_const_spec(c.shape) for c in consts]
    if cfg.fuse_in:
        in_specs += [pl.BlockSpec((TOP_K, m, d // 2), lambda q: (0, front(q), 0)),
                     rows_spec(front, TOPK_LANES), batch_spec(front, 1, 6 * d)]
    out_shape = [jax.ShapeDtypeStruct((b, s, d), F32), jax.ShapeDtypeStruct((b, s, d // 2), jnp.int32),
                 jax.ShapeDtypeStruct((b * s, TOPK_LANES), jnp.int32), jax.ShapeDtypeStruct((b * s, TOPK_LANES), F32),
                 jax.ShapeDtypeStruct((b * s, TOPK_LANES), jnp.int32),
                 jax.ShapeDtypeStruct((SUBLANES, ROUTER_LANES), F32),
                 jax.ShapeDtypeStruct((b, POOL_STATE, d), F32), jax.ShapeDtypeStruct((b, CONV_STATE, d), F32)]
    out_specs = [tile_spec(back), tile_spec(back, d // 2),
                 rows_spec(back, TOPK_LANES), rows_spec(back, TOPK_LANES), rows_spec(back, TOPK_LANES),
                 pl.BlockSpec((SUBLANES, ROUTER_LANES), lambda q: (0, 0)),
                 batch_spec(front, POOL_STATE, d), batch_spec(front, CONV_STATE, d)]
    if cfg.emit_v:
        out_shape.append(jax.ShapeDtypeStruct((b, s, d), F32))
        out_specs.append(tile_spec(front))
    return pl.pallas_call(
        functools.partial(_mixer_kernel, cfg),
        grid=(n_tiles + 1,),
        in_specs=in_specs,
        out_specs=out_specs,
        out_shape=out_shape,
        scratch_shapes=[pltpu.VMEM((bt, POOL_HIST, d), F32),
                        pltpu.VMEM((bt, n_slab, CONV_HIST + t, LANES), F32),
                        pltpu.VMEM((1, ROUTER_LANES), F32),
                        pltpu.VMEM((m, d), BF16), pltpu.VMEM((m, d), F32)],
        compiler_params=pltpu.CompilerParams(dimension_semantics=("arbitrary",),
                                             vmem_limit_bytes=V7X_VMEM_LIMIT_BYTES),
        name=f"mixer_t{t}",
    )(x, mod, mod, prev_pool, prev_conv, *consts, *fuse)


def _ada_kernel(c_ref, w_ref, b_ref, o_ref):
    o_ref[0] = _dot(jax.nn.silu(c_ref[...]).astype(BF16), w_ref[0].astype(BF16)) + b_ref[0]


def _ada_call(c_all, w_ada, b_ada):
    n_layers, d, six_d = w_ada.shape
    rows = c_all.shape[0]
    bn = six_d // 6
    return pl.pallas_call(
        _ada_kernel,
        grid=(n_layers, six_d // bn),
        in_specs=[pl.BlockSpec((rows, d), lambda l, n: (0, 0)),
                  pl.BlockSpec((1, d, bn), lambda l, n: (l, 0, n)),
                  pl.BlockSpec((1, 1, bn), lambda l, n: (l, 0, n))],
        out_specs=pl.BlockSpec((1, rows, bn), lambda l, n: (l, 0, n)),
        out_shape=jax.ShapeDtypeStruct((n_layers, rows, six_d), F32),
        compiler_params=pltpu.CompilerParams(dimension_semantics=("arbitrary", "arbitrary")),
        name="adaln",
    )(c_all, w_ada, b_ada.reshape(n_layers, 1, six_d))


def _moe_kernel(be_ref, nused_ref, x_ref, wgu_ref, bgu_ref, wdn_ref, bdn_ref, y_ref, wgu_bf, wdn_bf):
    i = pl.program_id(0)
    nused = nused_ref[0]
    dff = wdn_bf.shape[0]

    @pl.when(i < nused)
    def _body():
        changed = jnp.logical_or(i == 0, be_ref[i] != be_ref[jnp.maximum(i - 1, 0)])

        @pl.when(changed)
        def _cast_weights():
            wgu_bf[...] = wgu_ref[0].astype(BF16)
            wdn_bf[...] = wdn_ref[0].astype(BF16)

        gu = _dot(_unpack_rows(x_ref[...]).astype(BF16), wgu_bf[...]) + bgu_ref[0]
        g = jnp.minimum(gu[:, :dff], SWIGLU_LIMIT)
        u = jnp.clip(gu[:, dff:], -SWIGLU_LIMIT, SWIGLU_LIMIT)
        act = (u + 1.0) * g * jax.nn.sigmoid(SWIGLU_ALPHA * g)
        y_ref[...] = _pack_rows(_dot(act.astype(BF16), wdn_bf[...]) + bdn_ref[0])

    @pl.when(i >= nused)
    def _unused_block():
        y_ref[...] = jnp.zeros_like(y_ref)


def _moe_call(xs, block_e, nused, layer, w_gu_all, b_gu_all, w_dn_all, b_dn_all):
    n_blocks = block_e.shape[0]
    n_layers, n_exp, d, two_f = w_gu_all.shape
    dff = two_f // 2
    tm = MOE_TM
    e0 = layer * n_exp
    w_gu = w_gu_all.reshape(n_layers * n_exp, d, two_f)
    b_gu = b_gu_all.reshape(n_layers * n_exp, 1, two_f)
    w_dn = w_dn_all.reshape(n_layers * n_exp, dff, d)
    b_dn = b_dn_all.reshape(n_layers * n_exp, 1, d)
    grid_spec = pltpu.PrefetchScalarGridSpec(
        num_scalar_prefetch=2,
        grid=(n_blocks,),
        in_specs=[
            pl.BlockSpec((tm, d // 2), lambda i, be, nu: (i, 0)),
            pl.BlockSpec((1, d, two_f), lambda i, be, nu: (e0 + be[i], 0, 0)),
            pl.BlockSpec((1, 1, two_f), lambda i, be, nu: (e0 + be[i], 0, 0)),
            pl.BlockSpec((1, dff, d), lambda i, be, nu: (e0 + be[i], 0, 0)),
            pl.BlockSpec((1, 1, d), lambda i, be, nu: (e0 + be[i], 0, 0)),
        ],
        out_specs=pl.BlockSpec((tm, d // 2), lambda i, be, nu: (i, 0)),
        scratch_shapes=[pltpu.VMEM((d, two_f), BF16), pltpu.VMEM((dff, d), BF16)],
    )
    return pl.pallas_call(
        _moe_kernel,
        grid_spec=grid_spec,
        out_shape=jax.ShapeDtypeStruct((n_blocks * tm, d // 2), jnp.int32),
        compiler_params=pltpu.CompilerParams(dimension_semantics=("arbitrary",),
                                             vmem_limit_bytes=V7X_VMEM_LIMIT_BYTES),
        name="moe_experts",
    )(block_e, nused, xs, w_gu, b_gu, w_dn, b_dn)


def _route(top_i, rank, counts):
    tm = MOE_TM
    n_all = top_i.shape[0]
    n_blocks = n_all * TOP_K // tm + N_EXPERTS
    padded = (counts + tm - 1) // tm * tm
    pad_ends = jnp.cumsum(padded)
    pad_starts = pad_ends - padded
    blk = jnp.arange(n_blocks, dtype=jnp.int32)[:, None] * tm
    block_e = jnp.minimum(jnp.sum((pad_ends[None, :] <= blk).astype(jnp.int32), axis=1), N_EXPERTS - 1)
    nused = (pad_ends[-1] // tm).astype(jnp.int32).reshape(1)
    onehot = top_i[:, :, None] == jnp.arange(N_EXPERTS, dtype=jnp.int32)[None, None, :]
    slot_of = jnp.sum(jnp.where(onehot, pad_starts[None, None, :], 0), axis=-1) + rank
    return slot_of.T.astype(jnp.int32), block_e.astype(jnp.int32), nused


SC_CORES = 2
SC_SUBCORES = 16
SC_WORKERS = SC_CORES * SC_SUBCORES
SC_CHUNK = 64


def _sc_worker_id():
    return lax.axis_index("s") * SC_CORES + lax.axis_index("c")


def _sc_plan(n_rows, split=False):
    per_w = n_rows // SC_WORKERS
    ch = min(SC_CHUNK, per_w // 2 if split else per_w)
    n_ch = per_w // ch
    assert per_w * SC_WORKERS == n_rows and n_ch * ch == per_w
    return per_w, ch, n_ch, (n_ch >= 2 and n_ch % 2 == 0)


def _sc_dispatch(tables, slots, n_slots):
    nt = len(tables)
    d, dtype = tables[0].shape[1], tables[0].dtype
    plans = [_sc_plan(h.shape[0]) for h in tables]

    buf_shapes = sorted({(2 if piped else 1, ch) for (_, ch, _, piped) in plans})

    def body(*refs):
        h_refs, idx_refs, out_hbm = refs[:nt], refs[nt:2 * nt], refs[2 * nt]
        scratch = refs[2 * nt + 1:]
        idx_vs, shared, (rsem, wsem) = scratch[:nt], scratch[nt:-2], scratch[-2:]
        bufs = [shared[buf_shapes.index((2 if piped else 1, ch))] for (_, ch, _, piped) in plans]
        wid = _sc_worker_id()
        for h_hbm, idx_hbm, idx_v, buf, (tpw, ch, n_ch, piped) in zip(h_refs, idx_refs, idx_vs, bufs, plans):
            pltpu.sync_copy(idx_hbm.at[wid], idx_v)

            def read(c, b, h_hbm=h_hbm, buf=buf, tpw=tpw, ch=ch):
                return pltpu.make_async_copy(h_hbm.at[pl.ds(wid * tpw + c * ch, ch)], buf.at[b], rsem.at[b])

            def write(c, k, b, buf=buf, idx_v=idx_v):
                return pltpu.make_async_copy(buf.at[b], out_hbm.at[idx_v.at[c * TOP_K + k]], wsem.at[b])

            if piped:
                read(0, 0).start()

                @pl.loop(0, n_ch, step=2)
                def _(c0, read=read, write=write, n_ch=n_ch):
                    for b in range(2):
                        c = c0 + b

                        @pl.when(c + 1 < n_ch)
                        def _():
                            @pl.when(c >= 1)
                            def _():
                                for k in range(TOP_K):
                                    write(c - 1, k, 1 - b).wait()
                            read(c + 1, 1 - b).start()

                        read(c, b).wait()
                        for k in range(TOP_K):
                            write(c, k, b).start()

                for c in (n_ch - 2, n_ch - 1):
                    for k in range(TOP_K):
                        write(c, k, c % 2).wait()
            else:
                for c in range(n_ch):
                    cp_in = read(c, 0)
                    cp_in.start()
                    cp_in.wait()
                    for k in range(TOP_K):
                        write(c, k, 0).start()
                    for k in range(TOP_K):
                        write(c, k, 0).wait()

    scratch_types = [pltpu.VMEM((n_ch * TOP_K, ch), jnp.int32) for (_, ch, n_ch, _) in plans]
    scratch_types += [pltpu.VMEM((nbuf, ch, d), dtype) for (nbuf, ch) in buf_shapes]
    scratch_types += [pltpu.SemaphoreType.DMA((2,)), pltpu.SemaphoreType.DMA((2,))]
    call = pl.kernel(
        body,
        out_type=jax.ShapeDtypeStruct((n_slots, d), dtype),
        mesh=plsc.VectorSubcoreMesh(core_axis_name="c", subcore_axis_name="s"),
        scratch_types=scratch_types,
        name="sc_dispatch",
    )
    idx = [s.reshape(TOP_K, SC_WORKERS, n_ch, ch).transpose(1, 2, 0, 3).reshape(SC_WORKERS, n_ch * TOP_K, ch)
           for s, (_, ch, n_ch, _) in zip(slots, plans)]
    return call(*tables, *idx)


def _sc_collect(ys, slots):
    nt = len(slots)
    d, dtype = ys.shape[1], ys.dtype
    plans = [_sc_plan(s.size, split=True) for s in slots]
    assert all(piped for (_, _, _, piped) in plans)

    def body(*refs):
        ys_hbm, idx_refs, out_refs = refs[0], refs[1:1 + nt], refs[1 + nt:1 + 2 * nt]
        scratch = refs[1 + 2 * nt:]
        idx_vs, bufs, (gsem, wsem) = scratch[:nt], scratch[nt:2 * nt], scratch[2 * nt:]
        wid = _sc_worker_id()
        for idx_hbm, out_hbm, idx_v, rows_v, (per_w, ch, n_ch, _) in zip(idx_refs, out_refs, idx_vs, bufs, plans):
            base = wid * per_w
            pltpu.sync_copy(idx_hbm.at[pl.ds(base, per_w)], idx_v)

            def gather(g, b, idx_v=idx_v, rows_v=rows_v, ch=ch):
                return pltpu.make_async_copy(ys_hbm.at[idx_v.at[pl.ds(g * ch, ch)]], rows_v.at[b], gsem.at[b])

            def put(g, b, out_hbm=out_hbm, rows_v=rows_v, base=base, ch=ch):
                return pltpu.make_async_copy(rows_v.at[b], out_hbm.at[pl.ds(base + g * ch, ch)], wsem.at[b])

            gather(0, 0).start()

            @pl.loop(0, n_ch, step=2)
            def _(g0, gather=gather, put=put, n_ch=n_ch):
                for b in range(2):
                    g = g0 + b

                    @pl.when(g + 1 < n_ch)
                    def _():
                        @pl.when(g >= 1)
                        def _():
                            put(g - 1, 1 - b).wait()
                        gather(g + 1, 1 - b).start()

                    gather(g, b).wait()
                    put(g, b).start()

            for g in (n_ch - 2, n_ch - 1):
                put(g, g % 2).wait()

    call = pl.kernel(
        body,
        out_type=tuple(jax.ShapeDtypeStruct((s.size, d), dtype) for s in slots),
        mesh=plsc.VectorSubcoreMesh(core_axis_name="c", subcore_axis_name="s"),
        scratch_types=[pltpu.VMEM((per_w,), jnp.int32) for (per_w, _, _, _) in plans]
        + [pltpu.VMEM((2, ch, d), dtype) for (_, ch, _, _) in plans]
        + [pltpu.SemaphoreType.DMA((2,)), pltpu.SemaphoreType.DMA((2,))],
        name="sc_collect",
    )
    out = call(ys, *[s.reshape(s.size) for s in slots])
    return out if isinstance(out, (tuple, list)) else (out,)


def _combine_kernel(x_ref, mod_ref, y_ref, tg_ref, fg_ref, *rest):
    o_ref = rest[-1]
    bt, t, d = x_ref.shape
    m = bt * t
    g2 = mod_ref[:, :, 5 * d:6 * d]
    g2 = g2.reshape(1, d) if bt == 1 else jnp.broadcast_to(g2, (bt, t, d)).reshape(m, d)
    out = x_ref[...].reshape(m, d) + g2 * _moe_mix(y_ref, tg_ref[...])
    o_ref[...] = _rms(out, fg_ref[...]).reshape(bt, t, d)


def _combine_call(x_new, b0, mod, y4, tg, final_g, bt, t, out_batch, out_b0, out_prev=None):
    _, s, d = x_new.shape
    b = mod.shape[0]
    m = bt * t
    steps = s // t
    ob = out_b0 // bt
    in_specs = [pl.BlockSpec((bt, t, d), lambda bi, j: (bi + b0 // bt, j, 0)),
                pl.BlockSpec((bt, 1, 6 * d), lambda bi, j: (bi, 0, 0)),
                pl.BlockSpec((TOP_K, m, d // 2), lambda bi, j: (0, (bi + ob) * steps + j, 0)),
                pl.BlockSpec((m, TOPK_LANES), lambda bi, j: ((bi + ob) * steps + j, 0)),
                pl.BlockSpec((1, d), lambda bi, j: (0, 0))]
    operands = [x_new, mod, y4, tg, final_g]
    aliases = {}
    if out_prev is not None:
        in_specs.append(pl.BlockSpec(memory_space=pl.ANY))
        operands.append(out_prev)
        aliases = {len(operands) - 1: 0}
    return pl.pallas_call(
        _combine_kernel,
        grid=(b // bt, steps),
        in_specs=in_specs,
        out_specs=pl.BlockSpec((bt, t, d), lambda bi, j: (bi + out_b0 // bt, j, 0)),
        out_shape=jax.ShapeDtypeStruct((out_batch, s, d), F32),
        input_output_aliases=aliases,
        cost_estimate=pl.CostEstimate(flops=2 * (TOP_K + 3) * b * s * d, transcendentals=b * s,
                                      bytes_accessed=b * s * d * (4 + 4 + TOP_K * 2)),
        compiler_params=pltpu.CompilerParams(dimension_semantics=("arbitrary", "arbitrary")),
        name=f"combine_t{t}",
    )(*operands)


def _layer_weights(l, p, cfgs):
    d = p["w_out"].shape[-1]
    row = lambda v: v[l].reshape(1, d)
    wr = p["w_router"][l]
    wr_pad = jnp.zeros((d, ROUTER_LANES), F32).at[:, :N_EXPERTS].set(wr)
    wr_hi = wr_pad.astype(BF16)
    wr_lo = (wr_pad - wr_hi.astype(F32)).astype(BF16)
    b_router = jnp.full((1, ROUTER_LANES), -1e30, F32).at[0, :N_EXPERTS].set(p["b_router"][l])
    mask = jnp.tril(jnp.ones((GMLP_CHUNK, GMLP_CHUNK), dtype=bool))
    w_sp_full = jnp.where(mask[None], p["w_spatial"][l], 0)
    hd = d // GMLP_HEADS
    base = dict(norm_mix_g=row(p["norm_mix_g"]), norm_ffn_g=row(p["norm_ffn_g"]), w_in=p["w_in"][l].astype(BF16),
                ln_v_g=row(p["ln_v_g"]), ln_v_b=row(p["ln_v_b"]), w_a_out=p["w_a_out"][l].astype(BF16),
                w_pool=p["w_pool"][l].astype(BF16), b_pool=row(p["b_pool"]), pool_scale=row(p["pool_scale"]),
                w_dw=jnp.zeros((_round_up(CONV_K, SUBLANES), d), F32).at[:CONV_K].set(p["w_dw"][l]),
                b_dw=row(p["b_dw"]), ln_c_g=row(p["ln_c_g"]), ln_c_b=row(p["ln_c_b"]),
                w_c_out=p["w_c_out"][l].astype(BF16), w_out=p["w_out"][l].astype(BF16),
                wr_hi=wr_hi, wr_lo=wr_lo, b_router=b_router)
    out = []
    for cfg in cfgs:
        w_sp = jnp.zeros((GMLP_HEADS, cfg.ch, cfg.kg), F32).at[:, :, :cfg.ch].set(w_sp_full[:, :cfg.ch, :cfg.ch])
        bs_full = jnp.repeat(p["b_spatial"][l][:, :cfg.ch].T, hd, axis=1)
        m = cfg.bt * cfg.t
        ltri = (jnp.arange(m)[:, None] > jnp.arange(m)[None, :]).astype(BF16)
        out.append(dict(base, w_sp=w_sp.astype(BF16), bs_full=bs_full, pmat=_pool_band(cfg.t, cfg.kp), ltri=ltri))
    return out


def _pad_front(state, rows):
    pad = rows - state.shape[-2]
    return jnp.pad(state, ((0, 0),) * (state.ndim - 2) + ((pad, 0), (0, 0)))


def kernel(x_prompt, x_sample, c_prompt, c_sample, state_pool, state_conv, norm_mix_g, norm_ffn_g, w_ada, b_ada, w_in, ln_v_g, ln_v_b, w_spatial, b_spatial, w_a_out, w_pool, b_pool, pool_scale, w_dw, b_dw, ln_c_g, ln_c_b, w_c_out, w_out, w_router, b_router, w_gate_up, b_gate_up, w_down, b_down, final_norm_g):
    p = dict(norm_mix_g=norm_mix_g, norm_ffn_g=norm_ffn_g, w_in=w_in, ln_v_g=ln_v_g, ln_v_b=ln_v_b,
             w_spatial=w_spatial, b_spatial=b_spatial, w_a_out=w_a_out, w_pool=w_pool, b_pool=b_pool,
             pool_scale=pool_scale, w_dw=w_dw, b_dw=b_dw, ln_c_g=ln_c_g, ln_c_b=ln_c_b, w_c_out=w_c_out,
             w_out=w_out, w_router=w_router, b_router=b_router)
    n_layers = w_in.shape[0]
    bp, sp, d = x_prompt.shape
    bs, ss, _ = x_sample.shape
    tp = min(PROMPT_TILE, sp)
    bts = min(SAMPLE_BT, bs)
    cfg_p = MixerCfg(bt=1, t=tp, ch=GMLP_CHUNK, kg=GMLP_CHUNK, kp=_round_up(POOL_HIST + tp, LANES), d=d,
                     start_pos=0, emit_v=False, fuse_in=False, steps=sp // tp, n_tiles=bp * (sp // tp))
    cfg_s = MixerCfg(bt=bts, t=ss, ch=ss, kg=_round_up(ss, LANES), kp=_round_up(POOL_HIST + ss, LANES), d=d,
                     start_pos=PAST_LEN, emit_v=True, fuse_in=False, steps=1, n_tiles=bs // bts)
    n_p, n_s = bp * sp, bs * ss
    n_slots = (n_p + n_s) * TOP_K + N_EXPERTS * MOE_TM

    mod_all = _ada_call(jnp.concatenate([c_prompt, c_sample], axis=0), w_ada, b_ada)
    mod_p = mod_all[:, :bp].reshape(n_layers, bp, 1, 6 * d)
    mod_s = mod_all[:, bp:].reshape(n_layers, bs, 1, 6 * d)
    final_g = final_norm_g.reshape(1, d)

    hb = bp // 2
    halves = [(0, hb), (hb, bp - hb)]
    steps_p = sp // tp
    xs = x_sample
    parts = [dict(lo=0, n=bp, x=x_prompt)]
    zero_pool = jnp.zeros((bp, POOL_HIST, d), F32)
    zero_conv = jnp.zeros((bp, CONV_HIST, d), F32)
    zero_cnt = jnp.zeros((1, ROUTER_LANES), F32)
    fuse_p, fuse_s = {}, ()
    pools_p, convs_p, pools_s, convs_s, vs = [], [], [], [], []

    def holder(lo):
        return next(pp for pp in parts if pp["lo"] <= lo < pp["lo"] + pp["n"])

    for l in range(n_layers):
        fused = l > 0
        if fused:
            p, _ = lax.optimization_barrier((p, cnt_all))
        lw_p, lw_s = _layer_weights(l, p, (cfg_p, cfg_s))
        cnt = zero_cnt
        new_parts = []
        for lo, n in (halves if fused else [(0, bp)]):
            src = holder(lo)
            x_new, h2, ti, tg, rk, cnt, npool, nconv = _mixer_call(
                cfg_p._replace(fuse_in=fused, n_tiles=n * steps_p), src["x"], lo - src["lo"], mod_p[l, lo:lo + n],
                zero_pool[lo:lo + n], zero_conv[lo:lo + n], lw_p, cnt[0:1], fuse_p.get((lo, n), ()))
            new_parts.append(dict(lo=lo, n=n, x=x_new, h2=h2.reshape(n * sp, d // 2), ti=ti, tg=tg, rk=rk,
                                  npool=npool, nconv=nconv))
        parts = new_parts
        xs, h2s, tis, tgs, rks, cnt_all, npool_s, nconv_s, v_s = _mixer_call(
            cfg_s._replace(fuse_in=fused), xs, 0, mod_s[l], _pad_front(state_pool[l], POOL_HIST),
            _pad_front(state_conv[l], CONV_HIST), lw_s, cnt[0:1], fuse_s)
        top_i = jnp.concatenate([pp["ti"] for pp in parts] + [tis], axis=0)[:, :TOP_K]
        rank = jnp.concatenate([pp["rk"] for pp in parts] + [rks], axis=0)[:, :TOP_K]
        tg_p = jnp.concatenate([pp["tg"] for pp in parts], axis=0)
        slot_of, block_e, nused = _route(top_i, rank, cnt_all[0, :N_EXPERTS].astype(jnp.int32))
        tables = [pp["h2"] for pp in parts] + [h2s.reshape(n_s, d // 2)]
        bounds = [pp["lo"] * sp for pp in parts] + [n_p, n_p + n_s]
        slots = [slot_of[:, bounds[i]:bounds[i + 1]] for i in range(len(tables))]
        x_sorted = _sc_dispatch(tables, slots, n_slots)
        consumers = halves if l < n_layers - 1 else [(0, bp)]
        back_slots = [slot_of[:, lo * sp:(lo + n) * sp].reshape(-1) for lo, n in consumers]
        back_tg = [tg_p[lo * sp:(lo + n) * sp] for lo, n in consumers]
        slots_s = slot_of[:, n_p:].reshape(-1)
        x_sorted, back_slots, back_tg, slots_s = lax.optimization_barrier((x_sorted, back_slots, back_tg, slots_s))
        ys = _moe_call(x_sorted, block_e, nused, l, w_gate_up, b_gate_up, w_down, b_down)
        y4_parts = []
        for s_flat in back_slots[:-1]:
            y4_parts += _sc_collect(ys, [s_flat])
        y4_last, y4s = _sc_collect(ys, [back_slots[-1], slots_s])
        fuse_p = {(lo, n): (y4.reshape(TOP_K, n * sp, d // 2), tg, mod_p[l, lo:lo + n])
                  for (lo, n), y4, tg in zip(consumers, y4_parts + [y4_last], back_tg)}
        fuse_s = (y4s.reshape(TOP_K, n_s, d // 2), tgs, mod_s[l])
        pools_p.append(jnp.concatenate([pp["npool"] for pp in parts], axis=0))
        convs_p.append(jnp.concatenate([pp["nconv"] for pp in parts], axis=0))
        pools_s.append(npool_s)
        convs_s.append(nconv_s)
        vs.append(v_s)
    yp = None
    (y4, tg, _), = fuse_p.values()
    for pp in parts:
        yp = _combine_call(pp["x"], 0, mod_p[-1, pp["lo"]:pp["lo"] + pp["n"]], y4, tg, final_g, 1, tp, bp, pp["lo"],
                           yp)
    ys_out = _combine_call(xs, 0, mod_s[-1], *fuse_s[:2], final_g, bts, ss, bs, 0)
    return (yp, ys_out, jnp.stack(pools_p), jnp.stack(convs_p), jnp.stack(pools_s), jnp.stack(convs_s),
            jnp.stack(vs))
```

```python
import functools
from typing import NamedTuple

import jax
import jax.numpy as jnp
from jax import lax
from jax.experimental import pallas as pl
from jax.experimental.pallas import tpu as pltpu
from jax.experimental.pallas import tpu_sc as plsc

GMLP_CHUNK = 128
GMLP_HEADS = 4
POOL_WINDOWS = (2, 4, 8, 16)
POOL_STATE = max(POOL_WINDOWS) - 1
CONV_K = 31
CONV_STATE = CONV_K - 1
N_BRANCH = 3
N_EXPERTS = 32
TOP_K = 4
SWIGLU_LIMIT = 7.0
SWIGLU_ALPHA = 1.702
EPS = 1e-6
PAST_LEN = 2048

LANES = 128
SUBLANES = 8
V7X_VMEM_LIMIT_BYTES = 56 * 2**20

POOL_HIST = 16
CONV_HIST = 32
ROUTER_LANES = LANES
TOPK_LANES = 8
MOE_TM = 512
PROMPT_TILE = 256
SAMPLE_BT = 8

F32 = jnp.float32
BF16 = jnp.bfloat16


def _round_up(a, m):
    return (a + m - 1) // m * m


class MixerCfg(NamedTuple):
    bt: int
    t: int
    ch: int
    kg: int
    kp: int
    d: int
    start_pos: int
    emit_v: bool
    fuse_in: bool
    steps: int
    n_tiles: int


def _rms(x, g):
    return x * lax.rsqrt(jnp.mean(x * x, axis=-1, keepdims=True) + EPS) * g


def _ln(x, g, b):
    mu = jnp.mean(x, axis=-1, keepdims=True)
    xc = x - mu
    var = jnp.mean(xc * xc, axis=-1, keepdims=True)
    return xc * lax.rsqrt(var + EPS) * g + b


def _dot(a, b):
    return jnp.dot(a, b, preferred_element_type=F32)


def _pack_rows(x):
    half = x.shape[1] // 2
    hi = lax.bitcast_convert_type(x[:, :half].astype(BF16).astype(F32), jnp.int32)
    lo = lax.bitcast_convert_type(x[:, half:].astype(BF16).astype(F32), jnp.int32)
    return hi | lax.shift_right_logical(lo, 16)


def _unpack_rows(w):
    hi = lax.bitcast_convert_type(w & jnp.int32(-65536), F32)
    lo = lax.bitcast_convert_type(lax.shift_left(w, 16), F32)
    return jnp.concatenate([hi, lo], axis=1)


def _moe_mix(y4_ref, tg):
    f = tg[:, 0:1] * _unpack_rows(y4_ref[0])
    for k in range(1, TOP_K):
        f = f + tg[:, k:k + 1] * _unpack_rows(y4_ref[k])
    return f


def _mixer_kernel(cfg, x_ref, mod_ref, modb_ref, pp_ref, pc_ref, nmg_ref, nfg_ref, win_ref, lnvg_ref, lnvb_ref,
                  wsp_ref, bsf_ref, wao_ref, wpool_ref, bpool_ref, pscale_ref, wdw_ref, bdw_ref,
                  lncg_ref, lncb_ref, wco_ref, wout_ref, wrh_ref, wrl_ref, br_ref, pmat_ref, ltri_ref, cnt0_ref,
                  *rest):
    rest = list(rest)
    if cfg.fuse_in:
        y4_ref, tgp_ref, modp_ref = rest[:3]
        rest = rest[3:]
    xo_ref, h2_ref, ti_ref, tg_ref, rk_ref, cnt_ref, npool_ref, nconv_ref = rest[:8]
    rest = rest[8:]
    if cfg.emit_v:
        v_ref, hp_scr, xc_scr, run_scr, mg_scr, xr_scr = rest
    else:
        hp_scr, xc_scr, run_scr, mg_scr, xr_scr = rest
    bt, t, d = cfg.bt, cfg.t, cfg.d
    m = bt * t
    n_slab = d // LANES
    gd = d // len(POOL_WINDOWS)
    hd = d // GMLP_HEADS
    q = pl.program_id(0)
    j = 0 if cfg.steps == 1 else lax.rem(jnp.minimum(q, cfg.n_tiles - 1), cfg.steps)

    def mod_rows(k, ref=mod_ref):
        r = ref[:, :, k * d:(k + 1) * d]
        if bt == 1:
            return r.reshape(1, d)
        return jnp.broadcast_to(r, (bt, t, d)).reshape(m, d)

    @pl.when(q == 0)
    def _init_carries():
        run_scr[...] = cnt0_ref[...]
        mg_scr[...] = jnp.zeros_like(mg_scr)
        xr_scr[...] = jnp.zeros_like(xr_scr)

    @pl.when(j == 0)
    def _load_state():
        hp_scr[...] = pp_ref[...]
        for bi in range(bt):
            for c in range(n_slab):
                xc_scr[bi, c, 0:CONV_HIST, :] = pc_ref[bi, :, c * LANES:(c + 1) * LANES]

    x_prev = xr_scr[...]
    out_prev = _dot(mg_scr[...], wout_ref[...])

    x = x_ref[...].reshape(m, d)
    if cfg.fuse_in:
        x = x + mod_rows(5, modp_ref) * _moe_mix(y4_ref, tgp_ref[...])
    sh1, sc1 = mod_rows(0), mod_rows(1)
    h = (_rms(x, nmg_ref[...]) * (1.0 + sc1) + sh1).astype(BF16)

    def zcols(lo, hi):
        return _dot(h, win_ref[:, lo:hi])

    off_u, off_v, off_b, off_c, off_g = 0, d, 2 * d, 3 * d, 5 * d
    zc_a = zcols(off_c, off_c + d)
    zc_b = zcols(off_c + d, off_c + 2 * d)

    x_new = x_prev + mod_rows(2, modb_ref) * out_prev
    xo_ref[...] = x_new.reshape(bt, t, d)
    h2 = _rms(x_new, nfg_ref[...]) * (1.0 + mod_rows(4, modb_ref)) + mod_rows(3, modb_ref)
    h2_ref[...] = _pack_rows(h2).reshape(bt, t, d // 2)
    h2_hi = h2.astype(BF16)
    h2_lo = (h2 - h2_hi.astype(F32)).astype(BF16)
    logits = _dot(h2_hi, wrh_ref[...]) + _dot(h2_lo, wrh_ref[...]) + _dot(h2_hi, wrl_ref[...]) + br_ref[...]

    x_c = zc_a * jax.nn.sigmoid(zc_b)
    for bi in range(bt):
        for c in range(n_slab):
            xc_scr[bi, c, CONV_HIST:CONV_HIST + t, :] = x_c[bi * t:(bi + 1) * t, c * LANES:(c + 1) * LANES]

    lane_r = lax.broadcasted_iota(jnp.int32, (m, ROUTER_LANES), 1).astype(F32)
    lane_k = lax.broadcasted_iota(jnp.int32, (m, TOPK_LANES), 1)
    topk = dict(work=logits, ti=jnp.zeros((m, TOPK_LANES), jnp.int32), vals=[], onehots=[])

    def next_topk_round():
        r = len(topk["vals"])
        if r < TOP_K:
            mx = jnp.max(topk["work"], axis=-1, keepdims=True)
            idx = jnp.min(jnp.where(topk["work"] == mx, lane_r, float(ROUTER_LANES)), axis=-1, keepdims=True)
            topk["ti"] = jnp.where(lane_k == r, idx.astype(jnp.int32), topk["ti"])
            topk["vals"].append(mx)
            topk["work"] = jnp.where(lane_r == idx, -jnp.inf, topk["work"])
            topk["onehots"].append((lane_r == idx).astype(F32))

    proj_cols = dict(ga=off_g, gb=off_g + d, gc=off_g + 2 * d, zu=off_u, zv=off_v, zb=off_b)
    pending_proj = list(proj_cols)
    proj = {}

    def next_proj():
        if pending_proj:
            name = pending_proj.pop(0)
            proj[name] = zcols(proj_cols[name], proj_cols[name] + d)

    conv_parts = []
    for bi in range(bt):
        accs = []
        for c in range(n_slab):
            cs = slice(c * LANES, (c + 1) * LANES)
            acc = jnp.broadcast_to(bdw_ref[:, cs], (t, LANES))
            for k in range(CONV_K):
                acc = acc + wdw_ref[k:k + 1, cs] * xc_scr[bi, c, pl.ds(CONV_HIST - CONV_STATE + k, t), :]
            accs.append(acc)
            next_proj()
            next_topk_round()
        conv_parts.append(jnp.concatenate(accs, axis=1))
    for bi in range(bt):
        for c in range(n_slab):
            cs = slice(c * LANES, (c + 1) * LANES)
            tail = xc_scr[bi, c, t:t + CONV_HIST, :]
            nconv_ref[bi, :, cs] = xc_scr[bi, c, pl.ds(t + CONV_HIST - CONV_STATE, CONV_STATE), :]
            xc_scr[bi, c, 0:CONV_HIST, :] = tail
    while pending_proj:
        next_proj()
    while len(topk["vals"]) < TOP_K:
        next_topk_round()
    conv = conv_parts[0] if bt == 1 else jnp.concatenate(conv_parts, axis=0)

    top_vals, onehots = topk["vals"], topk["onehots"]
    exps = [jnp.exp(tv - top_vals[0]) for tv in top_vals]
    denom = exps[0] + exps[1] + exps[2] + exps[3]
    tg = jnp.zeros((m, TOPK_LANES), F32)
    for r in range(TOP_K):
        tg = jnp.where(lane_k == r, exps[r] / denom, tg)
    ti_ref[...] = topk["ti"]
    tg_ref[...] = tg
    base = run_scr[...]
    rk = jnp.zeros((m, TOPK_LANES), jnp.int32)
    new_base = base
    for r in range(TOP_K):
        before = _dot(ltri_ref[...], onehots[r].astype(BF16)) + new_base
        rank = jnp.sum(onehots[r] * before, axis=-1, keepdims=True)
        rk = jnp.where(lane_k == r, rank.astype(jnp.int32), rk)
        new_base = new_base + jnp.sum(onehots[r], axis=0, keepdims=True)
    new_base = jnp.where(q >= 1, new_base, base)
    run_scr[...] = new_base
    rk_ref[...] = rk
    cnt_ref[...] = jnp.broadcast_to(new_base, cnt_ref.shape)

    zb = proj["zb"]
    lane = lax.broadcasted_iota(jnp.int32, (1, d), 1)
    win_lane = jnp.left_shift(2, lane // gd).astype(F32)
    pos1 = (cfg.start_pos + 1 + j * t + lax.broadcasted_iota(jnp.int32, (t, 1), 0)).astype(F32)
    cnt = jnp.minimum(pos1, win_lane)
    pooled_parts = []
    for bi in range(bt):
        zb_b = zb[bi * t:(bi + 1) * t]
        pieces = [hp_scr[bi], zb_b]
        if cfg.kp > POOL_HIST + t:
            pieces = [jnp.zeros((cfg.kp - POOL_HIST - t, d), F32)] + pieces
        full = jnp.concatenate(pieces, axis=0)
        fullb = full.astype(BF16)
        sums = jnp.concatenate(
            [_dot(pmat_ref[g], fullb[:, g * gd:(g + 1) * gd]) for g in range(len(POOL_WINDOWS))], axis=1)
        pooled_parts.append(sums / cnt - zb_b)
        hp_scr[bi] = full[cfg.kp - POOL_HIST:]
        npool_ref[bi] = hp_scr[bi, pl.ds(POOL_HIST - POOL_STATE, POOL_STATE), :]
    pooled = (pooled_parts[0] if bt == 1 else jnp.concatenate(pooled_parts, axis=0)).astype(BF16)
    y_b = jnp.concatenate(
        [_dot(pooled[:, g * gd:(g + 1) * gd], wpool_ref[g]) for g in range(len(POOL_WINDOWS))], axis=1)
    y_b = (y_b + bpool_ref[...]) * pscale_ref[...]

    v = _ln(jax.nn.gelu(proj["zv"]), lnvg_ref[...], lnvb_ref[...])
    if cfg.emit_v:
        v_ref[...] = v.reshape(bt, t, d)
    vb = v.astype(BF16)
    mixed_rows = []
    for r0 in range(0, m, cfg.ch):
        vc = vb[r0:r0 + cfg.ch]
        if cfg.kg > cfg.ch:
            vc = jnp.concatenate([vc, jnp.zeros((cfg.kg - cfg.ch, d), BF16)], axis=0)
        mixed_rows.append(jnp.concatenate(
            [_dot(wsp_ref[hh], vc[:, hh * hd:(hh + 1) * hd]) for hh in range(GMLP_HEADS)], axis=1) + bsf_ref[...])
    mixed = mixed_rows[0] if len(mixed_rows) == 1 else jnp.concatenate(mixed_rows, axis=0)

    y_c = _dot(jax.nn.silu(_ln(conv, lncg_ref[...], lncb_ref[...])).astype(BF16), wco_ref[...])
    u = jax.nn.gelu(proj["zu"])
    y_a = _dot((u * mixed).astype(BF16), wao_ref[...])
    merged = jax.nn.sigmoid(proj["gb"]) * y_b
    merged = merged + jax.nn.sigmoid(proj["ga"]) * y_a
    merged = merged + jax.nn.sigmoid(proj["gc"]) * y_c

    mg_scr[...] = merged.astype(BF16)
    xr_scr[...] = x


def _const_spec(shape):
    nd = len(shape)
    return pl.BlockSpec(shape, lambda q, _nd=nd: (0,) * _nd, pipeline_mode=pl.Buffered(1))


def _pool_band(t, kp):
    col = jnp.arange(kp)[None, :]
    end = (kp - t) + jnp.arange(t)[:, None]
    return jnp.stack([((col <= end) & (col > end - w)) for w in POOL_WINDOWS]).astype(BF16)


def _mixer_call(cfg, x, b0, mod, prev_pool, prev_conv, lw, cnt0, fuse=()):
    _, s, d = x.shape
    b = mod.shape[0]
    bt, t = cfg.bt, cfg.t
    m = bt * t
    steps, n_tiles = cfg.steps, cfg.n_tiles
    assert steps == s // t and n_tiles == (b // bt) * steps and b0 % bt == 0
    assert steps == 1 or t >= max(CONV_STATE, POOL_STATE)
    n_slab = d // LANES

    def front(q):
        return jnp.minimum(q, n_tiles - 1)

    def back(q):
        return jnp.maximum(q - 1, 0)

    def tile_spec(which, width=d):
        return pl.BlockSpec((bt, t, width), lambda q: (which(q) // steps, which(q) % steps, 0))

    def batch_spec(which, rows, width):
        return pl.BlockSpec((bt, rows, width), lambda q: (which(q) // steps, 0, 0))

    def rows_spec(which, width):
        return pl.BlockSpec((m, width), lambda q: (which(q), 0))

    consts = [lw["norm_mix_g"], lw["norm_ffn_g"], lw["w_in"], lw["ln_v_g"], lw["ln_v_b"], lw["w_sp"], lw["bs_full"],
              lw["w_a_out"], lw["w_pool"], lw["b_pool"], lw["pool_scale"], lw["w_dw"], lw["b_dw"], lw["ln_c_g"],
              lw["ln_c_b"], lw["w_c_out"], lw["w_out"], lw["wr_hi"], lw["wr_lo"], lw["b_router"], lw["pmat"],
              lw["ltri"], cnt0]
    x_spec = pl.BlockSpec((bt, t, d), lambda q: (front(q) // steps + b0 // bt, front(q) % steps, 0))
    in_specs = [x_spec, batch_spec(front, 1, 6 * d), batch_spec(back, 1, 6 * d),
                batch_spec(front, POOL_HIST, d), batch_spec(front, CONV_HIST, d)]
    in_specs += [_const_spec(c.shape) for c in consts]
    if cfg.fuse_in:
        in_specs += [pl.BlockSpec((TOP_K, m, d // 2), lambda q: (0, front(q), 0)),
                     rows_spec(front, TOPK_LANES), batch_spec(front, 1, 6 * d)]
    out_shape = [jax.ShapeDtypeStruct((b, s, d), F32), jax.ShapeDtypeStruct((b, s, d // 2), jnp.int32),
                 jax.ShapeDtypeStruct((b * s, TOPK_LANES), jnp.int32), jax.ShapeDtypeStruct((b * s, TOPK_LANES), F32),
                 jax.ShapeDtypeStruct((b * s, TOPK_LANES), jnp.int32),
                 jax.ShapeDtypeStruct((SUBLANES, ROUTER_LANES), F32),
                 jax.ShapeDtypeStruct((b, POOL_STATE, d), F32), jax.ShapeDtypeStruct((b, CONV_STATE, d), F32)]
    out_specs = [tile_spec(back), tile_spec(back, d // 2),
                 rows_spec(back, TOPK_LANES), rows_spec(back, TOPK_LANES), rows_spec(back, TOPK_LANES),
                 pl.BlockSpec((SUBLANES, ROUTER_LANES), lambda q: (0, 0)),
                 batch_spec(front, POOL_STATE, d), batch_spec(front, CONV_STATE, d)]
    if cfg.emit_v:
        out_shape.append(jax.ShapeDtypeStruct((b, s, d), F32))
        out_specs.append(tile_spec(front))
    return pl.pallas_call(
        functools.partial(_mixer_kernel, cfg),
        grid=(n_tiles + 1,),
        in_specs=in_specs,
        out_specs=out_specs,
        out_shape=out_shape,
        scratch_shapes=[pltpu.VMEM((bt, POOL_HIST, d), F32),
                        pltpu.VMEM((bt, n_slab, CONV_HIST + t, LANES), F32),
                        pltpu.VMEM((1, ROUTER_LANES), F32),
                        pltpu.VMEM((m, d), BF16), pltpu.VMEM((m, d), F32)],
        compiler_params=pltpu.CompilerParams(dimension_semantics=("arbitrary",),
                                             vmem_limit_bytes=V7X_VMEM_LIMIT_BYTES),
        name=f"mixer_t{t}",
    )(x, mod, mod, prev_pool, prev_conv, *consts, *fuse)


def _ada_kernel(c_ref, w_ref, b_ref, o_ref):
    o_ref[0] = _dot(jax.nn.silu(c_ref[...]).astype(BF16), w_ref[0].astype(BF16)) + b_ref[0]


def _ada_call(c_all, w_ada, b_ada):
    n_layers, d, six_d = w_ada.shape
    rows = c_all.shape[0]
    bn = six_d // 6
    return pl.pallas_call(
        _ada_kernel,
        grid=(n_layers, six_d // bn),
        in_specs=[pl.BlockSpec((rows, d), lambda l, n: (0, 0)),
                  pl.BlockSpec((1, d, bn), lambda l, n: (l, 0, n)),
                  pl.BlockSpec((1, 1, bn), lambda l, n: (l, 0, n))],
        out_specs=pl.BlockSpec((1, rows, bn), lambda l, n: (l, 0, n)),
        out_shape=jax.ShapeDtypeStruct((n_layers, rows, six_d), F32),
        compiler_params=pltpu.CompilerParams(dimension_semantics=("arbitrary", "arbitrary")),
        name="adaln",
    )(c_all, w_ada, b_ada.reshape(n_layers, 1, six_d))


def _moe_kernel(be_ref, nused_ref, x_ref, wgu_ref, bgu_ref, wdn_ref, bdn_ref, y_ref, wgu_bf, wdn_bf):
    i = pl.program_id(0)
    nused = nused_ref[0]
    dff = wdn_bf.shape[0]

    @pl.when(i < nused)
    def _body():
        changed = jnp.logical_or(i == 0, be_ref[i] != be_ref[jnp.maximum(i - 1, 0)])

        @pl.when(changed)
        def _cast_weights():
            wgu_bf[...] = wgu_ref[0].astype(BF16)
            wdn_bf[...] = wdn_ref[0].astype(BF16)

        gu = _dot(_unpack_rows(x_ref[...]).astype(BF16), wgu_bf[...]) + bgu_ref[0]
        g = jnp.minimum(gu[:, :dff], SWIGLU_LIMIT)
        u = jnp.clip(gu[:, dff:], -SWIGLU_LIMIT, SWIGLU_LIMIT)
        act = (u + 1.0) * g * jax.nn.sigmoid(SWIGLU_ALPHA * g)
        y_ref[...] = _pack_rows(_dot(act.astype(BF16), wdn_bf[...]) + bdn_ref[0])

    @pl.when(i >= nused)
    def _unused_block():
        y_ref[...] = jnp.zeros_like(y_ref)


def _moe_call(xs, block_e, nused, layer, w_gu_all, b_gu_all, w_dn_all, b_dn_all):
    n_blocks = block_e.shape[0]
    n_layers, n_exp, d, two_f = w_gu_all.shape
    dff = two_f // 2
    tm = MOE_TM
    e0 = layer * n_exp
    w_gu = w_gu_all.reshape(n_layers * n_exp, d, two_f)
    b_gu = b_gu_all.reshape(n_layers * n_exp, 1, two_f)
    w_dn = w_dn_all.reshape(n_layers * n_exp, dff, d)
    b_dn = b_dn_all.reshape(n_layers * n_exp, 1, d)
    grid_spec = pltpu.PrefetchScalarGridSpec(
        num_scalar_prefetch=2,
        grid=(n_blocks,),
        in_specs=[
            pl.BlockSpec((tm, d // 2), lambda i, be, nu: (i, 0)),
            pl.BlockSpec((1, d, two_f), lambda i, be, nu: (e0 + be[i], 0, 0)),
            pl.BlockSpec((1, 1, two_f), lambda i, be, nu: (e0 + be[i], 0, 0)),
            pl.BlockSpec((1, dff, d), lambda i, be, nu: (e0 + be[i], 0, 0)),
            pl.BlockSpec((1, 1, d), lambda i, be, nu: (e0 + be[i], 0, 0)),
        ],
        out_specs=pl.BlockSpec((tm, d // 2), lambda i, be, nu: (i, 0)),
        scratch_shapes=[pltpu.VMEM((d, two_f), BF16), pltpu.VMEM((dff, d), BF16)],
    )
    return pl.pallas_call(
        _moe_kernel,
        grid_spec=grid_spec,
        out_shape=jax.ShapeDtypeStruct((n_blocks * tm, d // 2), jnp.int32),
        compiler_params=pltpu.CompilerParams(dimension_semantics=("arbitrary",),
                                             vmem_limit_bytes=V7X_VMEM_LIMIT_BYTES),
        name="moe_experts",
    )(block_e, nused, xs, w_gu, b_gu, w_dn, b_dn)


def _route(top_i, rank, counts):
    tm = MOE_TM
    n_all = top_i.shape[0]
    n_blocks = n_all * TOP_K // tm + N_EXPERTS
    padded = (counts + tm - 1) // tm * tm
    pad_ends = jnp.cumsum(padded)
    pad_starts = pad_ends - padded
    blk = jnp.arange(n_blocks, dtype=jnp.int32)[:, None] * tm
    block_e = jnp.minimum(jnp.sum((pad_ends[None, :] <= blk).astype(jnp.int32), axis=1), N_EXPERTS - 1)
    nused = (pad_ends[-1] // tm).astype(jnp.int32).reshape(1)
    onehot = top_i[:, :, None] == jnp.arange(N_EXPERTS, dtype=jnp.int32)[None, None, :]
    slot_of = jnp.sum(jnp.where(onehot, pad_starts[None, None, :], 0), axis=-1) + rank
    return slot_of.T.astype(jnp.int32), block_e.astype(jnp.int32), nused


SC_CORES = 2
SC_SUBCORES = 16
SC_WORKERS = SC_CORES * SC_SUBCORES
SC_CHUNK = 64


def _sc_worker_id():
    return lax.axis_index("s") * SC_CORES + lax.axis_index("c")


def _sc_plan(n_rows, split=False):
    per_w = n_rows // SC_WORKERS
    ch = min(SC_CHUNK, per_w // 2 if split else per_w)
    n_ch = per_w // ch
    assert per_w * SC_WORKERS == n_rows and n_ch * ch == per_w
    return per_w, ch, n_ch, (n_ch >= 2 and n_ch % 2 == 0)


def _sc_dispatch(tables, slots, n_slots):
    nt = len(tables)
    d, dtype = tables[0].shape[1], tables[0].dtype
    plans = [_sc_plan(h.shape[0]) for h in tables]

    buf_shapes = sorted({(2 if piped else 1, ch) for (_, ch, _, piped) in plans})

    def body(*refs):
        h_refs, idx_refs, out_hbm = refs[:nt], refs[nt:2 * nt], refs[2 * nt]
        scratch = refs[2 * nt + 1:]
        idx_vs, shared, (rsem, wsem) = scratch[:nt], scratch[nt:-2], scratch[-2:]
        bufs = [shared[buf_shapes.index((2 if piped else 1, ch))] for (_, ch, _, piped) in plans]
        wid = _sc_worker_id()
        for h_hbm, idx_hbm, idx_v, buf, (tpw, ch, n_ch, piped) in zip(h_refs, idx_refs, idx_vs, bufs, plans):
            pltpu.sync_copy(idx_hbm.at[wid], idx_v)

            def read(c, b, h_hbm=h_hbm, buf=buf, tpw=tpw, ch=ch):
                return pltpu.make_async_copy(h_hbm.at[pl.ds(wid * tpw + c * ch, ch)], buf.at[b], rsem.at[b])

            def write(c, k, b, buf=buf, idx_v=idx_v):
                return pltpu.make_async_copy(buf.at[b], out_hbm.at[idx_v.at[c * TOP_K + k]], wsem.at[b])

            if piped:
                read(0, 0).start()

                @pl.loop(0, n_ch, step=2)
                def _(c0, read=read, write=write, n_ch=n_ch):
                    for b in range(2):
                        c = c0 + b

                        @pl.when(c + 1 < n_ch)
                        def _():
                            @pl.when(c >= 1)
                            def _():
                                for k in range(TOP_K):
                                    write(c - 1, k, 1 - b).wait()
                            read(c + 1, 1 - b).start()

                        read(c, b).wait()
                        for k in range(TOP_K):
                            write(c, k, b).start()

                for c in (n_ch - 2, n_ch - 1):
                    for k in range(TOP_K):
                        write(c, k, c % 2).wait()
            else:
                for c in range(n_ch):
                    cp_in = read(c, 0)
                    cp_in.start()
                    cp_in.wait()
                    for k in range(TOP_K):
                        write(c, k, 0).start()
                    for k in range(TOP_K):
                        write(c, k, 0).wait()

    scratch_types = [pltpu.VMEM((n_ch * TOP_K, ch), jnp.int32) for (_, ch, n_ch, _) in plans]
    scratch_types += [pltpu.VMEM((nbuf, ch, d), dtype) for (nbuf, ch) in buf_shapes]
    scratch_types += [pltpu.SemaphoreType.DMA((2,)), pltpu.SemaphoreType.DMA((2,))]
    call = pl.kernel(
        body,
        out_type=jax.ShapeDtypeStruct((n_slots, d), dtype),
        mesh=plsc.VectorSubcoreMesh(core_axis_name="c", subcore_axis_name="s"),
        scratch_types=scratch_types,
        name="sc_dispatch",
    )
    idx = [s.reshape(TOP_K, SC_WORKERS, n_ch, ch).transpose(1, 2, 0, 3).reshape(SC_WORKERS, n_ch * TOP_K, ch)
           for s, (_, ch, n_ch, _) in zip(slots, plans)]
    return call(*tables, *idx)


def _sc_collect(ys, slots):
    nt = len(slots)
    d, dtype = ys.shape[1], ys.dtype
    plans = [_sc_plan(s.size, split=True) for s in slots]
    assert all(piped for (_, _, _, piped) in plans)

    def body(*refs):
        ys_hbm, idx_refs, out_refs = refs[0], refs[1:1 + nt], refs[1 + nt:1 + 2 * nt]
        scratch = refs[1 + 2 * nt:]
        idx_vs, bufs, (gsem, wsem) = scratch[:nt], scratch[nt:2 * nt], scratch[2 * nt:]
        wid = _sc_worker_id()
        for idx_hbm, out_hbm, idx_v, rows_v, (per_w, ch, n_ch, _) in zip(idx_refs, out_refs, idx_vs, bufs, plans):
            base = wid * per_w
            pltpu.sync_copy(idx_hbm.at[pl.ds(base, per_w)], idx_v)

            def gather(g, b, idx_v=idx_v, rows_v=rows_v, ch=ch):
                return pltpu.make_async_copy(ys_hbm.at[idx_v.at[pl.ds(g * ch, ch)]], rows_v.at[b], gsem.at[b])

            def put(g, b, out_hbm=out_hbm, rows_v=rows_v, base=base, ch=ch):
                return pltpu.make_async_copy(rows_v.at[b], out_hbm.at[pl.ds(base + g * ch, ch)], wsem.at[b])

            gather(0, 0).start()

            @pl.loop(0, n_ch, step=2)
            def _(g0, gather=gather, put=put, n_ch=n_ch):
                for b in range(2):
                    g = g0 + b

                    @pl.when(g + 1 < n_ch)
                    def _():
                        @pl.when(g >= 1)
                        def _():
                            put(g - 1, 1 - b).wait()
                        gather(g + 1, 1 - b).start()

                    gather(g, b).wait()
                    put(g, b).start()

            for g in (n_ch - 2, n_ch - 1):
                put(g, g % 2).wait()

    call = pl.kernel(
        body,
        out_type=tuple(jax.ShapeDtypeStruct((s.size, d), dtype) for s in slots),
        mesh=plsc.VectorSubcoreMesh(core_axis_name="c", subcore_axis_name="s"),
        scratch_types=[pltpu.VMEM((per_w,), jnp.int32) for (per_w, _, _, _) in plans]
        + [pltpu.VMEM((2, ch, d), dtype) for (_, ch, _, _) in plans]
        + [pltpu.SemaphoreType.DMA((2,)), pltpu.SemaphoreType.DMA((2,))],
        name="sc_collect",
    )
    out = call(ys, *[s.reshape(s.size) for s in slots])
    return out if isinstance(out, (tuple, list)) else (out,)


def _combine_kernel(x_ref, mod_ref, y_ref, tg_ref, fg_ref, *rest):
    o_ref = rest[-1]
    bt, t, d = x_ref.shape
    m = bt * t
    g2 = mod_ref[:, :, 5 * d:6 * d]
    g2 = g2.reshape(1, d) if bt == 1 else jnp.broadcast_to(g2, (bt, t, d)).reshape(m, d)
    out = x_ref[...].reshape(m, d) + g2 * _moe_mix(y_ref, tg_ref[...])
    o_ref[...] = _rms(out, fg_ref[...]).reshape(bt, t, d)


def _combine_call(x_new, b0, mod, y4, tg, final_g, bt, t, out_batch, out_b0, out_prev=None):
    _, s, d = x_new.shape
    b = mod.shape[0]
    m = bt * t
    steps = s // t
    ob = out_b0 // bt
    in_specs = [pl.BlockSpec((bt, t, d), lambda bi, j: (bi + b0 // bt, j, 0)),
                pl.BlockSpec((bt, 1, 6 * d), lambda bi, j: (bi, 0, 0)),
                pl.BlockSpec((TOP_K, m, d // 2), lambda bi, j: (0, (bi + ob) * steps + j, 0)),
                pl.BlockSpec((m, TOPK_LANES), lambda bi, j: ((bi + ob) * steps + j, 0)),
                pl.BlockSpec((1, d), lambda bi, j: (0, 0))]
    operands = [x_new, mod, y4, tg, final_g]
    aliases = {}
    if out_prev is not None:
        in_specs.append(pl.BlockSpec(memory_space=pl.ANY))
        operands.append(out_prev)
        aliases = {len(operands) - 1: 0}
    return pl.pallas_call(
        _combine_kernel,
        grid=(b // bt, steps),
        in_specs=in_specs,
        out_specs=pl.BlockSpec((bt, t, d), lambda bi, j: (bi + out_b0 // bt, j, 0)),
        out_shape=jax.ShapeDtypeStruct((out_batch, s, d), F32),
        input_output_aliases=aliases,
        compiler_params=pltpu.CompilerParams(dimension_semantics=("arbitrary", "arbitrary")),
        name=f"combine_t{t}",
    )(*operands)


def _layer_weights(l, p, cfgs):
    d = p["w_out"].shape[-1]
    row = lambda v: v[l].reshape(1, d)
    wr = p["w_router"][l]
    wr_pad = jnp.zeros((d, ROUTER_LANES), F32).at[:, :N_EXPERTS].set(wr)
    wr_hi = wr_pad.astype(BF16)
    wr_lo = (wr_pad - wr_hi.astype(F32)).astype(BF16)
    b_router = jnp.full((1, ROUTER_LANES), -1e30, F32).at[0, :N_EXPERTS].set(p["b_router"][l])
    mask = jnp.tril(jnp.ones((GMLP_CHUNK, GMLP_CHUNK), dtype=bool))
    w_sp_full = jnp.where(mask[None], p["w_spatial"][l], 0)
    hd = d // GMLP_HEADS
    base = dict(norm_mix_g=row(p["norm_mix_g"]), norm_ffn_g=row(p["norm_ffn_g"]), w_in=p["w_in"][l].astype(BF16),
                ln_v_g=row(p["ln_v_g"]), ln_v_b=row(p["ln_v_b"]), w_a_out=p["w_a_out"][l].astype(BF16),
                w_pool=p["w_pool"][l].astype(BF16), b_pool=row(p["b_pool"]), pool_scale=row(p["pool_scale"]),
                w_dw=jnp.zeros((_round_up(CONV_K, SUBLANES), d), F32).at[:CONV_K].set(p["w_dw"][l]),
                b_dw=row(p["b_dw"]), ln_c_g=row(p["ln_c_g"]), ln_c_b=row(p["ln_c_b"]),
                w_c_out=p["w_c_out"][l].astype(BF16), w_out=p["w_out"][l].astype(BF16),
                wr_hi=wr_hi, wr_lo=wr_lo, b_router=b_router)
    out = []
    for cfg in cfgs:
        w_sp = jnp.zeros((GMLP_HEADS, cfg.ch, cfg.kg), F32).at[:, :, :cfg.ch].set(w_sp_full[:, :cfg.ch, :cfg.ch])
        bs_full = jnp.repeat(p["b_spatial"][l][:, :cfg.ch].T, hd, axis=1)
        m = cfg.bt * cfg.t
        ltri = (jnp.arange(m)[:, None] > jnp.arange(m)[None, :]).astype(BF16)
        out.append(dict(base, w_sp=w_sp.astype(BF16), bs_full=bs_full, pmat=_pool_band(cfg.t, cfg.kp), ltri=ltri))
    return out


def _pad_front(state, rows):
    pad = rows - state.shape[-2]
    return jnp.pad(state, ((0, 0),) * (state.ndim - 2) + ((pad, 0), (0, 0)))


def kernel(x_prompt, x_sample, c_prompt, c_sample, state_pool, state_conv, norm_mix_g, norm_ffn_g, w_ada, b_ada, w_in, ln_v_g, ln_v_b, w_spatial, b_spatial, w_a_out, w_pool, b_pool, pool_scale, w_dw, b_dw, ln_c_g, ln_c_b, w_c_out, w_out, w_router, b_router, w_gate_up, b_gate_up, w_down, b_down, final_norm_g):
    p = dict(norm_mix_g=norm_mix_g, norm_ffn_g=norm_ffn_g, w_in=w_in, ln_v_g=ln_v_g, ln_v_b=ln_v_b,
             w_spatial=w_spatial, b_spatial=b_spatial, w_a_out=w_a_out, w_pool=w_pool, b_pool=b_pool,
             pool_scale=pool_scale, w_dw=w_dw, b_dw=b_dw, ln_c_g=ln_c_g, ln_c_b=ln_c_b, w_c_out=w_c_out,
             w_out=w_out, w_router=w_router, b_router=b_router)
    n_layers = w_in.shape[0]
    bp, sp, d = x_prompt.shape
    bs, ss, _ = x_sample.shape
    tp = min(PROMPT_TILE, sp)
    bts = min(SAMPLE_BT, bs)
    cfg_p = MixerCfg(bt=1, t=tp, ch=GMLP_CHUNK, kg=GMLP_CHUNK, kp=_round_up(POOL_HIST + tp, LANES), d=d,
                     start_pos=0, emit_v=False, fuse_in=False, steps=sp // tp, n_tiles=bp * (sp // tp))
    cfg_s = MixerCfg(bt=bts, t=ss, ch=ss, kg=_round_up(ss, LANES), kp=_round_up(POOL_HIST + ss, LANES), d=d,
                     start_pos=PAST_LEN, emit_v=True, fuse_in=False, steps=1, n_tiles=bs // bts)
    n_p, n_s = bp * sp, bs * ss
    n_slots = (n_p + n_s) * TOP_K + N_EXPERTS * MOE_TM

    mod_all = _ada_call(jnp.concatenate([c_prompt, c_sample], axis=0), w_ada, b_ada)
    mod_p = mod_all[:, :bp].reshape(n_layers, bp, 1, 6 * d)
    mod_s = mod_all[:, bp:].reshape(n_layers, bs, 1, 6 * d)
    final_g = final_norm_g.reshape(1, d)

    hb = bp // 2
    halves = [(0, hb), (hb, bp - hb)]
    steps_p = sp // tp
    xs = x_sample
    parts = [dict(lo=0, n=bp, x=x_prompt)]
    zero_pool = jnp.zeros((bp, POOL_HIST, d), F32)
    zero_conv = jnp.zeros((bp, CONV_HIST, d), F32)
    zero_cnt = jnp.zeros((1, ROUTER_LANES), F32)
    fuse_p, fuse_s = {}, ()
    pools_p, convs_p, pools_s, convs_s, vs = [], [], [], [], []

    def holder(lo):
        return next(pp for pp in parts if pp["lo"] <= lo < pp["lo"] + pp["n"])

    for l in range(n_layers):
        fused = l > 0
        if fused:
            p, _ = lax.optimization_barrier((p, cnt_all))
        lw_p, lw_s = _layer_weights(l, p, (cfg_p, cfg_s))
        cnt = zero_cnt
        new_parts = []
        for lo, n in (halves if fused else [(0, bp)]):
            src = holder(lo)
            x_new, h2, ti, tg, rk, cnt, npool, nconv = _mixer_call(
                cfg_p._replace(fuse_in=fused, n_tiles=n * steps_p), src["x"], lo - src["lo"], mod_p[l, lo:lo + n],
                zero_pool[lo:lo + n], zero_conv[lo:lo + n], lw_p, cnt[0:1], fuse_p.get((lo, n), ()))
            new_parts.append(dict(lo=lo, n=n, x=x_new, h2=h2.reshape(n * sp, d // 2), ti=ti, tg=tg, rk=rk,
                                  npool=npool, nconv=nconv))
        parts = new_parts
        xs, h2s, tis, tgs, rks, cnt_all, npool_s, nconv_s, v_s = _mixer_call(
            cfg_s._replace(fuse_in=fused), xs, 0, mod_s[l], _pad_front(state_pool[l], POOL_HIST),
            _pad_front(state_conv[l], CONV_HIST), lw_s, cnt[0:1], fuse_s)
        top_i = jnp.concatenate([pp["ti"] for pp in parts] + [tis], axis=0)[:, :TOP_K]
        rank = jnp.concatenate([pp["rk"] for pp in parts] + [rks], axis=0)[:, :TOP_K]
        tg_p = jnp.concatenate([pp["tg"] for pp in parts], axis=0)
        slot_of, block_e, nused = _route(top_i, rank, cnt_all[0, :N_EXPERTS].astype(jnp.int32))
        tables = [pp["h2"] for pp in parts] + [h2s.reshape(n_s, d // 2)]
        bounds = [pp["lo"] * sp for pp in parts] + [n_p, n_p + n_s]
        slots = [slot_of[:, bounds[i]:bounds[i + 1]] for i in range(len(tables))]
        x_sorted = _sc_dispatch(tables, slots, n_slots)
        consumers = halves if l < n_layers - 1 else [(0, bp)]
        back_slots = [slot_of[:, lo * sp:(lo + n) * sp].reshape(-1) for lo, n in consumers]
        back_tg = [tg_p[lo * sp:(lo + n) * sp] for lo, n in consumers]
        slots_s = slot_of[:, n_p:].reshape(-1)
        x_sorted, back_slots, back_tg, slots_s = lax.optimization_barrier((x_sorted, back_slots, back_tg, slots_s))
        ys = _moe_call(x_sorted, block_e, nused, l, w_gate_up, b_gate_up, w_down, b_down)
        y4_parts = []
        for s_flat in back_slots[:-1]:
            y4_parts += _sc_collect(ys, [s_flat])
        y4_last, y4s = _sc_collect(ys, [back_slots[-1], slots_s])
        fuse_p = {(lo, n): (y4.reshape(TOP_K, n * sp, d // 2), tg, mod_p[l, lo:lo + n])
                  for (lo, n), y4, tg in zip(consumers, y4_parts + [y4_last], back_tg)}
        fuse_s = (y4s.reshape(TOP_K, n_s, d // 2), tgs, mod_s[l])
        pools_p.append(jnp.concatenate([pp["npool"] for pp in parts], axis=0))
        convs_p.append(jnp.concatenate([pp["nconv"] for pp in parts], axis=0))
        pools_s.append(npool_s)
        convs_s.append(nconv_s)
        vs.append(v_s)
    yp = None
    (y4, tg, _), = fuse_p.values()
    for pp in parts:
        yp = _combine_call(pp["x"], 0, mod_p[-1, pp["lo"]:pp["lo"] + pp["n"]], y4, tg, final_g, 1, tp, bp, pp["lo"],
                           yp)
    ys_out = _combine_call(xs, 0, mod_s[-1], *fuse_s[:2], final_g, bts, ss, bs, 0)
    return (yp, ys_out, jnp.stack(pools_p), jnp.stack(convs_p), jnp.stack(pools_s), jnp.stack(convs_s),
            jnp.stack(vs))
```

```python
import functools
from typing import NamedTuple

import jax
import jax.numpy as jnp
from jax import lax
from jax.experimental import pallas as pl
from jax.experimental.pallas import tpu as pltpu
from jax.experimental.pallas import tpu_sc as plsc

GMLP_CHUNK = 128
GMLP_HEADS = 4
POOL_WINDOWS = (2, 4, 8, 16)
POOL_STATE = max(POOL_WINDOWS) - 1
CONV_K = 31
CONV_STATE = CONV_K - 1
N_BRANCH = 3
N_EXPERTS = 32
TOP_K = 4
SWIGLU_LIMIT = 7.0
SWIGLU_ALPHA = 1.702
EPS = 1e-6
PAST_LEN = 2048

LANES = 128
SUBLANES = 8
V7X_VMEM_LIMIT_BYTES = 56 * 2**20

POOL_HIST = 16
CONV_HIST = 32
ROUTER_LANES = LANES
TOPK_LANES = 8
MOE_TM = 512
PROMPT_TILE = 256
SAMPLE_BT = 8

F32 = jnp.float32
BF16 = jnp.bfloat16


def _round_up(a, m):
    return (a + m - 1) // m * m


class MixerCfg(NamedTuple):
    bt: int
    t: int
    ch: int
    kg: int
    kp: int
    d: int
    start_pos: int
    emit_v: bool
    fuse_in: bool
    steps: int
    n_tiles: int


def _rms(x, g):
    return x * lax.rsqrt(jnp.mean(x * x, axis=-1, keepdims=True) + EPS) * g


def _ln(x, g, b):
    mu = jnp.mean(x, axis=-1, keepdims=True)
    xc = x - mu
    var = jnp.mean(xc * xc, axis=-1, keepdims=True)
    return xc * lax.rsqrt(var + EPS) * g + b


def _dot(a, b):
    return jnp.dot(a, b, preferred_element_type=F32)


def _pack_rows(x):
    half = x.shape[1] // 2
    hi = lax.bitcast_convert_type(x[:, :half].astype(BF16).astype(F32), jnp.int32)
    lo = lax.bitcast_convert_type(x[:, half:].astype(BF16).astype(F32), jnp.int32)
    return hi | lax.shift_right_logical(lo, 16)


def _unpack_rows(w):
    hi = lax.bitcast_convert_type(w & jnp.int32(-65536), F32)
    lo = lax.bitcast_convert_type(lax.shift_left(w, 16), F32)
    return jnp.concatenate([hi, lo], axis=1)


def _moe_mix(y4_ref, tg):
    f = tg[:, 0:1] * _unpack_rows(y4_ref[0])
    for k in range(1, TOP_K):
        f = f + tg[:, k:k + 1] * _unpack_rows(y4_ref[k])
    return f


def _mixer_kernel(cfg, x_ref, mod_ref, modb_ref, pp_ref, pc_ref, nmg_ref, nfg_ref, win_ref, lnvg_ref, lnvb_ref,
                  wsp_ref, bsf_ref, wao_ref, wpool_ref, bpool_ref, pscale_ref, wdw_ref, bdw_ref,
                  lncg_ref, lncb_ref, wco_ref, wout_ref, wrh_ref, wrl_ref, br_ref, pmat_ref, ltri_ref, cnt0_ref,
                  *rest):
    rest = list(rest)
    if cfg.fuse_in:
        y4_ref, tgp_ref, modp_ref = rest[:3]
        rest = rest[3:]
    xo_ref, h2_ref, ti_ref, tg_ref, rk_ref, cnt_ref, npool_ref, nconv_ref = rest[:8]
    rest = rest[8:]
    if cfg.emit_v:
        v_ref, hp_scr, xc_scr, run_scr, mg_scr, xr_scr = rest
    else:
        hp_scr, xc_scr, run_scr, mg_scr, xr_scr = rest
    bt, t, d = cfg.bt, cfg.t, cfg.d
    m = bt * t
    n_slab = d // LANES
    gd = d // len(POOL_WINDOWS)
    hd = d // GMLP_HEADS
    q = pl.program_id(0)
    j = 0 if cfg.steps == 1 else lax.rem(jnp.minimum(q, cfg.n_tiles - 1), cfg.steps)

    def mod_rows(k, ref=mod_ref):
        r = ref[:, :, k * d:(k + 1) * d]
        if bt == 1:
            return r.reshape(1, d)
        return jnp.broadcast_to(r, (bt, t, d)).reshape(m, d)

    @pl.when(q == 0)
    def _init_carries():
        run_scr[...] = cnt0_ref[...]
        mg_scr[...] = jnp.zeros_like(mg_scr)
        xr_scr[...] = jnp.zeros_like(xr_scr)

    @pl.when(j == 0)
    def _load_state():
        hp_scr[...] = pp_ref[...]
        for bi in range(bt):
            for c in range(n_slab):
                xc_scr[bi, c, 0:CONV_HIST, :] = pc_ref[bi, :, c * LANES:(c + 1) * LANES]

    x_prev = xr_scr[...]
    out_prev = _dot(mg_scr[...], wout_ref[...])

    x = x_ref[...].reshape(m, d)
    if cfg.fuse_in:
        x = x + mod_rows(5, modp_ref) * _moe_mix(y4_ref, tgp_ref[...])
    sh1, sc1 = mod_rows(0), mod_rows(1)
    h = (_rms(x, nmg_ref[...]) * (1.0 + sc1) + sh1).astype(BF16)

    def zcols(lo, hi):
        return _dot(h, win_ref[:, lo:hi])

    off_u, off_v, off_b, off_c, off_g = 0, d, 2 * d, 3 * d, 5 * d
    zc_a = zcols(off_c, off_c + d)
    zc_b = zcols(off_c + d, off_c + 2 * d)

    x_new = x_prev + mod_rows(2, modb_ref) * out_prev
    xo_ref[...] = x_new.reshape(bt, t, d)
    h2 = _rms(x_new, nfg_ref[...]) * (1.0 + mod_rows(4, modb_ref)) + mod_rows(3, modb_ref)
    h2_ref[...] = _pack_rows(h2).reshape(bt, t, d // 2)
    h2_hi = h2.astype(BF16)
    h2_lo = (h2 - h2_hi.astype(F32)).astype(BF16)
    logits = _dot(h2_hi, wrh_ref[...]) + _dot(h2_lo, wrh_ref[...]) + _dot(h2_hi, wrl_ref[...]) + br_ref[...]

    x_c = zc_a * jax.nn.sigmoid(zc_b)
    for bi in range(bt):
        for c in range(n_slab):
            xc_scr[bi, c, CONV_HIST:CONV_HIST + t, :] = x_c[bi * t:(bi + 1) * t, c * LANES:(c + 1) * LANES]

    lane_r = lax.broadcasted_iota(jnp.int32, (m, ROUTER_LANES), 1).astype(F32)
    lane_k = lax.broadcasted_iota(jnp.int32, (m, TOPK_LANES), 1)
    topk = dict(work=logits, ti=jnp.zeros((m, TOPK_LANES), jnp.int32), vals=[], onehots=[])

    def next_topk_round():
        r = len(topk["vals"])
        if r < TOP_K:
            mx = jnp.max(topk["work"], axis=-1, keepdims=True)
            idx = jnp.min(jnp.where(topk["work"] == mx, lane_r, float(ROUTER_LANES)), axis=-1, keepdims=True)
            topk["ti"] = jnp.where(lane_k == r, idx.astype(jnp.int32), topk["ti"])
            topk["vals"].append(mx)
            topk["work"] = jnp.where(lane_r == idx, -jnp.inf, topk["work"])
            topk["onehots"].append((lane_r == idx).astype(F32))

    proj_cols = dict(ga=off_g, gb=off_g + d, gc=off_g + 2 * d, zu=off_u, zv=off_v, zb=off_b)
    pending_proj = list(proj_cols)
    proj = {}

    def next_proj():
        if pending_proj:
            name = pending_proj.pop(0)
            proj[name] = zcols(proj_cols[name], proj_cols[name] + d)

    conv_parts = []
    for bi in range(bt):
        accs = []
        for c in range(n_slab):
            cs = slice(c * LANES, (c + 1) * LANES)
            acc = jnp.broadcast_to(bdw_ref[:, cs], (t, LANES))
            for k in range(CONV_K):
                acc = acc + wdw_ref[k:k + 1, cs] * xc_scr[bi, c, pl.ds(CONV_HIST - CONV_STATE + k, t), :]
            accs.append(acc)
            next_proj()
            next_topk_round()
        conv_parts.append(jnp.concatenate(accs, axis=1))
    for bi in range(bt):
        for c in range(n_slab):
            cs = slice(c * LANES, (c + 1) * LANES)
            tail = xc_scr[bi, c, t:t + CONV_HIST, :]
            nconv_ref[bi, :, cs] = xc_scr[bi, c, pl.ds(t + CONV_HIST - CONV_STATE, CONV_STATE), :]
            xc_scr[bi, c, 0:CONV_HIST, :] = tail
    while pending_proj:
        next_proj()
    while len(topk["vals"]) < TOP_K:
        next_topk_round()
    conv = conv_parts[0] if bt == 1 else jnp.concatenate(conv_parts, axis=0)

    top_vals, onehots = topk["vals"], topk["onehots"]
    exps = [jnp.exp(tv - top_vals[0]) for tv in top_vals]
    denom = exps[0] + exps[1] + exps[2] + exps[3]
    tg = jnp.zeros((m, TOPK_LANES), F32)
    for r in range(TOP_K):
        tg = jnp.where(lane_k == r, exps[r] / denom, tg)
    ti_ref[...] = topk["ti"]
    tg_ref[...] = tg
    base = run_scr[...]
    rk = jnp.zeros((m, TOPK_LANES), jnp.int32)
    new_base = base
    for r in range(TOP_K):
        before = _dot(ltri_ref[...], onehots[r].astype(BF16)) + new_base
        rank = jnp.sum(onehots[r] * before, axis=-1, keepdims=True)
        rk = jnp.where(lane_k == r, rank.astype(jnp.int32), rk)
        new_base = new_base + jnp.sum(onehots[r], axis=0, keepdims=True)
    new_base = jnp.where(q >= 1, new_base, base)
    run_scr[...] = new_base
    rk_ref[...] = rk
    cnt_ref[...] = jnp.broadcast_to(new_base, cnt_ref.shape)

    zb = proj["zb"]
    lane = lax.broadcasted_iota(jnp.int32, (1, d), 1)
    win_lane = jnp.left_shift(2, lane // gd).astype(F32)
    pos1 = (cfg.start_pos + 1 + j * t + lax.broadcasted_iota(jnp.int32, (t, 1), 0)).astype(F32)
    cnt = jnp.minimum(pos1, win_lane)
    pooled_parts = []
    for bi in range(bt):
        zb_b = zb[bi * t:(bi + 1) * t]
        pieces = [hp_scr[bi], zb_b]
        if cfg.kp > POOL_HIST + t:
            pieces = [jnp.zeros((cfg.kp - POOL_HIST - t, d), F32)] + pieces
        full = jnp.concatenate(pieces, axis=0)
        fullb = full.astype(BF16)
        sums = jnp.concatenate(
            [_dot(pmat_ref[g], fullb[:, g * gd:(g + 1) * gd]) for g in range(len(POOL_WINDOWS))], axis=1)
        pooled_parts.append(sums / cnt - zb_b)
        hp_scr[bi] = full[cfg.kp - POOL_HIST:]
        npool_ref[bi] = hp_scr[bi, pl.ds(POOL_HIST - POOL_STATE, POOL_STATE), :]
    pooled = (pooled_parts[0] if bt == 1 else jnp.concatenate(pooled_parts, axis=0)).astype(BF16)
    y_b = jnp.concatenate(
        [_dot(pooled[:, g * gd:(g + 1) * gd], wpool_ref[g]) for g in range(len(POOL_WINDOWS))], axis=1)
    y_b = (y_b + bpool_ref[...]) * pscale_ref[...]

    v = _ln(jax.nn.gelu(proj["zv"]), lnvg_ref[...], lnvb_ref[...])
    if cfg.emit_v:
        v_ref[...] = v.reshape(bt, t, d)
    vb = v.astype(BF16)
    mixed_rows = []
    for r0 in range(0, m, cfg.ch):
        vc = vb[r0:r0 + cfg.ch]
        if cfg.kg > cfg.ch:
            vc = jnp.concatenate([vc, jnp.zeros((cfg.kg - cfg.ch, d), BF16)], axis=0)
        mixed_rows.append(jnp.concatenate(
            [_dot(wsp_ref[hh], vc[:, hh * hd:(hh + 1) * hd]) for hh in range(GMLP_HEADS)], axis=1) + bsf_ref[...])
    mixed = mixed_rows[0] if len(mixed_rows) == 1 else jnp.concatenate(mixed_rows, axis=0)

    y_c = _dot(jax.nn.silu(_ln(conv, lncg_ref[...], lncb_ref[...])).astype(BF16), wco_ref[...])
    u = jax.nn.gelu(proj["zu"])
    y_a = _dot((u * mixed).astype(BF16), wao_ref[...])
    merged = jax.nn.sigmoid(proj["gb"]) * y_b
    merged = merged + jax.nn.sigmoid(proj["ga"]) * y_a
    merged = merged + jax.nn.sigmoid(proj["gc"]) * y_c

    mg_scr[...] = merged.astype(BF16)
    xr_scr[...] = x


def _const_spec(shape):
    nd = len(shape)
    return pl.BlockSpec(shape, lambda q, _nd=nd: (0,) * _nd, pipeline_mode=pl.Buffered(1))


def _pool_band(t, kp):
    col = jnp.arange(kp)[None, :]
    end = (kp - t) + jnp.arange(t)[:, None]
    return jnp.stack([((col <= end) & (col > end - w)) for w in POOL_WINDOWS]).astype(BF16)


def _mixer_call(cfg, x, b0, mod, prev_pool, prev_conv, lw, cnt0, fuse=()):
    _, s, d = x.shape
    b = mod.shape[0]
    bt, t = cfg.bt, cfg.t
    m = bt * t
    steps, n_tiles = cfg.steps, cfg.n_tiles
    assert steps == s // t and n_tiles == (b // bt) * steps and b0 % bt == 0
    assert steps == 1 or t >= max(CONV_STATE, POOL_STATE)
    n_slab = d // LANES

    def front(q):
        return jnp.minimum(q, n_tiles - 1)

    def back(q):
        return jnp.maximum(q - 1, 0)

    def tile_spec(which, width=d):
        return pl.BlockSpec((bt, t, width), lambda q: (which(q) // steps, which(q) % steps, 0))

    def batch_spec(which, rows, width):
        return pl.BlockSpec((bt, rows, width), lambda q: (which(q) // steps, 0, 0))

    def rows_spec(which, width):
        return pl.BlockSpec((m, width), lambda q: (which(q), 0))

    consts = [lw["norm_mix_g"], lw["norm_ffn_g"], lw["w_in"], lw["ln_v_g"], lw["ln_v_b"], lw["w_sp"], lw["bs_full"],
              lw["w_a_out"], lw["w_pool"], lw["b_pool"], lw["pool_scale"], lw["w_dw"], lw["b_dw"], lw["ln_c_g"],
              lw["ln_c_b"], lw["w_c_out"], lw["w_out"], lw["wr_hi"], lw["wr_lo"], lw["b_router"], lw["pmat"],
              lw["ltri"], cnt0]
    x_spec = pl.BlockSpec((bt, t, d), lambda q: (front(q) // steps + b0 // bt, front(q) % steps, 0))
    in_specs = [x_spec, batch_spec(front, 1, 6 * d), batch_spec(back, 1, 6 * d),
                batch_spec(front, POOL_HIST, d), batch_spec(front, CONV_HIST, d)]
    in_specs += [_const_spec(c.shape) for c in consts]
    if cfg.fuse_in:
        in_specs += [pl.BlockSpec((TOP_K, m, d // 2), lambda q: (0, front(q), 0)),
                     rows_spec(front, TOPK_LANES), batch_spec(front, 1, 6 * d)]
    out_shape = [jax.ShapeDtypeStruct((b, s, d), F32), jax.ShapeDtypeStruct((b, s, d // 2), jnp.int32),
                 jax.ShapeDtypeStruct((b * s, TOPK_LANES), jnp.int32), jax.ShapeDtypeStruct((b * s, TOPK_LANES), F32),
                 jax.ShapeDtypeStruct((b * s, TOPK_LANES), jnp.int32),
                 jax.ShapeDtypeStruct((SUBLANES, ROUTER_LANES), F32),
                 jax.ShapeDtypeStruct((b, POOL_STATE, d), F32), jax.ShapeDtypeStruct((b, CONV_STATE, d), F32)]
    out_specs = [tile_spec(back), tile_spec(back, d // 2),
                 rows_spec(back, TOPK_LANES), rows_spec(back, TOPK_LANES), rows_spec(back, TOPK_LANES),
                 pl.BlockSpec((SUBLANES, ROUTER_LANES), lambda q: (0, 0)),
                 batch_spec(front, POOL_STATE, d), batch_spec(front, CONV_STATE, d)]
    if cfg.emit_v:
        out_shape.append(jax.ShapeDtypeStruct((b, s, d), F32))
        out_specs.append(tile_spec(front))
    return pl.pallas_call(
        functools.partial(_mixer_kernel, cfg),
        grid=(n_tiles + 1,),
        in_specs=in_specs,
        out_specs=out_specs,
        out_shape=out_shape,
        scratch_shapes=[pltpu.VMEM((bt, POOL_HIST, d), F32),
                        pltpu.VMEM((bt, n_slab, CONV_HIST + t, LANES), F32),
                        pltpu.VMEM((1, ROUTER_LANES), F32),
                        pltpu.VMEM((m, d), BF16), pltpu.VMEM((m, d), F32)],
        compiler_params=pltpu.CompilerParams(dimension_semantics=("arbitrary",),
                                             vmem_limit_bytes=V7X_VMEM_LIMIT_BYTES),
        name=f"mixer_t{t}",
    )(x, mod, mod, prev_pool, prev_conv, *consts, *fuse)


def _ada_kernel(c_ref, w_ref, b_ref, o_ref):
    o_ref[0] = _dot(jax.nn.silu(c_ref[...]).astype(BF16), w_ref[0].astype(BF16)) + b_ref[0]


def _ada_call(c_all, w_ada, b_ada):
    n_layers, d, six_d = w_ada.shape
    rows = c_all.shape[0]
    bn = six_d // 6
    return pl.pallas_call(
        _ada_kernel,
        grid=(n_layers, six_d // bn),
        in_specs=[pl.BlockSpec((rows, d), lambda l, n: (0, 0)),
                  pl.BlockSpec((1, d, bn), lambda l, n: (l, 0, n)),
                  pl.BlockSpec((1, 1, bn), lambda l, n: (l, 0, n))],
        out_specs=pl.BlockSpec((1, rows, bn), lambda l, n: (l, 0, n)),
        out_shape=jax.ShapeDtypeStruct((n_layers, rows, six_d), F32),
        compiler_params=pltpu.CompilerParams(dimension_semantics=("arbitrary", "arbitrary")),
        name="adaln",
    )(c_all, w_ada, b_ada.reshape(n_layers, 1, six_d))


def _moe_kernel(be_ref, nused_ref, x_ref, wgu_ref, bgu_ref, wdn_ref, bdn_ref, y_ref, wgu_bf, wdn_bf):
    i = pl.program_id(0)
    nused = nused_ref[0]
    dff = wdn_bf.shape[0]

    @pl.when(i < nused)
    def _body():
        changed = jnp.logical_or(i == 0, be_ref[i] != be_ref[jnp.maximum(i - 1, 0)])

        @pl.when(changed)
        def _cast_weights():
            wgu_bf[...] = wgu_ref[0].astype(BF16)
            wdn_bf[...] = wdn_ref[0].astype(BF16)

        half = x_ref.shape[0] // 2
        gus = [_dot(_unpack_rows(x_ref[r0:r0 + half, :]).astype(BF16), wgu_bf[...]) + bgu_ref[0]
               for r0 in (0, half)]
        for r0, gu in zip((0, half), gus):
            g = jnp.minimum(gu[:, :dff], SWIGLU_LIMIT)
            u = jnp.clip(gu[:, dff:], -SWIGLU_LIMIT, SWIGLU_LIMIT)
            act = (u + 1.0) * g * jax.nn.sigmoid(SWIGLU_ALPHA * g)
            y_ref[r0:r0 + half, :] = _pack_rows(_dot(act.astype(BF16), wdn_bf[...]) + bdn_ref[0])

    @pl.when(i >= nused)
    def _unused_block():
        y_ref[...] = jnp.zeros_like(y_ref)


def _moe_call(xs, block_e, nused, layer, w_gu_all, b_gu_all, w_dn_all, b_dn_all):
    n_blocks = block_e.shape[0]
    n_layers, n_exp, d, two_f = w_gu_all.shape
    dff = two_f // 2
    tm = MOE_TM
    e0 = layer * n_exp
    w_gu = w_gu_all.reshape(n_layers * n_exp, d, two_f)
    b_gu = b_gu_all.reshape(n_layers * n_exp, 1, two_f)
    w_dn = w_dn_all.reshape(n_layers * n_exp, dff, d)
    b_dn = b_dn_all.reshape(n_layers * n_exp, 1, d)
    grid_spec = pltpu.PrefetchScalarGridSpec(
        num_scalar_prefetch=2,
        grid=(n_blocks,),
        in_specs=[
            pl.BlockSpec((tm, d // 2), lambda i, be, nu: (i, 0)),
            pl.BlockSpec((1, d, two_f), lambda i, be, nu: (e0 + be[i], 0, 0)),
            pl.BlockSpec((1, 1, two_f), lambda i, be, nu: (e0 + be[i], 0, 0)),
            pl.BlockSpec((1, dff, d), lambda i, be, nu: (e0 + be[i], 0, 0)),
            pl.BlockSpec((1, 1, d), lambda i, be, nu: (e0 + be[i], 0, 0)),
        ],
        out_specs=pl.BlockSpec((tm, d // 2), lambda i, be, nu: (i, 0)),
        scratch_shapes=[pltpu.VMEM((d, two_f), BF16), pltpu.VMEM((dff, d), BF16)],
    )
    return pl.pallas_call(
        _moe_kernel,
        grid_spec=grid_spec,
        out_shape=jax.ShapeDtypeStruct((n_blocks * tm, d // 2), jnp.int32),
        compiler_params=pltpu.CompilerParams(dimension_semantics=("arbitrary",),
                                             vmem_limit_bytes=V7X_VMEM_LIMIT_BYTES),
        name="moe_experts",
    )(block_e, nused, xs, w_gu, b_gu, w_dn, b_dn)


def _route(top_i, rank, counts):
    tm = MOE_TM
    n_all = top_i.shape[0]
    n_blocks = n_all * TOP_K // tm + N_EXPERTS
    padded = (counts + tm - 1) // tm * tm
    pad_ends = jnp.cumsum(padded)
    pad_starts = pad_ends - padded
    blk = jnp.arange(n_blocks, dtype=jnp.int32)[:, None] * tm
    block_e = jnp.minimum(jnp.sum((pad_ends[None, :] <= blk).astype(jnp.int32), axis=1), N_EXPERTS - 1)
    nused = (pad_ends[-1] // tm).astype(jnp.int32).reshape(1)
    onehot = top_i[:, :, None] == jnp.arange(N_EXPERTS, dtype=jnp.int32)[None, None, :]
    slot_of = jnp.sum(jnp.where(onehot, pad_starts[None, None, :], 0), axis=-1) + rank
    return slot_of.T.astype(jnp.int32), block_e.astype(jnp.int32), nused


SC_CORES = 2
SC_SUBCORES = 16
SC_WORKERS = SC_CORES * SC_SUBCORES
SC_CHUNK = 64


def _sc_worker_id():
    return lax.axis_index("s") * SC_CORES + lax.axis_index("c")


def _sc_plan(n_rows, split=False):
    per_w = n_rows // SC_WORKERS
    ch = min(SC_CHUNK, per_w // 2 if split else per_w)
    n_ch = per_w // ch
    assert per_w * SC_WORKERS == n_rows and n_ch * ch == per_w
    return per_w, ch, n_ch, (n_ch >= 2 and n_ch % 2 == 0)


def _sc_dispatch(tables, slots, n_slots):
    nt = len(tables)
    d, dtype = tables[0].shape[1], tables[0].dtype
    plans = [_sc_plan(h.shape[0]) for h in tables]

    buf_shapes = sorted({(2 if piped else 1, ch) for (_, ch, _, piped) in plans})

    def body(*refs):
        h_refs, idx_refs, out_hbm = refs[:nt], refs[nt:2 * nt], refs[2 * nt]
        scratch = refs[2 * nt + 1:]
        idx_vs, shared, (rsem, wsem) = scratch[:nt], scratch[nt:-2], scratch[-2:]
        bufs = [shared[buf_shapes.index((2 if piped else 1, ch))] for (_, ch, _, piped) in plans]
        wid = _sc_worker_id()
        for h_hbm, idx_hbm, idx_v, buf, (tpw, ch, n_ch, piped) in zip(h_refs, idx_refs, idx_vs, bufs, plans):
            pltpu.sync_copy(idx_hbm.at[wid], idx_v)

            def read(c, b, h_hbm=h_hbm, buf=buf, tpw=tpw, ch=ch):
                return pltpu.make_async_copy(h_hbm.at[pl.ds(wid * tpw + c * ch, ch)], buf.at[b], rsem.at[b])

            def write(c, k, b, buf=buf, idx_v=idx_v):
                return pltpu.make_async_copy(buf.at[b], out_hbm.at[idx_v.at[c * TOP_K + k]], wsem.at[b])

            if piped:
                read(0, 0).start()

                @pl.loop(0, n_ch, step=2)
                def _(c0, read=read, write=write, n_ch=n_ch):
                    for b in range(2):
                        c = c0 + b

                        @pl.when(c + 1 < n_ch)
                        def _():
                            @pl.when(c >= 1)
                            def _():
                                for k in range(TOP_K):
                                    write(c - 1, k, 1 - b).wait()
                            read(c + 1, 1 - b).start()

                        read(c, b).wait()
                        for k in range(TOP_K):
                            write(c, k, b).start()

                for c in (n_ch - 2, n_ch - 1):
                    for k in range(TOP_K):
                        write(c, k, c % 2).wait()
            else:
                for c in range(n_ch):
                    cp_in = read(c, 0)
                    cp_in.start()
                    cp_in.wait()
                    for k in range(TOP_K):
                        write(c, k, 0).start()
                    for k in range(TOP_K):
                        write(c, k, 0).wait()

    scratch_types = [pltpu.VMEM((n_ch * TOP_K, ch), jnp.int32) for (_, ch, n_ch, _) in plans]
    scratch_types += [pltpu.VMEM((nbuf, ch, d), dtype) for (nbuf, ch) in buf_shapes]
    scratch_types += [pltpu.SemaphoreType.DMA((2,)), pltpu.SemaphoreType.DMA((2,))]
    call = pl.kernel(
        body,
        out_type=jax.ShapeDtypeStruct((n_slots, d), dtype),
        mesh=plsc.VectorSubcoreMesh(core_axis_name="c", subcore_axis_name="s"),
        scratch_types=scratch_types,
        name="sc_dispatch",
    )
    idx = [s.reshape(TOP_K, SC_WORKERS, n_ch, ch).transpose(1, 2, 0, 3).reshape(SC_WORKERS, n_ch * TOP_K, ch)
           for s, (_, ch, n_ch, _) in zip(slots, plans)]
    return call(*tables, *idx)


def _sc_collect(ys, slots):
    nt = len(slots)
    d, dtype = ys.shape[1], ys.dtype
    plans = [_sc_plan(s.size, split=True) for s in slots]
    assert all(piped for (_, _, _, piped) in plans)

    def body(*refs):
        ys_hbm, idx_refs, out_refs = refs[0], refs[1:1 + nt], refs[1 + nt:1 + 2 * nt]
        scratch = refs[1 + 2 * nt:]
        idx_vs, bufs, (gsem, wsem) = scratch[:nt], scratch[nt:2 * nt], scratch[2 * nt:]
        wid = _sc_worker_id()
        for idx_hbm, out_hbm, idx_v, rows_v, (per_w, ch, n_ch, _) in zip(idx_refs, out_refs, idx_vs, bufs, plans):
            base = wid * per_w
            pltpu.sync_copy(idx_hbm.at[pl.ds(base, per_w)], idx_v)

            def gather(g, b, idx_v=idx_v, rows_v=rows_v, ch=ch):
                return pltpu.make_async_copy(ys_hbm.at[idx_v.at[pl.ds(g * ch, ch)]], rows_v.at[b], gsem.at[b])

            def put(g, b, out_hbm=out_hbm, rows_v=rows_v, base=base, ch=ch):
                return pltpu.make_async_copy(rows_v.at[b], out_hbm.at[pl.ds(base + g * ch, ch)], wsem.at[b])

            gather(0, 0).start()

            @pl.loop(0, n_ch, step=2)
            def _(g0, gather=gather, put=put, n_ch=n_ch):
                for b in range(2):
                    g = g0 + b

                    @pl.when(g + 1 < n_ch)
                    def _():
                        @pl.when(g >= 1)
                        def _():
                            put(g - 1, 1 - b).wait()
                        gather(g + 1, 1 - b).start()

                    gather(g, b).wait()
                    put(g, b).start()

            for g in (n_ch - 2, n_ch - 1):
                put(g, g % 2).wait()

    call = pl.kernel(
        body,
        out_type=tuple(jax.ShapeDtypeStruct((s.size, d), dtype) for s in slots),
        mesh=plsc.VectorSubcoreMesh(core_axis_name="c", subcore_axis_name="s"),
        scratch_types=[pltpu.VMEM((per_w,), jnp.int32) for (per_w, _, _, _) in plans]
        + [pltpu.VMEM((2, ch, d), dtype) for (_, ch, _, _) in plans]
        + [pltpu.SemaphoreType.DMA((2,)), pltpu.SemaphoreType.DMA((2,))],
        name="sc_collect",
    )
    out = call(ys, *[s.reshape(s.size) for s in slots])
    return out if isinstance(out, (tuple, list)) else (out,)


def _combine_kernel(x_ref, mod_ref, y_ref, tg_ref, fg_ref, *rest):
    o_ref = rest[-1]
    bt, t, d = x_ref.shape
    m = bt * t
    g2 = mod_ref[:, :, 5 * d:6 * d]
    g2 = g2.reshape(1, d) if bt == 1 else jnp.broadcast_to(g2, (bt, t, d)).reshape(m, d)
    out = x_ref[...].reshape(m, d) + g2 * _moe_mix(y_ref, tg_ref[...])
    o_ref[...] = _rms(out, fg_ref[...]).reshape(bt, t, d)


def _combine_call(x_new, b0, mod, y4, tg, final_g, bt, t, out_batch, out_b0, out_prev=None):
    _, s, d = x_new.shape
    b = mod.shape[0]
    m = bt * t
    steps = s // t
    ob = out_b0 // bt
    in_specs = [pl.BlockSpec((bt, t, d), lambda bi, j: (bi + b0 // bt, j, 0)),
                pl.BlockSpec((bt, 1, 6 * d), lambda bi, j: (bi, 0, 0)),
                pl.BlockSpec((TOP_K, m, d // 2), lambda bi, j: (0, (bi + ob) * steps + j, 0)),
                pl.BlockSpec((m, TOPK_LANES), lambda bi, j: ((bi + ob) * steps + j, 0)),
                pl.BlockSpec((1, d), lambda bi, j: (0, 0))]
    operands = [x_new, mod, y4, tg, final_g]
    aliases = {}
    if out_prev is not None:
        in_specs.append(pl.BlockSpec(memory_space=pl.ANY))
        operands.append(out_prev)
        aliases = {len(operands) - 1: 0}
    return pl.pallas_call(
        _combine_kernel,
        grid=(b // bt, steps),
        in_specs=in_specs,
        out_specs=pl.BlockSpec((bt, t, d), lambda bi, j: (bi + out_b0 // bt, j, 0)),
        out_shape=jax.ShapeDtypeStruct((out_batch, s, d), F32),
        input_output_aliases=aliases,
        cost_estimate=pl.CostEstimate(flops=2 * (TOP_K + 3) * b * s * d, transcendentals=b * s,
                                      bytes_accessed=b * s * d * (4 + 4 + TOP_K * 2)),
        compiler_params=pltpu.CompilerParams(dimension_semantics=("arbitrary", "arbitrary")),
        name=f"combine_t{t}",
    )(*operands)


def _layer_weights(l, p, cfgs):
    d = p["w_out"].shape[-1]
    row = lambda v: v[l].reshape(1, d)
    wr = p["w_router"][l]
    wr_pad = jnp.zeros((d, ROUTER_LANES), F32).at[:, :N_EXPERTS].set(wr)
    wr_hi = wr_pad.astype(BF16)
    wr_lo = (wr_pad - wr_hi.astype(F32)).astype(BF16)
    b_router = jnp.full((1, ROUTER_LANES), -1e30, F32).at[0, :N_EXPERTS].set(p["b_router"][l])
    mask = jnp.tril(jnp.ones((GMLP_CHUNK, GMLP_CHUNK), dtype=bool))
    w_sp_full = jnp.where(mask[None], p["w_spatial"][l], 0)
    hd = d // GMLP_HEADS
    base = dict(norm_mix_g=row(p["norm_mix_g"]), norm_ffn_g=row(p["norm_ffn_g"]), w_in=p["w_in"][l].astype(BF16),
                ln_v_g=row(p["ln_v_g"]), ln_v_b=row(p["ln_v_b"]), w_a_out=p["w_a_out"][l].astype(BF16),
                w_pool=p["w_pool"][l].astype(BF16), b_pool=row(p["b_pool"]), pool_scale=row(p["pool_scale"]),
                w_dw=jnp.zeros((_round_up(CONV_K, SUBLANES), d), F32).at[:CONV_K].set(p["w_dw"][l]),
                b_dw=row(p["b_dw"]), ln_c_g=row(p["ln_c_g"]), ln_c_b=row(p["ln_c_b"]),
                w_c_out=p["w_c_out"][l].astype(BF16), w_out=p["w_out"][l].astype(BF16),
                wr_hi=wr_hi, wr_lo=wr_lo, b_router=b_router)
    out = []
    for cfg in cfgs:
        w_sp = jnp.zeros((GMLP_HEADS, cfg.ch, cfg.kg), F32).at[:, :, :cfg.ch].set(w_sp_full[:, :cfg.ch, :cfg.ch])
        bs_full = jnp.repeat(p["b_spatial"][l][:, :cfg.ch].T, hd, axis=1)
        m = cfg.bt * cfg.t
        ltri = (jnp.arange(m)[:, None] > jnp.arange(m)[None, :]).astype(BF16)
        out.append(dict(base, w_sp=w_sp.astype(BF16), bs_full=bs_full, pmat=_pool_band(cfg.t, cfg.kp), ltri=ltri))
    return out


def _pad_front(state, rows):
    pad = rows - state.shape[-2]
    return jnp.pad(state, ((0, 0),) * (state.ndim - 2) + ((pad, 0), (0, 0)))


def kernel(x_prompt, x_sample, c_prompt, c_sample, state_pool, state_conv, norm_mix_g, norm_ffn_g, w_ada, b_ada, w_in, ln_v_g, ln_v_b, w_spatial, b_spatial, w_a_out, w_pool, b_pool, pool_scale, w_dw, b_dw, ln_c_g, ln_c_b, w_c_out, w_out, w_router, b_router, w_gate_up, b_gate_up, w_down, b_down, final_norm_g):
    p = dict(norm_mix_g=norm_mix_g, norm_ffn_g=norm_ffn_g, w_in=w_in, ln_v_g=ln_v_g, ln_v_b=ln_v_b,
             w_spatial=w_spatial, b_spatial=b_spatial, w_a_out=w_a_out, w_pool=w_pool, b_pool=b_pool,
             pool_scale=pool_scale, w_dw=w_dw, b_dw=b_dw, ln_c_g=ln_c_g, ln_c_b=ln_c_b, w_c_out=w_c_out,
             w_out=w_out, w_router=w_router, b_router=b_router)
    n_layers = w_in.shape[0]
    bp, sp, d = x_prompt.shape
    bs, ss, _ = x_sample.shape
    tp = min(PROMPT_TILE, sp)
    bts = min(SAMPLE_BT, bs)
    cfg_p = MixerCfg(bt=1, t=tp, ch=GMLP_CHUNK, kg=GMLP_CHUNK, kp=_round_up(POOL_HIST + tp, LANES), d=d,
                     start_pos=0, emit_v=False, fuse_in=False, steps=sp // tp, n_tiles=bp * (sp // tp))
    cfg_s = MixerCfg(bt=bts, t=ss, ch=ss, kg=_round_up(ss, LANES), kp=_round_up(POOL_HIST + ss, LANES), d=d,
                     start_pos=PAST_LEN, emit_v=True, fuse_in=False, steps=1, n_tiles=bs // bts)
    n_p, n_s = bp * sp, bs * ss
    n_slots = (n_p + n_s) * TOP_K + N_EXPERTS * MOE_TM

    mod_all = _ada_call(jnp.concatenate([c_prompt, c_sample], axis=0), w_ada, b_ada)
    mod_p = mod_all[:, :bp].reshape(n_layers, bp, 1, 6 * d)
    mod_s = mod_all[:, bp:].reshape(n_layers, bs, 1, 6 * d)
    final_g = final_norm_g.reshape(1, d)

    hb = bp // 2
    halves = [(0, hb), (hb, bp - hb)]
    steps_p = sp // tp
    xs = x_sample
    parts = [dict(lo=0, n=bp, x=x_prompt)]
    zero_pool = jnp.zeros((bp, POOL_HIST, d), F32)
    zero_conv = jnp.zeros((bp, CONV_HIST, d), F32)
    zero_cnt = jnp.zeros((1, ROUTER_LANES), F32)
    fuse_p, fuse_s = {}, ()
    pools_p, convs_p, pools_s, convs_s, vs = [], [], [], [], []

    def holder(lo):
        return next(pp for pp in parts if pp["lo"] <= lo < pp["lo"] + pp["n"])

    for l in range(n_layers):
        fused = l > 0
        if fused:
            p, _ = lax.optimization_barrier((p, cnt_all))
        lw_p, lw_s = _layer_weights(l, p, (cfg_p, cfg_s))
        cnt = zero_cnt
        new_parts = []
        for lo, n in (halves if fused else [(0, bp)]):
            src = holder(lo)
            x_new, h2, ti, tg, rk, cnt, npool, nconv = _mixer_call(
                cfg_p._replace(fuse_in=fused, n_tiles=n * steps_p), src["x"], lo - src["lo"], mod_p[l, lo:lo + n],
                zero_pool[lo:lo + n], zero_conv[lo:lo + n], lw_p, cnt[0:1], fuse_p.get((lo, n), ()))
            new_parts.append(dict(lo=lo, n=n, x=x_new, h2=h2.reshape(n * sp, d // 2), ti=ti, tg=tg, rk=rk,
                                  npool=npool, nconv=nconv))
        parts = new_parts
        xs, h2s, tis, tgs, rks, cnt_all, npool_s, nconv_s, v_s = _mixer_call(
            cfg_s._replace(fuse_in=fused), xs, 0, mod_s[l], _pad_front(state_pool[l], POOL_HIST),
            _pad_front(state_conv[l], CONV_HIST), lw_s, cnt[0:1], fuse_s)
        top_i = jnp.concatenate([pp["ti"] for pp in parts] + [tis], axis=0)[:, :TOP_K]
        rank = jnp.concatenate([pp["rk"] for pp in parts] + [rks], axis=0)[:, :TOP_K]
        tg_p = jnp.concatenate([pp["tg"] for pp in parts], axis=0)
        slot_of, block_e, nused = _route(top_i, rank, cnt_all[0, :N_EXPERTS].astype(jnp.int32))
        tables = [pp["h2"] for pp in parts] + [h2s.reshape(n_s, d // 2)]
        bounds = [pp["lo"] * sp for pp in parts] + [n_p, n_p + n_s]
        slots = [slot_of[:, bounds[i]:bounds[i + 1]] for i in range(len(tables))]
        x_sorted = _sc_dispatch(tables, slots, n_slots)
        consumers = halves if l < n_layers - 1 else [(0, bp)]
        back_slots = [slot_of[:, lo * sp:(lo + n) * sp].reshape(-1) for lo, n in consumers]
        back_tg = [tg_p[lo * sp:(lo + n) * sp] for lo, n in consumers]
        slots_s = slot_of[:, n_p:].reshape(-1)
        x_sorted, back_slots, back_tg, slots_s = lax.optimization_barrier((x_sorted, back_slots, back_tg, slots_s))
        ys = _moe_call(x_sorted, block_e, nused, l, w_gate_up, b_gate_up, w_down, b_down)
        y4_parts = []
        for s_flat in back_slots[:-1]:
            y4_parts += _sc_collect(ys, [s_flat])
        y4_last, y4s = _sc_collect(ys, [back_slots[-1], slots_s])
        fuse_p = {(lo, n): (y4.reshape(TOP_K, n * sp, d // 2), tg, mod_p[l, lo:lo + n])
                  for (lo, n), y4, tg in zip(consumers, y4_parts + [y4_last], back_tg)}
        fuse_s = (y4s.reshape(TOP_K, n_s, d // 2), tgs, mod_s[l])
        pools_p.append(jnp.concatenate([pp["npool"] for pp in parts], axis=0))
        convs_p.append(jnp.concatenate([pp["nconv"] for pp in parts], axis=0))
        pools_s.append(npool_s)
        convs_s.append(nconv_s)
        vs.append(v_s)
    yp = None
    (y4, tg, _), = fuse_p.values()
    for pp in parts:
        yp = _combine_call(pp["x"], 0, mod_p[-1, pp["lo"]:pp["lo"] + pp["n"]], y4, tg, final_g, 1, tp, bp, pp["lo"],
                           yp)
    ys_out = _combine_call(xs, 0, mod_s[-1], *fuse_s[:2], final_g, bts, ss, bs, 0)
    return (yp, ys_out, jnp.stack(pools_p), jnp.stack(convs_p), jnp.stack(pools_s), jnp.stack(convs_s),
            jnp.stack(vs))
```
